```python
import math
import jax
import jax.numpy as jnp
from jax import lax
import numpy as np

D_MODEL = 2048
BATCH = 4
SEQ = 4096
DEPTH = 1

GDN_HEADS = 8
GDN_HEAD_DIM = 128
GDN_WIDTH = GDN_HEADS * GDN_HEAD_DIM
GDN_CONV = 4
GDN_CHUNK = 64

MOBA_HEADS = 8
MOBA_HEAD_DIM = 128
MOBA_WIDTH = MOBA_HEADS * MOBA_HEAD_DIM
MOBA_BLOCK = 256
MOBA_TOPK = 3
MOBA_QBLOCK = 64
ROPE_THETA = 500000.0
ROPE_DIM = MOBA_HEAD_DIM // 4

D_FF = 5632
FFN_CONV = 3

NORM_EPS = 1e-6
NEG_INF = -1e30

IN_SPLITS = (3 * GDN_WIDTH, GDN_HEADS, GDN_HEADS, GDN_WIDTH, 3 * MOBA_WIDTH, D_MODEL, D_MODEL)
IN_DIM = sum(IN_SPLITS)

kernel_name = "hybrid_gdn_moba_convffn"


def rms_norm(x, w):
    xf = x.astype(jnp.float32)
    y = xf * lax.rsqrt(jnp.mean(xf * xf, axis=-1, keepdims=True) + NORM_EPS)
    return (y * w.astype(jnp.float32)).astype(x.dtype)


def l2norm(x):
    return x * lax.rsqrt(jnp.sum(x * x, axis=-1, keepdims=True) + 1e-6)


def causal_dwconv(x, w, bias=None):
    width, ch = w.shape
    y = lax.conv_general_dilated(
        x, w[:, None, :].astype(x.dtype), window_strides=(1,), padding=[(width - 1, 0)],
        dimension_numbers=('NWC', 'WIO', 'NWC'), feature_group_count=ch)
    if bias is not None:
        y = y + bias.astype(x.dtype)
    return y


def partial_rope(x, positions):
    half = ROPE_DIM // 2
    inv_freq = ROPE_THETA ** (-jnp.arange(half, dtype=jnp.float32) / half)
    ang = positions.astype(jnp.float32)[..., None] * inv_freq
    cos = jnp.cos(ang)[:, :, None, :]
    sin = jnp.sin(ang)[:, :, None, :]
    xr = x[..., :ROPE_DIM].astype(jnp.float32)
    x1, x2 = xr[..., :half], xr[..., half:]
    rot = jnp.concatenate([x1 * cos - x2 * sin, x2 * cos + x1 * sin], axis=-1).astype(x.dtype)
    return jnp.concatenate([rot, x[..., ROPE_DIM:]], axis=-1)


def chunk_gated_delta_rule(q, k, v, g, beta):
    B, S, H, Dk = q.shape
    Dv = v.shape[-1]
    C = GDN_CHUNK
    N = S // C

    def chunks(t):
        return jnp.moveaxis(t.reshape((B, N, C, H) + t.shape[3:]), 3, 1)

    q = chunks(q) * (Dk ** -0.5)
    k = chunks(k)
    v = chunks(v)
    g = chunks(g)
    beta = chunks(beta)
    gc = jnp.cumsum(g, axis=-1)
    r = jnp.arange(C)
    causal = r[:, None] >= r[None, :]
    strict = r[:, None] > r[None, :]
    gamma = jnp.exp(jnp.where(causal, gc[..., :, None] - gc[..., None, :], -jnp.inf))
    kb = k * beta[..., None]
    a_mat = jnp.where(strict, jnp.einsum('bhncd,bhnsd->bhncs', kb, k) * gamma, 0.0)
    eye = jnp.eye(C, dtype=q.dtype)
    t_mat = lax.linalg.triangular_solve(eye + a_mat, jnp.broadcast_to(eye, a_mat.shape),
                                        left_side=True, lower=True, unit_diagonal=True)
    u = jnp.einsum('bhncs,bhnsv->bhncv', t_mat, v * beta[..., None])
    w = jnp.einsum('bhncs,bhnsk->bhnck', t_mat, kb * jnp.exp(gc)[..., None])
    qk = jnp.einsum('bhncd,bhnsd->bhncs', q, k) * gamma
    qg = q * jnp.exp(gc)[..., None]
    kd = k * jnp.exp(gc[..., -1:] - gc)[..., None]
    glast = jnp.exp(gc[..., -1])

    def step(state, xs):
        u_n, w_n, qg_n, qk_n, kd_n, gl_n = xs
        v_new = u_n - jnp.einsum('bhck,bhkv->bhcv', w_n, state)
        o = jnp.einsum('bhck,bhkv->bhcv', qg_n, state) + jnp.einsum('bhcs,bhsv->bhcv', qk_n, v_new)
        state = state * gl_n[..., None, None] + jnp.einsum('bhck,bhcv->bhkv', kd_n, v_new)
        return state, o

    xs = tuple(jnp.moveaxis(t, 2, 0) for t in (u, w, qg, qk, kd, glast))
    s0 = jnp.zeros((B, H, Dk, Dv), q.dtype)
    _, o = lax.scan(step, s0, xs)
    return o.transpose(1, 0, 3, 2, 4).reshape(B, S, H, Dv)


def gated_deltanet(qkv, beta_logit, a_logit, z, conv_w, a_log, dt_bias, norm_w):
    B, S, _ = qkv.shape
    dtype = qkv.dtype
    f32 = jnp.float32
    qkv = jax.nn.silu(causal_dwconv(qkv, conv_w))
    q, k, v = jnp.split(qkv.astype(f32), 3, axis=-1)
    shp = (B, S, GDN_HEADS, GDN_HEAD_DIM)
    q = l2norm(q.reshape(shp))
    k = l2norm(k.reshape(shp))
    v = v.reshape(shp)
    beta = jax.nn.sigmoid(beta_logit.astype(f32))
    g = -jnp.exp(a_log.astype(f32)) * jax.nn.softplus(a_logit.astype(f32) + dt_bias.astype(f32))
    o = chunk_gated_delta_rule(q, k, v, g, beta)
    o = rms_norm(o, norm_w) * jax.nn.silu(z.astype(f32).reshape(shp))
    return o.reshape(B, S, GDN_WIDTH).astype(dtype)


def moba_attention(q, k, v):
    B, S, H, D = q.shape
    BLK = MOBA_BLOCK
    NB = -(-S // BLK)
    Sp = NB * BLK
    K = min(MOBA_TOPK, NB)
    QB = MOBA_QBLOCK
    NQ = Sp // QB
    pad = ((0, 0), (0, Sp - S), (0, 0), (0, 0))
    qh = jnp.pad(q, pad).transpose(0, 2, 1, 3) * (D ** -0.5)
    kb = jnp.pad(k, pad).transpose(0, 2, 1, 3).reshape(B, H, NB, BLK, D)
    vb = jnp.pad(v, pad).transpose(0, 2, 1, 3).reshape(B, H, NB, BLK, D)
    k_mean = jnp.mean(kb.astype(jnp.float32), axis=3)
    gate = jnp.einsum('bhsd,bhnd->bhsn', qh.astype(jnp.float32), k_mean)
    q_blk = jnp.arange(Sp) // BLK
    fully_past = jnp.arange(NB)[None, :] < q_blk[:, None]
    gate = jnp.where(fully_past, gate, -jnp.inf)
    _, sel = lax.top_k(gate, K)
    qs = qh.reshape(B, H, NQ, QB, D).transpose(0, 2, 1, 3, 4).reshape(B * NQ, H, QB, D)
    ss = sel.reshape(B, H, NQ, QB, K).transpose(0, 2, 1, 3, 4).reshape(B * NQ, H, QB, K)
    b_idx = jnp.repeat(jnp.arange(B, dtype=jnp.int32), NQ)
    n_idx = jnp.tile(jnp.arange(NQ, dtype=jnp.int32), B)
    head = jnp.arange(H)[:, None, None]

    def query_block(args):
        qn, sn, b, n = args
        kb_b = kb[b]
        vb_b = vb[b]
        k_sel = kb_b[head, sn]
        v_sel = vb_b[head, sn]
        t = n * QB + jnp.arange(QB)
        own = (n * QB) // BLK
        k_own = lax.dynamic_index_in_dim(kb_b, own, axis=1, keepdims=False)
        v_own = lax.dynamic_index_in_dim(vb_b, own, axis=1, keepdims=False)
        s_past = jnp.einsum('hqd,hqkld->hqkl', qn, k_sel, preferred_element_type=jnp.float32)
        slot_ok = jnp.arange(K)[None, :] < (t // BLK)[:, None]
        s_past = jnp.where(slot_ok[None, :, :, None], s_past, NEG_INF).reshape(H, QB, K * BLK)
        s_own = jnp.einsum('hqd,hld->hql', qn, k_own, preferred_element_type=jnp.float32)
        own_ok = (own * BLK + jnp.arange(BLK))[None, :] <= t[:, None]
        s_own = jnp.where(own_ok[None], s_own, NEG_INF)
        p = jax.nn.softmax(jnp.concatenate([s_past, s_own], axis=-1), axis=-1).astype(v.dtype)
        p_past = p[..., :K * BLK].reshape(H, QB, K, BLK)
        p_own = p[..., K * BLK:]
        return (jnp.einsum('hqkl,hqkld->hqd', p_past, v_sel)
                + jnp.einsum('hql,hld->hqd', p_own, v_own))

    out = lax.map(query_block, (qs, ss, b_idx, n_idx))
    out = out.reshape(B, NQ, H, QB, D).transpose(0, 1, 3, 2, 4).reshape(B, Sp, H, D)
    return out[:, :S]


def split_columns(proj):
    offs = [int(o) for o in np.cumsum(IN_SPLITS)[:-1]]
    return jnp.split(proj, offs, axis=-1)


def setup_inputs(seed: int = 0) -> dict:
    key = jax.random.key(seed)
    ks = jax.random.split(key, 20)
    f32 = jnp.float32
    L = DEPTH

    def nrm(k, shape, scale):
        return jax.random.normal(k, shape, f32) * scale

    x = nrm(ks[0], (BATCH, SEQ, D_MODEL), 1.0)
    positions = jnp.broadcast_to(jnp.arange(SEQ, dtype=jnp.int32), (BATCH, SEQ))
    ln1 = 1.0 + nrm(ks[1], (L, D_MODEL), 0.02)
    w_in = nrm(ks[2], (L, D_MODEL, IN_DIM), D_MODEL ** -0.5)
    gdn_conv = nrm(ks[3], (L, GDN_CONV, 3 * GDN_WIDTH), GDN_CONV ** -0.5)
    gdn_a_log = jnp.log(jax.random.uniform(ks[4], (L, GDN_HEADS), f32, 1.0, 16.0))
    dt = jnp.exp(jax.random.uniform(ks[5], (L, GDN_HEADS), f32, math.log(1e-3), math.log(1e-1)))
    gdn_dt_bias = dt + jnp.log(-jnp.expm1(-dt))
    gdn_norm = 1.0 + nrm(ks[6], (L, GDN_HEAD_DIM), 0.02)
    w_branch_a = nrm(ks[7], (L, GDN_WIDTH, D_MODEL), GDN_WIDTH ** -0.5)
    w_branch_b = nrm(ks[8], (L, MOBA_WIDTH, D_MODEL), MOBA_WIDTH ** -0.5)
    w_out = nrm(ks[9], (L, D_MODEL, D_MODEL), D_MODEL ** -0.5)
    ln2 = 1.0 + nrm(ks[10], (L, D_MODEL), 0.02)
    w_up = nrm(ks[11], (L, D_MODEL, 2 * D_FF), D_MODEL ** -0.5)
    ffn_conv = nrm(ks[12], (L, FFN_CONV, 2 * D_FF), FFN_CONV ** -0.5)
    ffn_conv_bias = nrm(ks[13], (L, 2 * D_FF), 0.01)
    w_down = nrm(ks[14], (L, D_FF, D_MODEL), D_FF ** -0.5)
    final_norm = 1.0 + nrm(ks[15], (D_MODEL,), 0.02)
    return {"x": x, "positions": positions, "ln1": ln1, "w_in": w_in, "gdn_conv": gdn_conv,
            "gdn_a_log": gdn_a_log, "gdn_dt_bias": gdn_dt_bias, "gdn_norm": gdn_norm,
            "w_branch_a": w_branch_a, "w_branch_b": w_branch_b, "w_out": w_out, "ln2": ln2,
            "w_up": w_up, "ffn_conv": ffn_conv, "ffn_conv_bias": ffn_conv_bias,
            "w_down": w_down, "final_norm": final_norm}


def reference(x, positions, ln1, w_in, gdn_conv, gdn_a_log, gdn_dt_bias, gdn_norm,
              w_branch_a, w_branch_b, w_out, ln2, w_up, ffn_conv, ffn_conv_bias, w_down,
              final_norm):
    B, S, _ = x.shape
    for l in range(DEPTH):
        h = rms_norm(x, ln1[l])
        proj = h @ w_in[l]
        qkv_a, beta_a, decay_a, z_a, qkv_b, gate_a, gate_b = split_columns(proj)
        y_a = gated_deltanet(qkv_a, beta_a, decay_a, z_a, gdn_conv[l], gdn_a_log[l],
                             gdn_dt_bias[l], gdn_norm[l]) @ w_branch_a[l]
        q_b, k_b, v_b = jnp.split(qkv_b, 3, axis=-1)
        shp = (B, S, MOBA_HEADS, MOBA_HEAD_DIM)
        q_b = partial_rope(q_b.reshape(shp), positions)
        k_b = partial_rope(k_b.reshape(shp), positions)
        y_b = moba_attention(q_b, k_b, v_b.reshape(shp)).reshape(B, S, MOBA_WIDTH) @ w_branch_b[l]
        merged = jax.nn.sigmoid(gate_a) * y_a + jax.nn.sigmoid(gate_b) * y_b
        x = x + merged @ w_out[l]
        h = rms_norm(x, ln2[l])
        u = causal_dwconv(h @ w_up[l], ffn_conv[l], ffn_conv_bias[l])
        u_gate, u_val = jnp.split(u, 2, axis=-1)
        x = x + (jax.nn.silu(u_gate) * u_val) @ w_down[l]
    return rms_norm(x, final_norm)
```

```python
import functools
import math

import jax
import jax.numpy as jnp
import numpy as np
from jax import lax
from jax.experimental import pallas as pl
from jax.experimental.pallas import tpu as pltpu

F32 = jnp.float32
BF16 = jnp.bfloat16

D_MODEL = 2048
GDN_HEADS = 8
HEAD_DIM = 128
GDN_WIDTH = GDN_HEADS * HEAD_DIM
GDN_CONV = 4
GDN_CHUNK = 64
MOBA_HEADS = 8
MOBA_WIDTH = MOBA_HEADS * HEAD_DIM
MOBA_BLOCK = 256
MOBA_TOPK = 3
ROPE_THETA = 500000.0
ROPE_DIM = HEAD_DIM // 4
ROPE_HALF = ROPE_DIM // 2
D_FF = 5632
FFN_CONV = 3
NORM_EPS = 1e-6
L2_EPS = 1e-6
NEG_INF = -1e30

V7X_LANES = 128
V7X_SUBLANES = 8
V7X_VMEM_LIMIT_BYTES = 56 * 1024 * 1024

GDN_GROUP = 4 * GDN_CHUNK


def _cparams(*sem):
    return pltpu.CompilerParams(dimension_semantics=sem, vmem_limit_bytes=V7X_VMEM_LIMIT_BYTES)


def _dot(a, b):
    return jnp.dot(a, b, preferred_element_type=F32)


def _dot_nt(a, b):
    return lax.dot_general(a, b, (((1,), (1,)), ((), ())), preferred_element_type=F32)


def _dot_hi(a, b):
    return jnp.dot(a, b, preferred_element_type=F32, precision=lax.Precision.HIGHEST)


def _sigmoid(x):
    return 1.0 / (1.0 + jnp.exp(-x))


def _silu(x):
    return x * _sigmoid(x)


def _shift_rows(cur, prev8, s):
    if s == 0:
        return cur
    rolled = pltpu.roll(cur, s, axis=0)
    rolled_prev = pltpu.roll(prev8, s, axis=0)
    row = lax.broadcasted_iota(jnp.int32, prev8.shape, 0)
    first = jnp.where(row < s, rolled_prev, rolled[0:V7X_SUBLANES])
    return jnp.concatenate([first, rolled[V7X_SUBLANES:]], axis=0)


def _causal_conv(cur, prev8, cw, width):
    y = cw[width - 1:width, :] * cur
    for j in range(width - 1):
        y = y + cw[j:j + 1, :] * _shift_rows(cur, prev8, width - 1 - j)
    return y


def _norm_kernel(x_ref, w_ref, h_ref, ht_ref):
    x = x_ref[...]
    y = x * lax.rsqrt(jnp.mean(x * x, axis=-1, keepdims=True) + NORM_EPS) * w_ref[...]
    h_ref[...] = y.astype(h_ref.dtype)
    ht_ref[...] = y.T.astype(ht_ref.dtype)


def _norm_with_transpose(x2d, w, batch, seq, tm=256):
    m, d = x2d.shape
    per_seq = seq // tm
    return pl.pallas_call(
        _norm_kernel,
        grid=(m // tm,),
        in_specs=[pl.BlockSpec((tm, d), lambda i: (i, 0)),
                  pl.BlockSpec((1, d), lambda i: (0, 0))],
        out_specs=[pl.BlockSpec((tm, d), lambda i: (i, 0)),
                   pl.BlockSpec((None, d, tm), lambda i: (i // per_seq, 0, i % per_seq))],
        out_shape=[jax.ShapeDtypeStruct((m, d), BF16),
                   jax.ShapeDtypeStruct((batch, d, seq), BF16)],
        compiler_params=_cparams("arbitrary"),
        name="norm1",
    )(x2d, w.reshape(1, d))


def _final_norm_kernel(x_ref, w_ref, o_ref):
    x = x_ref[...]
    o_ref[...] = x * lax.rsqrt(jnp.mean(x * x, axis=-1, keepdims=True) + NORM_EPS) * w_ref[...]


def _final_norm(x2d, w, tm=512):
    m, d = x2d.shape
    return pl.pallas_call(
        _final_norm_kernel,
        grid=(m // tm,),
        in_specs=[pl.BlockSpec((tm, d), lambda i: (i, 0)),
                  pl.BlockSpec((1, d), lambda i: (0, 0))],
        out_specs=pl.BlockSpec((tm, d), lambda i: (i, 0)),
        out_shape=jax.ShapeDtypeStruct((m, d), F32),
        compiler_params=_cparams("arbitrary"),
        name="final_norm",
    )(x2d, w.reshape(1, d))


def _proj_kernel(h_ref, w_ref, o_ref, *, act):
    acc = _dot(h_ref[...], w_ref[...])
    if act == "sigmoid":
        acc = _sigmoid(acc)
    o_ref[...] = acc.astype(o_ref.dtype)


def _proj(h, w, *, act, out_dtype, tm, tn, name):
    m, k = h.shape
    n = w.shape[1]
    return pl.pallas_call(
        functools.partial(_proj_kernel, act=act),
        grid=(n // tn, m // tm),
        in_specs=[pl.BlockSpec((tm, k), lambda j, i: (i, 0)),
                  pl.BlockSpec((k, tn), lambda j, i: (0, j))],
        out_specs=pl.BlockSpec((tm, tn), lambda j, i: (i, j)),
        out_shape=jax.ShapeDtypeStruct((m, n), out_dtype),
        compiler_params=_cparams("arbitrary", "arbitrary"),
        name=name,
    )(h, w)


def _gdn_proj_kernel(h_ref, w_ref, cw_ref, o_ref, carry_ref, *, tiles_per_seq, tiles_per_part):
    n = pl.program_id(0)
    m = pl.program_id(1)
    acc = _dot(h_ref[...], w_ref[...])
    tm, tn = acc.shape

    @pl.when(m % tiles_per_seq == 0)
    def _():
        carry_ref[...] = jnp.zeros_like(carry_ref)

    y = _silu(_causal_conv(acc, carry_ref[...], cw_ref[...], GDN_CONV))
    carry_ref[...] = acc[tm - V7X_SUBLANES:tm, :]

    part = n // tiles_per_part
    q_scale = jnp.where(part == 0, HEAD_DIM ** -0.5, 1.0).astype(F32)
    heads = []
    for hd in range(tn // HEAD_DIM):
        blk = y[:, hd * HEAD_DIM:(hd + 1) * HEAD_DIM]
        nrm = blk * lax.rsqrt(jnp.sum(blk * blk, axis=-1, keepdims=True) + L2_EPS)
        heads.append(jnp.where(part < 2, nrm * q_scale, blk))
    o_ref[...] = jnp.concatenate(heads, axis=1)


def _gdn_proj(h, w, conv_w, seq, tm=512, tn=512):
    m, k = h.shape
    n = w.shape[1]
    return pl.pallas_call(
        functools.partial(_gdn_proj_kernel, tiles_per_seq=seq // tm, tiles_per_part=GDN_WIDTH // tn),
        grid=(n // tn, m // tm),
        in_specs=[pl.BlockSpec((tm, k), lambda j, i: (i, 0)),
                  pl.BlockSpec((k, tn), lambda j, i: (0, j)),
                  pl.BlockSpec((GDN_CONV, tn), lambda j, i: (0, j))],
        out_specs=pl.BlockSpec((tm, tn), lambda j, i: (i, j)),
        out_shape=jax.ShapeDtypeStruct((m, n), F32),
        scratch_shapes=[pltpu.VMEM((V7X_SUBLANES, tn), F32)],
        compiler_params=_cparams("arbitrary", "arbitrary"),
        name="gdn_qkv_proj",
    )(h, w, conv_w)


def _moba_proj_kernel(wt_ref, ht_ref, pos_ref, freq_ref, o_ref):
    part = pl.program_id(0)
    acc = _dot(wt_ref[...], ht_ref[...])
    ang = freq_ref[...] * pos_ref[...].astype(F32)
    roped = part < 2
    cos = jnp.where(roped, jnp.cos(ang), 1.0)
    sin = jnp.where(roped, jnp.sin(ang), 0.0)
    scale = jnp.where(part == 0, HEAD_DIM ** -0.5, 1.0).astype(F32)
    rows = []
    for hd in range(acc.shape[0] // HEAD_DIM):
        base = hd * HEAD_DIM
        x1 = acc[base:base + ROPE_HALF]
        x2 = acc[base + ROPE_HALF:base + ROPE_DIM]
        rows.append(x1 * cos - x2 * sin)
        rows.append(x2 * cos + x1 * sin)
        rows.append(acc[base + ROPE_DIM:base + HEAD_DIM])
    o_ref[...] = (jnp.concatenate(rows, axis=0) * scale).astype(o_ref.dtype)


def _moba_proj(wt, ht, positions, inv_freq, tt=512):
    batch, k, seq = ht.shape
    rows = wt.shape[0]
    tr = MOBA_WIDTH
    return pl.pallas_call(
        _moba_proj_kernel,
        grid=(rows // tr, batch, seq // tt),
        in_specs=[pl.BlockSpec((tr, k), lambda r, b, t: (r, 0)),
                  pl.BlockSpec((None, k, tt), lambda r, b, t: (b, 0, t)),
                  pl.BlockSpec((None, 1, tt), lambda r, b, t: (b, 0, t)),
                  pl.BlockSpec((ROPE_HALF, 1), lambda r, b, t: (0, 0))],
        out_specs=pl.BlockSpec((None, tr, tt), lambda r, b, t: (b, r, t)),
        out_shape=jax.ShapeDtypeStruct((batch, rows, seq), BF16),
        compiler_params=_cparams("arbitrary", "arbitrary", "arbitrary"),
        name="moba_qkv_proj",
    )(wt, ht, positions.reshape(batch, 1, seq), inv_freq)


def _gdn_kernel(q_ref, k_ref, v_ref, z_ref, sm_ref, a_ref, dt_ref, nw_ref, o_ref,
                state_ref, lbd_ref, subd_ref, ubd_ref, *, groups):
    head = pl.program_id(1)
    t = pl.program_id(2)
    g = GDN_GROUP
    c = GDN_CHUNK

    @pl.when(t == 0)
    def _():
        state_ref[...] = jnp.zeros_like(state_ref)

    ri = lax.broadcasted_iota(jnp.int32, (g, g), 0)
    ci = lax.broadcasted_iota(jnp.int32, (g, g), 1)
    shift = int(math.log2(c))
    same = jnp.right_shift(ri, shift) == jnp.right_shift(ci, shift)
    lbd_ref[...] = jnp.where(same & (ci <= ri), 1.0, 0.0)
    subd_ref[...] = jnp.where(same & (ri > ci), 1.0, 0.0)
    ubd_ref[...] = jnp.where(same & (ci > ri), 1.0, 0.0)
    lane = lax.broadcasted_iota(jnp.int32, (g, V7X_LANES), 1)
    eye = jnp.where(ri == ci, 1.0, 0.0)

    def group_body(gi, carry):
        r0 = pl.multiple_of(gi * g, g)
        q = q_ref[pl.ds(r0, g), :]
        k = k_ref[pl.ds(r0, g), :]
        v = v_ref[pl.ds(r0, g), :]
        z = z_ref[pl.ds(r0, g), :]
        sm = sm_ref[pl.ds(r0, g), :]
        lbd = lbd_ref[...]
        strict = subd_ref[...]

        beta_all = _sigmoid(sm)
        xs = sm + dt_ref[...]
        softplus = jnp.maximum(xs, 0.0) + jnp.log1p(jnp.exp(-jnp.abs(xs)))
        g_all = -jnp.exp(a_ref[...]) * softplus
        beta_col = jnp.sum(jnp.where(lane == head, beta_all, 0.0), axis=-1, keepdims=True)
        g_col = jnp.sum(jnp.where(lane == GDN_HEADS + head, g_all, 0.0), axis=-1, keepdims=True)
        beta_b = jnp.broadcast_to(beta_col, (g, HEAD_DIM))
        g_b = jnp.broadcast_to(g_col, (g, HEAD_DIM))
        g_b2 = jnp.broadcast_to(g_col, (g, g))

        gc_b = _dot_hi(lbd, g_b)
        dmat = _dot_hi(lbd, g_b2 * strict)
        rest = _dot_hi(ubd_ref[...], g_b)
        eg = jnp.exp(gc_b)
        gamma = jnp.exp(dmat)

        kb = k * beta_b
        k16 = k.astype(BF16)
        a_mat = jnp.where(strict > 0.0, _dot_nt(kb.astype(BF16), k16) * gamma, 0.0)
        p = -a_mat
        t_mat = eye + p
        for _ in range(5):
            p = _dot_hi(p, p)
            t_mat = t_mat + _dot_hi(t_mat, p)
        rhs = jnp.concatenate([v * beta_b, kb * eg], axis=1)
        uw = _dot(t_mat.astype(BF16), rhs.astype(BF16))
        u = uw[:, :HEAD_DIM]
        w = uw[:, HEAD_DIM:]
        qk = jnp.where(lbd > 0.0, _dot_nt(q.astype(BF16), k16) * gamma, 0.0)
        qg = (q * eg).astype(BF16)
        kd_t = (k * jnp.exp(rest)).T.astype(BF16)
        w16 = w.astype(BF16)
        qk16 = qk.astype(BF16)

        state = state_ref[...]
        outs = []
        zeros_c = jnp.zeros((c, HEAD_DIM), F32)
        for ch in range(g // c):
            rows = slice(ch * c, (ch + 1) * c)
            s16 = state.astype(BF16)
            v_new = u[rows] - _dot(w16[rows], s16)
            vn_all = jnp.concatenate([zeros_c] * ch + [v_new] + [zeros_c] * (g // c - 1 - ch),
                                     axis=0).astype(BF16)
            outs.append(_dot(qg[rows], s16) + _dot(qk16[rows], vn_all))
            g_last = eg[(ch + 1) * c - 1:(ch + 1) * c, :]
            state = state * g_last + _dot(kd_t, vn_all)
        state_ref[...] = state
        o = jnp.concatenate(outs, axis=0)
        o = o * lax.rsqrt(jnp.mean(o * o, axis=-1, keepdims=True) + NORM_EPS) * nw_ref[...]
        o_ref[pl.ds(r0, g), :] = (o * _silu(z)).astype(o_ref.dtype)
        return carry

    lax.fori_loop(0, groups, group_body, 0)


def _gdn(qkv, z, small, a_pad, dt_pad, norm_w, batch, seq, ts=1024):
    m = qkv.shape[0]
    tiles = seq // ts
    hb = GDN_HEADS
    row = lambda b, h, t: b * tiles + t
    return pl.pallas_call(
        functools.partial(_gdn_kernel, groups=ts // GDN_GROUP),
        grid=(batch, GDN_HEADS, tiles),
        in_specs=[pl.BlockSpec((ts, HEAD_DIM), lambda b, h, t: (row(b, h, t), h)),
                  pl.BlockSpec((ts, HEAD_DIM), lambda b, h, t: (row(b, h, t), hb + h)),
                  pl.BlockSpec((ts, HEAD_DIM), lambda b, h, t: (row(b, h, t), 2 * hb + h)),
                  pl.BlockSpec((ts, HEAD_DIM), lambda b, h, t: (row(b, h, t), h)),
                  pl.BlockSpec((ts, V7X_LANES), lambda b, h, t: (row(b, h, t), 0)),
                  pl.BlockSpec((1, V7X_LANES), lambda b, h, t: (0, 0)),
                  pl.BlockSpec((1, V7X_LANES), lambda b, h, t: (0, 0)),
                  pl.BlockSpec((1, HEAD_DIM), lambda b, h, t: (0, 0))],
        out_specs=pl.BlockSpec((ts, HEAD_DIM), lambda b, h, t: (row(b, h, t), h)),
        out_shape=jax.ShapeDtypeStruct((m, GDN_WIDTH), BF16),
        scratch_shapes=[pltpu.VMEM((HEAD_DIM, HEAD_DIM), F32),
                        pltpu.VMEM((GDN_GROUP, GDN_GROUP), F32),
                        pltpu.VMEM((GDN_GROUP, GDN_GROUP), F32),
                        pltpu.VMEM((GDN_GROUP, GDN_GROUP), F32)],
        compiler_params=_cparams("arbitrary", "arbitrary", "arbitrary"),
        name="gated_delta_rule",
    )(qkv, qkv, qkv, z, small, a_pad, dt_pad, norm_w)


def _moba_kernel(q_ref, k_ref, v_ref, o_ref, kn_ref, km_ref, s_ref, p_ref, bias_ref, m_ref, l_ref, *, nblk):
    blk = MOBA_BLOCK
    for j in range(nblk):
        kt = k_ref[:, j * blk:(j + 1) * blk].astype(F32)
        kn = kt.T
        kn_ref[j * blk:(j + 1) * blk, :] = kn.astype(BF16)
        km_ref[j:j + 1, :] = jnp.sum(kn, axis=0, keepdims=True) * (1.0 / blk)
    p_ref[...] = jnp.zeros_like(p_ref)

    n_iota = lax.broadcasted_iota(jnp.int32, (nblk, blk), 0)
    key_i = lax.broadcasted_iota(jnp.int32, (blk, blk), 0)
    qry_i = lax.broadcasted_iota(jnp.int32, (blk, blk), 1)
    causal_bias = jnp.where(key_i <= qry_i, 0.0, NEG_INF)

    def tile_body(i, carry):
        c0 = pl.multiple_of(i * blk, blk)
        qt = q_ref[:, pl.ds(c0, blk)]
        gate = _dot_hi(km_ref[...], qt.astype(F32))
        valid = n_iota < i
        rows = []
        for n in range(nblk):
            g_n = gate[n:n + 1, :]
            beats = valid & ((gate > g_n) | ((gate == g_n) & (n_iota < n)))
            cnt = jnp.sum(jnp.where(beats, 1.0, 0.0), axis=0, keepdims=True)
            rows.append(jnp.where(cnt < float(MOBA_TOPK), 0.0, NEG_INF))
        bias_ref[...] = jnp.where(valid, jnp.concatenate(rows, axis=0), NEG_INF)

        m_ref[...] = jnp.full(m_ref.shape, -jnp.inf, F32)
        for j in range(nblk):
            @pl.when(j <= i)
            def _():
                s = _dot(kn_ref[j * blk:(j + 1) * blk, :], qt)
                past = jnp.broadcast_to(bias_ref[j:j + 1, :], (blk, blk))
                s = s + jnp.where(j == i, causal_bias, past)
                s_ref[j * blk:(j + 1) * blk, :] = s
                part = jnp.max(s.reshape(blk // V7X_SUBLANES, V7X_SUBLANES, blk), axis=0)
                m_ref[...] = jnp.maximum(m_ref[...], part)
        m_row = jnp.max(m_ref[...], axis=0, keepdims=True)

        l_ref[...] = jnp.zeros(l_ref.shape, F32)
        for j in range(nblk):
            @pl.when(j <= i)
            def _():
                p = jnp.exp(s_ref[j * blk:(j + 1) * blk, :] - m_row)
                l_ref[...] += jnp.sum(p.reshape(blk // V7X_SUBLANES, V7X_SUBLANES, blk), axis=0)
                p_ref[j * blk:(j + 1) * blk, :] = p.astype(BF16)
        l_row = jnp.sum(l_ref[...], axis=0, keepdims=True)
        o_t = _dot(v_ref[...], p_ref[...]) / l_row
        o_ref[pl.ds(c0, blk), :] = o_t.T.astype(o_ref.dtype)
        return carry

    lax.fori_loop(0, nblk, tile_body, 0)


def _moba(qkv_t, batch, seq):
    nblk = seq // MOBA_BLOCK
    hb = MOBA_HEADS
    return pl.pallas_call(
        functools.partial(_moba_kernel, nblk=nblk),
        grid=(batch, MOBA_HEADS),
        in_specs=[pl.BlockSpec((None, HEAD_DIM, seq), lambda b, h: (b, h, 0)),
                  pl.BlockSpec((None, HEAD_DIM, seq), lambda b, h: (b, hb + h, 0)),
                  pl.BlockSpec((None, HEAD_DIM, seq), lambda b, h: (b, 2 * hb + h, 0))],
        out_specs=pl.BlockSpec((seq, HEAD_DIM), lambda b, h: (b, h)),
        out_shape=jax.ShapeDtypeStruct((batch * seq, MOBA_WIDTH), BF16),
        scratch_shapes=[pltpu.VMEM((seq, HEAD_DIM), BF16),
                        pltpu.VMEM((nblk, HEAD_DIM), F32),
                        pltpu.VMEM((seq, MOBA_BLOCK), F32),
                        pltpu.VMEM((seq, MOBA_BLOCK), BF16),
                        pltpu.VMEM((nblk, MOBA_BLOCK), F32),
                        pltpu.VMEM((V7X_SUBLANES, MOBA_BLOCK), F32),
                        pltpu.VMEM((V7X_SUBLANES, MOBA_BLOCK), F32)],
        compiler_params=_cparams("arbitrary", "arbitrary"),
        name="moba_attention",
    )(qkv_t, qkv_t, qkv_t)


def _merge_kernel(a_ref, b_ref, wa_ref, wb_ref, ga_ref, gb_ref, o_ref):
    ya = _dot(a_ref[...], wa_ref[...])
    yb = _dot(b_ref[...], wb_ref[...])
    o_ref[...] = (ga_ref[...] * ya + gb_ref[...] * yb).astype(o_ref.dtype)


def _merge(ya_in, yb_in, wa, wb, gates, tm=512, tn=1024):
    m, k = ya_in.shape
    n = wa.shape[1]
    nb = n // tn
    return pl.pallas_call(
        _merge_kernel,
        grid=(nb, m // tm),
        in_specs=[pl.BlockSpec((tm, k), lambda j, i: (i, 0)),
                  pl.BlockSpec((tm, k), lambda j, i: (i, 0)),
                  pl.BlockSpec((k, tn), lambda j, i: (0, j)),
                  pl.BlockSpec((k, tn), lambda j, i: (0, j)),
                  pl.BlockSpec((tm, tn), lambda j, i: (i, j)),
                  pl.BlockSpec((tm, tn), lambda j, i: (i, nb + j))],
        out_specs=pl.BlockSpec((tm, tn), lambda j, i: (i, j)),
        out_shape=jax.ShapeDtypeStruct((m, n), BF16),
        compiler_params=_cparams("arbitrary", "arbitrary"),
        name="branch_merge",
    )(ya_in, yb_in, wa, wb, gates, gates)


def _outproj_kernel(a_ref, w_ref, x_ref, nw_ref, x1_ref, h_ref):
    x1 = x_ref[...] + _dot(a_ref[...], w_ref[...])
    x1_ref[...] = x1
    h = x1 * lax.rsqrt(jnp.mean(x1 * x1, axis=-1, keepdims=True) + NORM_EPS) * nw_ref[...]
    h_ref[...] = h.astype(h_ref.dtype)


def _outproj(a, w, x2d, nw, tm=256):
    m, k = a.shape
    n = w.shape[1]
    return pl.pallas_call(
        _outproj_kernel,
        grid=(m // tm,),
        in_specs=[pl.BlockSpec((tm, k), lambda i: (i, 0)),
                  pl.BlockSpec((k, n), lambda i: (0, 0)),
                  pl.BlockSpec((tm, n), lambda i: (i, 0)),
                  pl.BlockSpec((1, n), lambda i: (0, 0))],
        out_specs=[pl.BlockSpec((tm, n), lambda i: (i, 0)),
                   pl.BlockSpec((tm, n), lambda i: (i, 0))],
        out_shape=[jax.ShapeDtypeStruct((m, n), F32),
                   jax.ShapeDtypeStruct((m, n), BF16)],
        compiler_params=_cparams("arbitrary"),
        name="out_proj_norm2",
    )(a, w, x2d, nw.reshape(1, n))


def _upproj_kernel(h_ref, wg_ref, wv_ref, cg_ref, cv_ref, bg_ref, bv_ref, o_ref,
                   carry_g, carry_v, *, tiles_per_seq):
    m = pl.program_id(1)
    h = h_ref[...]
    ug = _dot(h, wg_ref[...])
    uv = _dot(h, wv_ref[...])
    tm = ug.shape[0]

    @pl.when(m % tiles_per_seq == 0)
    def _():
        carry_g[...] = jnp.zeros_like(carry_g)
        carry_v[...] = jnp.zeros_like(carry_v)

    yg = _causal_conv(ug, carry_g[...], cg_ref[...], FFN_CONV) + bg_ref[...]
    yv = _causal_conv(uv, carry_v[...], cv_ref[...], FFN_CONV) + bv_ref[...]
    carry_g[...] = ug[tm - V7X_SUBLANES:tm, :]
    carry_v[...] = uv[tm - V7X_SUBLANES:tm, :]
    o_ref[...] = (_silu(yg) * yv).astype(o_ref.dtype)


def _upproj(h, w_up, conv_w, conv_b, seq, tm=512, tn=512):
    m, k = h.shape
    nb = D_FF // tn
    return pl.pallas_call(
        functools.partial(_upproj_kernel, tiles_per_seq=seq // tm),
        grid=(nb, m // tm),
        in_specs=[pl.BlockSpec((tm, k), lambda j, i: (i, 0)),
                  pl.BlockSpec((k, tn), lambda j, i: (0, j)),
                  pl.BlockSpec((k, tn), lambda j, i: (0, nb + j)),
                  pl.BlockSpec((FFN_CONV, tn), lambda j, i: (0, j)),
                  pl.BlockSpec((FFN_CONV, tn), lambda j, i: (0, nb + j)),
                  pl.BlockSpec((1, tn), lambda j, i: (0, j)),
                  pl.BlockSpec((1, tn), lambda j, i: (0, nb + j))],
        out_specs=pl.BlockSpec((tm, tn), lambda j, i: (i, j)),
        out_shape=jax.ShapeDtypeStruct((m, D_FF), BF16),
        scratch_shapes=[pltpu.VMEM((V7X_SUBLANES, tn), F32),
                        pltpu.VMEM((V7X_SUBLANES, tn), F32)],
        compiler_params=_cparams("arbitrary", "arbitrary"),
        name="up_proj_conv_gate",
    )(h, w_up, w_up, conv_w, conv_w, conv_b, conv_b)


def _downproj_kernel(a_ref, w_ref, x_ref, o_ref):
    o_ref[...] = x_ref[...] + _dot(a_ref[...], w_ref[...])


def _downproj(a, w, x1, tm=512, tn=1024):
    m, k = a.shape
    n = w.shape[1]
    return pl.pallas_call(
        _downproj_kernel,
        grid=(n // tn, m // tm),
        in_specs=[pl.BlockSpec((tm, k), lambda j, i: (i, 0)),
                  pl.BlockSpec((k, tn), lambda j, i: (0, j)),
                  pl.BlockSpec((tm, tn), lambda j, i: (i, j))],
        out_specs=pl.BlockSpec((tm, tn), lambda j, i: (i, j)),
        out_shape=jax.ShapeDtypeStruct((m, n), F32),
        compiler_params=_cparams("arbitrary", "arbitrary"),
        name="down_proj",
    )(a, w, x1)


def _pad_lanes(vec, offset):
    return jnp.zeros((1, V7X_LANES), F32).at[0, offset:offset + vec.shape[0]].set(vec.astype(F32))


def kernel(x, positions, ln1, w_in, gdn_conv, gdn_a_log, gdn_dt_bias, gdn_norm, w_branch_a, w_branch_b,
           w_out, ln2, w_up, ffn_conv, ffn_conv_bias, w_down, final_norm):
    batch, seq, d = x.shape
    m = batch * seq
    depth = ln1.shape[0]
    o_qkv_a = 3 * GDN_WIDTH
    o_small = o_qkv_a + 2 * GDN_HEADS
    o_z = o_small + GDN_WIDTH
    o_qkv_b = o_z + 3 * MOBA_WIDTH
    inv_freq = (ROPE_THETA ** (-jnp.arange(ROPE_HALF, dtype=F32) / ROPE_HALF)).reshape(ROPE_HALF, 1)

    x2d = x.reshape(m, d)
    for l in range(depth):
        w = w_in[l]
        w_qkv_a = w[:, :o_qkv_a].astype(BF16)
        w_small = jnp.pad(w[:, o_qkv_a:o_small], ((0, 0), (0, V7X_LANES - 2 * GDN_HEADS))).astype(BF16)
        w_z = w[:, o_small:o_z].astype(BF16)
        w_qkv_b_t = w[:, o_z:o_qkv_b].T.astype(BF16)
        w_gates = w[:, o_qkv_b:].astype(BF16)

        h1, h1_t = _norm_with_transpose(x2d, ln1[l], batch, seq)
        qkv_a = _gdn_proj(h1, w_qkv_a, gdn_conv[l], seq)
        z_a = _proj(h1, w_z, act=None, out_dtype=F32, tm=1024, tn=1024, name="z_proj")
        small = _proj(h1, w_small, act=None, out_dtype=F32, tm=1024, tn=V7X_LANES, name="beta_decay_proj")
        gates = _proj(h1, w_gates, act="sigmoid", out_dtype=F32, tm=1024, tn=1024, name="gate_proj")
        qkv_b_t = _moba_proj(w_qkv_b_t, h1_t, positions, inv_freq)

        gdn_out = _gdn(qkv_a, z_a, small, _pad_lanes(gdn_a_log[l], GDN_HEADS), _pad_lanes(gdn_dt_bias[l], GDN_HEADS),
                       gdn_norm[l].reshape(1, HEAD_DIM), batch, seq)
        attn = _moba(qkv_b_t, batch, seq)
        merged = _merge(gdn_out, attn, w_branch_a[l].astype(BF16), w_branch_b[l].astype(BF16), gates)
        x1, h2 = _outproj(merged, w_out[l].astype(BF16), x2d, ln2[l])
        act = _upproj(h2, w_up[l].astype(BF16), ffn_conv[l], ffn_conv_bias[l].reshape(1, 2 * D_FF), seq)
        x2d = _downproj(act, w_down[l].astype(BF16), x1)
    return _final_norm(x2d, final_norm).reshape(batch, seq, d)
```

```python
import functools
import math

import jax
import jax.numpy as jnp
import numpy as np
from jax import lax
from jax.experimental import pallas as pl
from jax.experimental.pallas import tpu as pltpu

F32 = jnp.float32
BF16 = jnp.bfloat16

D_MODEL = 2048
GDN_HEADS = 8
HEAD_DIM = 128
GDN_WIDTH = GDN_HEADS * HEAD_DIM
GDN_CONV = 4
GDN_CHUNK = 64
MOBA_HEADS = 8
MOBA_WIDTH = MOBA_HEADS * HEAD_DIM
MOBA_BLOCK = 256
MOBA_TOPK = 3
ROPE_THETA = 500000.0
ROPE_DIM = HEAD_DIM // 4
ROPE_HALF = ROPE_DIM // 2
D_FF = 5632
FFN_CONV = 3
NORM_EPS = 1e-6
L2_EPS = 1e-6
NEG_INF = -1e30

V7X_LANES = 128
V7X_SUBLANES = 8
V7X_VMEM_LIMIT_BYTES = 56 * 1024 * 1024

GDN_GROUP = 4 * GDN_CHUNK


def _cparams(*sem):
    return pltpu.CompilerParams(dimension_semantics=sem, vmem_limit_bytes=V7X_VMEM_LIMIT_BYTES)


def _dot(a, b):
    return jnp.dot(a, b, preferred_element_type=F32)


def _dot_nt(a, b):
    return lax.dot_general(a, b, (((1,), (1,)), ((), ())), preferred_element_type=F32)


def _dot_hi(a, b):
    return jnp.dot(a, b, preferred_element_type=F32, precision=lax.Precision.HIGHEST)


def _split3(x):
    hi = x.astype(BF16)
    r1 = x - hi.astype(F32)
    mid = r1.astype(BF16)
    lo = (r1 - mid.astype(F32)).astype(BF16)
    return hi, mid, lo


def _sigmoid(x):
    return 1.0 / (1.0 + jnp.exp(-x))


def _silu(x):
    return x * _sigmoid(x)


def _shift_rows(cur, prev8, s):
    if s == 0:
        return cur
    rolled = pltpu.roll(cur, s, axis=0)
    rolled_prev = pltpu.roll(prev8, s, axis=0)
    row = lax.broadcasted_iota(jnp.int32, prev8.shape, 0)
    first = jnp.where(row < s, rolled_prev, rolled[0:V7X_SUBLANES])
    return jnp.concatenate([first, rolled[V7X_SUBLANES:]], axis=0)


def _causal_conv(cur, prev8, cw, width):
    y = cw[width - 1:width, :] * cur
    for j in range(width - 1):
        y = y + cw[j:j + 1, :] * _shift_rows(cur, prev8, width - 1 - j)
    return y


def _norm_kernel(x_ref, w_ref, h_ref, ht_ref):
    x = x_ref[...]
    y = x * lax.rsqrt(jnp.mean(x * x, axis=-1, keepdims=True) + NORM_EPS) * w_ref[...]
    h_ref[...] = y.astype(h_ref.dtype)
    ht_ref[...] = y.T.astype(ht_ref.dtype)


def _norm_with_transpose(x2d, w, batch, seq, tm=256):
    m, d = x2d.shape
    per_seq = seq // tm
    return pl.pallas_call(
        _norm_kernel,
        grid=(m // tm,),
        in_specs=[pl.BlockSpec((tm, d), lambda i: (i, 0)),
                  pl.BlockSpec((1, d), lambda i: (0, 0))],
        out_specs=[pl.BlockSpec((tm, d), lambda i: (i, 0)),
                   pl.BlockSpec((None, d, tm), lambda i: (i // per_seq, 0, i % per_seq))],
        out_shape=[jax.ShapeDtypeStruct((m, d), BF16),
                   jax.ShapeDtypeStruct((batch, d, seq), BF16)],
        compiler_params=_cparams("arbitrary"),
        name="norm1",
    )(x2d, w.reshape(1, d))


def _final_norm_kernel(x_ref, w_ref, o_ref):
    x = x_ref[...]
    o_ref[...] = x * lax.rsqrt(jnp.mean(x * x, axis=-1, keepdims=True) + NORM_EPS) * w_ref[...]


def _final_norm(x2d, w, tm=512):
    m, d = x2d.shape
    return pl.pallas_call(
        _final_norm_kernel,
        grid=(m // tm,),
        in_specs=[pl.BlockSpec((tm, d), lambda i: (i, 0)),
                  pl.BlockSpec((1, d), lambda i: (0, 0))],
        out_specs=pl.BlockSpec((tm, d), lambda i: (i, 0)),
        out_shape=jax.ShapeDtypeStruct((m, d), F32),
        compiler_params=_cparams("arbitrary"),
        name="final_norm",
    )(x2d, w.reshape(1, d))


def _proj_kernel(h_ref, w_ref, o_ref, *, act):
    acc = _dot(h_ref[...], w_ref[...])
    if act == "sigmoid":
        acc = _sigmoid(acc)
    o_ref[...] = acc.astype(o_ref.dtype)


def _proj(h, w, *, act, out_dtype, tm, tn, name):
    m, k = h.shape
    n = w.shape[1]
    return pl.pallas_call(
        functools.partial(_proj_kernel, act=act),
        grid=(n // tn, m // tm),
        in_specs=[pl.BlockSpec((tm, k), lambda j, i: (i, 0)),
                  pl.BlockSpec((k, tn), lambda j, i: (0, j))],
        out_specs=pl.BlockSpec((tm, tn), lambda j, i: (i, j)),
        out_shape=jax.ShapeDtypeStruct((m, n), out_dtype),
        compiler_params=_cparams("arbitrary", "arbitrary"),
        name=name,
    )(h, w)


def _gdn_proj_kernel(h_ref, w_ref, cw_ref, o_ref, carry_ref, *, tiles_per_seq, tiles_per_part, sub):
    n = pl.program_id(0)
    m = pl.program_id(1)
    tn = w_ref.shape[1]

    @pl.when(m % tiles_per_seq == 0)
    def _():
        carry_ref[...] = jnp.zeros_like(carry_ref)

    part = n // tiles_per_part
    q_scale = jnp.where(part == 0, HEAD_DIM ** -0.5, 1.0).astype(F32)
    prev = carry_ref[...]
    for r in range(h_ref.shape[0] // sub):
        acc = _dot(h_ref[r * sub:(r + 1) * sub, :], w_ref[...])
        y = _silu(_causal_conv(acc, prev, cw_ref[...], GDN_CONV))
        prev = acc[sub - V7X_SUBLANES:sub, :]
        heads = []
        for hd in range(tn // HEAD_DIM):
            blk = y[:, hd * HEAD_DIM:(hd + 1) * HEAD_DIM]
            nrm = blk * lax.rsqrt(jnp.sum(blk * blk, axis=-1, keepdims=True) + L2_EPS)
            heads.append(jnp.where(part < 2, nrm * q_scale, blk))
        o_ref[r * sub:(r + 1) * sub, :] = jnp.concatenate(heads, axis=1)
    carry_ref[...] = prev


def _gdn_proj(h, w, conv_w, seq, tm=1024, tn=512, sub=256):
    m, k = h.shape
    n = w.shape[1]
    return pl.pallas_call(
        functools.partial(_gdn_proj_kernel, tiles_per_seq=seq // tm, tiles_per_part=GDN_WIDTH // tn, sub=sub),
        grid=(n // tn, m // tm),
        in_specs=[pl.BlockSpec((tm, k), lambda j, i: (i, 0)),
                  pl.BlockSpec((k, tn), lambda j, i: (0, j)),
                  pl.BlockSpec((GDN_CONV, tn), lambda j, i: (0, j))],
        out_specs=pl.BlockSpec((tm, tn), lambda j, i: (i, j)),
        out_shape=jax.ShapeDtypeStruct((m, n), F32),
        scratch_shapes=[pltpu.VMEM((V7X_SUBLANES, tn), F32)],
        compiler_params=_cparams("arbitrary", "arbitrary"),
        name="gdn_qkv_proj",
    )(h, w, conv_w)


def _moba_proj_kernel(wt_ref, ht_ref, pos_ref, freq_ref, o_ref):
    part = pl.program_id(0)
    acc = _dot(wt_ref[...], ht_ref[...])
    ang = freq_ref[...] * pos_ref[...].astype(F32)
    roped = part < 2
    cos = jnp.where(roped, jnp.cos(ang), 1.0)
    sin = jnp.where(roped, jnp.sin(ang), 0.0)
    scale = jnp.where(part == 0, HEAD_DIM ** -0.5, 1.0).astype(F32)
    rows = []
    for hd in range(acc.shape[0] // HEAD_DIM):
        base = hd * HEAD_DIM
        x1 = acc[base:base + ROPE_HALF]
        x2 = acc[base + ROPE_HALF:base + ROPE_DIM]
        rows.append(x1 * cos - x2 * sin)
        rows.append(x2 * cos + x1 * sin)
        rows.append(acc[base + ROPE_DIM:base + HEAD_DIM])
    o_ref[...] = (jnp.concatenate(rows, axis=0) * scale).astype(o_ref.dtype)


def _moba_proj(wt, ht, positions, inv_freq, tt=512):
    batch, k, seq = ht.shape
    rows = wt.shape[0]
    tr = MOBA_WIDTH
    return pl.pallas_call(
        _moba_proj_kernel,
        grid=(rows // tr, batch, seq // tt),
        in_specs=[pl.BlockSpec((tr, k), lambda r, b, t: (r, 0)),
                  pl.BlockSpec((None, k, tt), lambda r, b, t: (b, 0, t)),
                  pl.BlockSpec((None, 1, tt), lambda r, b, t: (b, 0, t)),
                  pl.BlockSpec((ROPE_HALF, 1), lambda r, b, t: (0, 0))],
        out_specs=pl.BlockSpec((None, tr, tt), lambda r, b, t: (b, r, t)),
        out_shape=jax.ShapeDtypeStruct((batch, rows, seq), BF16),
        compiler_params=_cparams("arbitrary", "arbitrary", "arbitrary"),
        name="moba_qkv_proj",
    )(wt, ht, positions.reshape(batch, 1, seq), inv_freq)


def _gdn_kernel(q_ref, k_ref, v_ref, z_ref, sm_ref, a_ref, dt_ref, nw_ref, o_ref,
                state_ref, lbd_ref, subd_ref, *, groups, heads_per_step):
    t = pl.program_id(2)
    g = GDN_GROUP
    c = GDN_CHUNK

    @pl.when(t == 0)
    def _():
        state_ref[...] = jnp.zeros_like(state_ref)

    ri = lax.broadcasted_iota(jnp.int32, (g, g), 0)
    ci = lax.broadcasted_iota(jnp.int32, (g, g), 1)
    shift = int(math.log2(c))
    same = jnp.right_shift(ri, shift) == jnp.right_shift(ci, shift)
    lbd_ref[...] = jnp.where(same & (ci <= ri), 1.0, 0.0)
    subd_ref[...] = jnp.where(same & (ri > ci), 1.0, 0.0)
    lane = lax.broadcasted_iota(jnp.int32, (g, V7X_LANES), 1)
    eye = jnp.where(ri == ci, 1.0, 0.0)

    def group_body(gi, carry):
        hs = range(heads_per_step)
        nch = g // c
        r0 = pl.multiple_of(gi * g, g)
        sm = sm_ref[pl.ds(r0, g), :]
        beta_all = _sigmoid(sm)
        xs = sm + dt_ref[...]
        softplus = jnp.maximum(xs, 0.0) + jnp.log1p(jnp.exp(-jnp.abs(xs)))
        g_all = -jnp.exp(a_ref[...]) * softplus
        lbd = lbd_ref[...]
        lbd16 = lbd.astype(BF16)
        strict = subd_ref[...]
        cols = [slice(hh * HEAD_DIM, (hh + 1) * HEAD_DIM) for hh in hs]
        heads = [pl.program_id(1) * heads_per_step + hh for hh in hs]

        k = [k_ref[pl.ds(r0, g), cols[hh]] for hh in hs]
        k16 = [k[hh].astype(BF16) for hh in hs]
        beta_b = [jnp.broadcast_to(jnp.sum(jnp.where(lane == heads[hh], beta_all, 0.0), axis=-1, keepdims=True),
                                   (g, HEAD_DIM)) for hh in hs]
        g_b = [jnp.broadcast_to(jnp.sum(jnp.where(lane == GDN_HEADS + heads[hh], g_all, 0.0), axis=-1, keepdims=True),
                                (g, HEAD_DIM)) for hh in hs]
        kb = [k[hh] * beta_b[hh] for hh in hs]

        cs = [_dot(lbd16, jnp.concatenate(_split3(g_b[hh]), axis=1)) for hh in hs]
        kk = [_dot_nt(kb[hh].astype(BF16), k16[hh]) for hh in hs]
        qk_raw = [_dot_nt(q_ref[pl.ds(r0, g), cols[hh]].astype(BF16), k16[hh]) for hh in hs]
        gc_b = [cs[hh][:, :HEAD_DIM] + cs[hh][:, HEAD_DIM:2 * HEAD_DIM] + cs[hh][:, 2 * HEAD_DIM:] for hh in hs]
        gamma = []
        for hh in hs:
            gc_row = gc_b[hh].T[0:1, :]
            dmat = jnp.where(lbd > 0.0, jnp.concatenate([gc_b[hh], gc_b[hh]], axis=1) - gc_row, 0.0)
            gamma.append(jnp.exp(dmat))
        eg = [jnp.exp(gc_b[hh]) for hh in hs]

        p = [-jnp.where(strict > 0.0, kk[hh] * gamma[hh], 0.0) for hh in hs]
        t_mat = [eye + p[hh] for hh in hs]
        for _ in range(5):
            p16 = [p[hh].astype(BF16) for hh in hs]
            p = [_dot(p16[hh], p16[hh]) for hh in hs]
            tp = [_dot(t_mat[hh].astype(BF16), p[hh].astype(BF16)) for hh in hs]
            t_mat = [t_mat[hh] + tp[hh] for hh in hs]
        uw = [_dot(t_mat[hh].astype(BF16),
                   jnp.concatenate([v_ref[pl.ds(r0, g), cols[hh]] * beta_b[hh], kb[hh] * eg[hh]], axis=1).astype(BF16))
              for hh in hs]
        u = [uw[hh][:, :HEAD_DIM] for hh in hs]
        w16 = [uw[hh][:, HEAD_DIM:].astype(BF16) for hh in hs]
        qk16 = [jnp.where(lbd > 0.0, qk_raw[hh] * gamma[hh], 0.0).astype(BF16) for hh in hs]
        qg = [(q_ref[pl.ds(r0, g), cols[hh]] * eg[hh]).astype(BF16) for hh in hs]
        kd_t = []
        for hh in hs:
            gc_last = jnp.concatenate(
                [jnp.broadcast_to(gc_b[hh][(ch + 1) * c - 1:(ch + 1) * c, :], (c, HEAD_DIM)) for ch in range(nch)],
                axis=0)
            kd_t.append((k[hh] * jnp.exp(gc_last - gc_b[hh])).T.astype(BF16))

        state = [state_ref[hh] for hh in hs]
        outs = [[] for _ in hs]
        zeros_c = jnp.zeros((c, HEAD_DIM), F32)
        for ch in range(nch):
            rows = slice(ch * c, (ch + 1) * c)
            s16 = [state[hh].astype(BF16) for hh in hs]
            ws = [_dot(w16[hh][rows], s16[hh]) for hh in hs]
            qs = [_dot(qg[hh][rows], s16[hh]) for hh in hs]
            vn_all = [jnp.concatenate([zeros_c] * ch + [u[hh][rows] - ws[hh]] + [zeros_c] * (nch - 1 - ch),
                                      axis=0).astype(BF16) for hh in hs]
            sv = [_dot(kd_t[hh], vn_all[hh]) for hh in hs]
            ov = [_dot(qk16[hh][rows], vn_all[hh]) for hh in hs]
            for hh in hs:
                outs[hh].append(qs[hh] + ov[hh])
                state[hh] = state[hh] * eg[hh][(ch + 1) * c - 1:(ch + 1) * c, :] + sv[hh]
        for hh in hs:
            state_ref[hh] = state[hh]
            o = jnp.concatenate(outs[hh], axis=0)
            o = o * lax.rsqrt(jnp.mean(o * o, axis=-1, keepdims=True) + NORM_EPS) * nw_ref[...]
            o_ref[pl.ds(r0, g), cols[hh]] = (o * _silu(z_ref[pl.ds(r0, g), cols[hh]])).astype(o_ref.dtype)
        return carry

    lax.fori_loop(0, groups, group_body, 0)


def _gdn(qkv, z, small, a_pad, dt_pad, norm_w, batch, seq, ts=1024, heads_per_step=4):
    m = qkv.shape[0]
    tiles = seq // ts
    hp = heads_per_step
    hb = GDN_HEADS // hp
    wd = hp * HEAD_DIM
    row = lambda b, h, t: b * tiles + t
    return pl.pallas_call(
        functools.partial(_gdn_kernel, groups=ts // GDN_GROUP, heads_per_step=hp),
        grid=(batch, hb, tiles),
        in_specs=[pl.BlockSpec((ts, wd), lambda b, h, t: (row(b, h, t), h)),
                  pl.BlockSpec((ts, wd), lambda b, h, t: (row(b, h, t), hb + h)),
                  pl.BlockSpec((ts, wd), lambda b, h, t: (row(b, h, t), 2 * hb + h)),
                  pl.BlockSpec((ts, wd), lambda b, h, t: (row(b, h, t), h)),
                  pl.BlockSpec((ts, V7X_LANES), lambda b, h, t: (row(b, h, t), 0)),
                  pl.BlockSpec((1, V7X_LANES), lambda b, h, t: (0, 0)),
                  pl.BlockSpec((1, V7X_LANES), lambda b, h, t: (0, 0)),
                  pl.BlockSpec((1, HEAD_DIM), lambda b, h, t: (0, 0))],
        out_specs=pl.BlockSpec((ts, wd), lambda b, h, t: (row(b, h, t), h)),
        out_shape=jax.ShapeDtypeStruct((m, GDN_WIDTH), BF16),
        scratch_shapes=[pltpu.VMEM((hp, HEAD_DIM, HEAD_DIM), F32),
                        pltpu.VMEM((GDN_GROUP, GDN_GROUP), F32),
                        pltpu.VMEM((GDN_GROUP, GDN_GROUP), F32)],
        compiler_params=_cparams("arbitrary", "arbitrary", "arbitrary"),
        name="gated_delta_rule",
    )(qkv, qkv, qkv, z, small, a_pad, dt_pad, norm_w)


def _moba_kernel(q_ref, k_ref, v_ref, o_ref, kn_ref, km_ref, s_ref, p_ref, bias_ref, *, nblk):
    blk = MOBA_BLOCK
    for j in range(nblk):
        kt = k_ref[:, j * blk:(j + 1) * blk].astype(F32)
        kn = kt.T
        kn_ref[j * blk:(j + 1) * blk, :] = kn.astype(BF16)
        km_ref[j:j + 1, :] = jnp.sum(kn, axis=0, keepdims=True) * (1.0 / blk)

    key_i = lax.broadcasted_iota(jnp.int32, (blk, blk), 0)
    qry_i = lax.broadcasted_iota(jnp.int32, (blk, blk), 1)
    causal_bias = jnp.where(key_i <= qry_i, 0.0, NEG_INF)
    groups = blk // V7X_SUBLANES

    for i in range(nblk):
        qt = q_ref[:, i * blk:(i + 1) * blk]
        if i > MOBA_TOPK:
            gate = _dot_hi(km_ref[...], qt.astype(F32))
            n_iota = lax.broadcasted_iota(jnp.int32, (nblk, blk), 0)
            valid = n_iota < i
            for n in range(i):
                g_n = gate[n:n + 1, :]
                beats = valid & ((gate > g_n) | ((gate == g_n) & (n_iota < n)))
                cnt = jnp.sum(jnp.where(beats, 1.0, 0.0), axis=0, keepdims=True)
                bias_ref[n:n + 1, :] = jnp.where(cnt < float(MOBA_TOPK), 0.0, NEG_INF)

        m_acc = None
        for j in range(i + 1):
            s = _dot(kn_ref[j * blk:(j + 1) * blk, :], qt)
            if j == i:
                s = s + causal_bias
            elif i > MOBA_TOPK:
                s = s + bias_ref[j:j + 1, :]
            s_ref[j * blk:(j + 1) * blk, :] = s
            part = jnp.max(s.reshape(groups, V7X_SUBLANES, blk), axis=0)
            m_acc = part if m_acc is None else jnp.maximum(m_acc, part)
        m_row = jnp.max(m_acc, axis=0, keepdims=True)

        l_acc = None
        for j in range(i + 1):
            p = jnp.exp(s_ref[j * blk:(j + 1) * blk, :] - m_row)
            part = jnp.sum(p.reshape(groups, V7X_SUBLANES, blk), axis=0)
            l_acc = part if l_acc is None else l_acc + part
            p_ref[j * blk:(j + 1) * blk, :] = p.astype(BF16)
        l_row = jnp.sum(l_acc, axis=0, keepdims=True)
        o_t = _dot(v_ref[:, 0:(i + 1) * blk], p_ref[0:(i + 1) * blk, :]) / l_row
        o_ref[i * blk:(i + 1) * blk, :] = o_t.T.astype(o_ref.dtype)


def _moba(qkv_t, batch, seq):
    nblk = seq // MOBA_BLOCK
    hb = MOBA_HEADS
    return pl.pallas_call(
        functools.partial(_moba_kernel, nblk=nblk),
        grid=(batch, MOBA_HEADS),
        in_specs=[pl.BlockSpec((None, HEAD_DIM, seq), lambda b, h: (b, h, 0)),
                  pl.BlockSpec((None, HEAD_DIM, seq), lambda b, h: (b, hb + h, 0)),
                  pl.BlockSpec((None, HEAD_DIM, seq), lambda b, h: (b, 2 * hb + h, 0))],
        out_specs=pl.BlockSpec((seq, HEAD_DIM), lambda b, h: (b, h)),
        out_shape=jax.ShapeDtypeStruct((batch * seq, MOBA_WIDTH), BF16),
        scratch_shapes=[pltpu.VMEM((seq, HEAD_DIM), BF16),
                        pltpu.VMEM((nblk, HEAD_DIM), F32),
                        pltpu.VMEM((seq, MOBA_BLOCK), F32),
                        pltpu.VMEM((seq, MOBA_BLOCK), BF16),
                        pltpu.VMEM((nblk, MOBA_BLOCK), F32)],
        compiler_params=_cparams("arbitrary", "arbitrary"),
        name="moba_attention",
    )(qkv_t, qkv_t, qkv_t)


def _merge_kernel(a_ref, b_ref, wa_ref, wb_ref, ga_ref, gb_ref, o_ref):
    ya = _dot(a_ref[...], wa_ref[...])
    yb = _dot(b_ref[...], wb_ref[...])
    o_ref[...] = (ga_ref[...] * ya + gb_ref[...] * yb).astype(o_ref.dtype)


def _merge(ya_in, yb_in, wa, wb, gates, tm=512, tn=1024):
    m, k = ya_in.shape
    n = wa.shape[1]
    nb = n // tn
    return pl.pallas_call(
        _merge_kernel,
        grid=(nb, m // tm),
        in_specs=[pl.BlockSpec((tm, k), lambda j, i: (i, 0)),
                  pl.BlockSpec((tm, k), lambda j, i: (i, 0)),
                  pl.BlockSpec((k, tn), lambda j, i: (0, j)),
                  pl.BlockSpec((k, tn), lambda j, i: (0, j)),
                  pl.BlockSpec((tm, tn), lambda j, i: (i, j)),
                  pl.BlockSpec((tm, tn), lambda j, i: (i, nb + j))],
        out_specs=pl.BlockSpec((tm, tn), lambda j, i: (i, j)),
        out_shape=jax.ShapeDtypeStruct((m, n), BF16),
        compiler_params=_cparams("arbitrary", "arbitrary"),
        name="branch_merge",
    )(ya_in, yb_in, wa, wb, gates, gates)


def _outproj_kernel(a_ref, w_ref, x_ref, nw_ref, x1_ref, h_ref):
    x1 = x_ref[...] + _dot(a_ref[...], w_ref[...])
    x1_ref[...] = x1
    h = x1 * lax.rsqrt(jnp.mean(x1 * x1, axis=-1, keepdims=True) + NORM_EPS) * nw_ref[...]
    h_ref[...] = h.astype(h_ref.dtype)


def _outproj(a, w, x2d, nw, tm=256):
    m, k = a.shape
    n = w.shape[1]
    return pl.pallas_call(
        _outproj_kernel,
        grid=(m // tm,),
        in_specs=[pl.BlockSpec((tm, k), lambda i: (i, 0)),
                  pl.BlockSpec((k, n), lambda i: (0, 0)),
                  pl.BlockSpec((tm, n), lambda i: (i, 0)),
                  pl.BlockSpec((1, n), lambda i: (0, 0))],
        out_specs=[pl.BlockSpec((tm, n), lambda i: (i, 0)),
                   pl.BlockSpec((tm, n), lambda i: (i, 0))],
        out_shape=[jax.ShapeDtypeStruct((m, n), F32),
                   jax.ShapeDtypeStruct((m, n), BF16)],
        compiler_params=_cparams("arbitrary"),
        name="out_proj_norm2",
    )(a, w, x2d, nw.reshape(1, n))


def _upproj_kernel(h_ref, wg_ref, wv_ref, cg_ref, cv_ref, bg_ref, bv_ref, o_ref,
                   carry_g, carry_v, *, tiles_per_seq, sub):
    m = pl.program_id(1)

    @pl.when(m % tiles_per_seq == 0)
    def _():
        carry_g[...] = jnp.zeros_like(carry_g)
        carry_v[...] = jnp.zeros_like(carry_v)

    prev_g = carry_g[...]
    prev_v = carry_v[...]
    for r in range(h_ref.shape[0] // sub):
        h = h_ref[r * sub:(r + 1) * sub, :]
        ug = _dot(h, wg_ref[...])
        uv = _dot(h, wv_ref[...])
        yg = _causal_conv(ug, prev_g, cg_ref[...], FFN_CONV) + bg_ref[...]
        yv = _causal_conv(uv, prev_v, cv_ref[...], FFN_CONV) + bv_ref[...]
        prev_g = ug[sub - V7X_SUBLANES:sub, :]
        prev_v = uv[sub - V7X_SUBLANES:sub, :]
        o_ref[r * sub:(r + 1) * sub, :] = (_silu(yg) * yv).astype(o_ref.dtype)
    carry_g[...] = prev_g
    carry_v[...] = prev_v


def _upproj(h, w_up, conv_w, conv_b, seq, tm=1024, tn=512, sub=256):
    m, k = h.shape
    nb = D_FF // tn
    return pl.pallas_call(
        functools.partial(_upproj_kernel, tiles_per_seq=seq // tm, sub=sub),
        grid=(nb, m // tm),
        in_specs=[pl.BlockSpec((tm, k), lambda j, i: (i, 0)),
                  pl.BlockSpec((k, tn), lambda j, i: (0, j)),
                  pl.BlockSpec((k, tn), lambda j, i: (0, nb + j)),
                  pl.BlockSpec((FFN_CONV, tn), lambda j, i: (0, j)),
                  pl.BlockSpec((FFN_CONV, tn), lambda j, i: (0, nb + j)),
                  pl.BlockSpec((1, tn), lambda j, i: (0, j)),
                  pl.BlockSpec((1, tn), lambda j, i: (0, nb + j))],
        out_specs=pl.BlockSpec((tm, tn), lambda j, i: (i, j)),
        out_shape=jax.ShapeDtypeStruct((m, D_FF), BF16),
        scratch_shapes=[pltpu.VMEM((V7X_SUBLANES, tn), F32),
                        pltpu.VMEM((V7X_SUBLANES, tn), F32)],
        compiler_params=_cparams("arbitrary", "arbitrary"),
        name="up_proj_conv_gate",
    )(h, w_up, w_up, conv_w, conv_w, conv_b, conv_b)


def _downproj_kernel(a_ref, w_ref, x_ref, o_ref):
    o_ref[...] = x_ref[...] + _dot(a_ref[...], w_ref[...])


def _downproj(a, w, x1, tm=512, tn=1024):
    m, k = a.shape
    n = w.shape[1]
    return pl.pallas_call(
        _downproj_kernel,
        grid=(n // tn, m // tm),
        in_specs=[pl.BlockSpec((tm, k), lambda j, i: (i, 0)),
                  pl.BlockSpec((k, tn), lambda j, i: (0, j)),
                  pl.BlockSpec((tm, tn), lambda j, i: (i, j))],
        out_specs=pl.BlockSpec((tm, tn), lambda j, i: (i, j)),
        out_shape=jax.ShapeDtypeStruct((m, n), F32),
        compiler_params=_cparams("arbitrary", "arbitrary"),
        name="down_proj",
    )(a, w, x1)


def _pad_lanes(vec, offset):
    return jnp.zeros((1, V7X_LANES), F32).at[0, offset:offset + vec.shape[0]].set(vec.astype(F32))


def kernel(x, positions, ln1, w_in, gdn_conv, gdn_a_log, gdn_dt_bias, gdn_norm, w_branch_a, w_branch_b,
           w_out, ln2, w_up, ffn_conv, ffn_conv_bias, w_down, final_norm):
    batch, seq, d = x.shape
    m = batch * seq
    depth = ln1.shape[0]
    o_qkv_a = 3 * GDN_WIDTH
    o_small = o_qkv_a + 2 * GDN_HEADS
    o_z = o_small + GDN_WIDTH
    o_qkv_b = o_z + 3 * MOBA_WIDTH
    inv_freq = (ROPE_THETA ** (-jnp.arange(ROPE_HALF, dtype=F32) / ROPE_HALF)).reshape(ROPE_HALF, 1)

    x2d = x.reshape(m, d)
    for l in range(depth):
        w = w_in[l]
        w_qkv_a = w[:, :o_qkv_a].astype(BF16)
        w_small = jnp.pad(w[:, o_qkv_a:o_small], ((0, 0), (0, V7X_LANES - 2 * GDN_HEADS))).astype(BF16)
        w_z = w[:, o_small:o_z].astype(BF16)
        w_qkv_b_t = w[:, o_z:o_qkv_b].T.astype(BF16)
        w_gates = w[:, o_qkv_b:].astype(BF16)

        h1, h1_t = _norm_with_transpose(x2d, ln1[l], batch, seq)
        qkv_a = _gdn_proj(h1, w_qkv_a, gdn_conv[l], seq)
        z_a = _proj(h1, w_z, act=None, out_dtype=F32, tm=1024, tn=1024, name="z_proj")
        small = _proj(h1, w_small, act=None, out_dtype=F32, tm=1024, tn=V7X_LANES, name="beta_decay_proj")
        gates = _proj(h1, w_gates, act="sigmoid", out_dtype=F32, tm=1024, tn=1024, name="gate_proj")
        qkv_b_t = _moba_proj(w_qkv_b_t, h1_t, positions, inv_freq)

        gdn_out = _gdn(qkv_a, z_a, small, _pad_lanes(gdn_a_log[l], GDN_HEADS), _pad_lanes(gdn_dt_bias[l], GDN_HEADS),
                       gdn_norm[l].reshape(1, HEAD_DIM), batch, seq)
        attn = _moba(qkv_b_t, batch, seq)
        merged = _merge(gdn_out, attn, w_branch_a[l].astype(BF16), w_branch_b[l].astype(BF16), gates)
        x1, h2 = _outproj(merged, w_out[l].astype(BF16), x2d, ln2[l])
        act = _upproj(h2, w_up[l].astype(BF16), ffn_conv[l], ffn_conv_bias[l].reshape(1, 2 * D_FF), seq)
        x2d = _downproj(act, w_down[l].astype(BF16), x1)
    return _final_norm(x2d, final_norm).reshape(batch, seq, d)
```

```python
import functools
import math

import jax
import jax.numpy as jnp
import numpy as np
from jax import lax
from jax.experimental import pallas as pl
from jax.experimental.pallas import tpu as pltpu

F32 = jnp.float32
BF16 = jnp.bfloat16

D_MODEL = 2048
GDN_HEADS = 8
HEAD_DIM = 128
GDN_WIDTH = GDN_HEADS * HEAD_DIM
GDN_CONV = 4
GDN_CHUNK = 64
MOBA_HEADS = 8
MOBA_WIDTH = MOBA_HEADS * HEAD_DIM
MOBA_BLOCK = 256
MOBA_TOPK = 3
ROPE_THETA = 500000.0
ROPE_DIM = HEAD_DIM // 4
ROPE_HALF = ROPE_DIM // 2
D_FF = 5632
FFN_CONV = 3
NORM_EPS = 1e-6
L2_EPS = 1e-6
NEG_INF = -1e30

V7X_LANES = 128
V7X_SUBLANES = 8
V7X_VMEM_LIMIT_BYTES = 56 * 1024 * 1024

GDN_GROUP = 4 * GDN_CHUNK
MOBA_Q_SCALE = HEAD_DIM ** -0.5 * math.log2(math.e)


def _cparams(*sem):
    return pltpu.CompilerParams(dimension_semantics=sem, vmem_limit_bytes=V7X_VMEM_LIMIT_BYTES)


def _dot(a, b):
    return jnp.dot(a, b, preferred_element_type=F32)


def _dot_nt(a, b):
    return lax.dot_general(a, b, (((1,), (1,)), ((), ())), preferred_element_type=F32)


def _dot_hi(a, b):
    return jnp.dot(a, b, preferred_element_type=F32, precision=lax.Precision.HIGHEST)


def _split3(x):
    hi = x.astype(BF16)
    r1 = x - hi.astype(F32)
    mid = r1.astype(BF16)
    lo = (r1 - mid.astype(F32)).astype(BF16)
    return hi, mid, lo


def _sigmoid(x):
    return 1.0 / (1.0 + jnp.exp(-x))


def _silu(x):
    return x * _sigmoid(x)


def _shift_rows(cur, prev8, s):
    if s == 0:
        return cur
    rolled = pltpu.roll(cur, s, axis=0)
    rolled_prev = pltpu.roll(prev8, s, axis=0)
    row = lax.broadcasted_iota(jnp.int32, prev8.shape, 0)
    first = jnp.where(row < s, rolled_prev, rolled[0:V7X_SUBLANES])
    return jnp.concatenate([first, rolled[V7X_SUBLANES:]], axis=0)


def _causal_conv(cur, prev8, cw, width):
    y = cw[width - 1:width, :] * cur
    for j in range(width - 1):
        y = y + cw[j:j + 1, :] * _shift_rows(cur, prev8, width - 1 - j)
    return y


def _norm_kernel(x_ref, w_ref, h_ref, ht_ref):
    x = x_ref[...]
    y = x * lax.rsqrt(jnp.mean(x * x, axis=-1, keepdims=True) + NORM_EPS) * w_ref[...]
    h_ref[...] = y.astype(h_ref.dtype)
    ht_ref[...] = y.T.astype(ht_ref.dtype)


def _norm_with_transpose(x2d, w, batch, seq, tm=256):
    m, d = x2d.shape
    per_seq = seq // tm
    return pl.pallas_call(
        _norm_kernel,
        grid=(m // tm,),
        in_specs=[pl.BlockSpec((tm, d), lambda i: (i, 0)),
                  pl.BlockSpec((1, d), lambda i: (0, 0))],
        out_specs=[pl.BlockSpec((tm, d), lambda i: (i, 0)),
                   pl.BlockSpec((None, d, tm), lambda i: (i // per_seq, 0, i % per_seq))],
        out_shape=[jax.ShapeDtypeStruct((m, d), BF16),
                   jax.ShapeDtypeStruct((batch, d, seq), BF16)],
        compiler_params=_cparams("arbitrary"),
        name="norm1",
    )(x2d, w.reshape(1, d))


def _final_norm_kernel(x_ref, w_ref, o_ref):
    x = x_ref[...]
    o_ref[...] = x * lax.rsqrt(jnp.mean(x * x, axis=-1, keepdims=True) + NORM_EPS) * w_ref[...]


def _final_norm(x2d, w, tm=512):
    m, d = x2d.shape
    return pl.pallas_call(
        _final_norm_kernel,
        grid=(m // tm,),
        in_specs=[pl.BlockSpec((tm, d), lambda i: (i, 0)),
                  pl.BlockSpec((1, d), lambda i: (0, 0))],
        out_specs=pl.BlockSpec((tm, d), lambda i: (i, 0)),
        out_shape=jax.ShapeDtypeStruct((m, d), F32),
        compiler_params=_cparams("arbitrary"),
        name="final_norm",
    )(x2d, w.reshape(1, d))


def _proj_kernel(h_ref, w_ref, o_ref, *, act):
    acc = _dot(h_ref[...], w_ref[...])
    if act == "sigmoid":
        acc = _sigmoid(acc)
    o_ref[...] = acc.astype(o_ref.dtype)


def _proj(h, w, *, act, out_dtype, tm, tn, name):
    m, k = h.shape
    n = w.shape[1]
    return pl.pallas_call(
        functools.partial(_proj_kernel, act=act),
        grid=(n // tn, m // tm),
        in_specs=[pl.BlockSpec((tm, k), lambda j, i: (i, 0)),
                  pl.BlockSpec((k, tn), lambda j, i: (0, j))],
        out_specs=pl.BlockSpec((tm, tn), lambda j, i: (i, j)),
        out_shape=jax.ShapeDtypeStruct((m, n), out_dtype),
        compiler_params=_cparams("arbitrary", "arbitrary"),
        name=name,
    )(h, w)


def _gdn_proj_kernel(h_ref, w_ref, cw_ref, o_ref, carry_ref, *, tiles_per_seq, tiles_per_part, sub):
    n = pl.program_id(0)
    m = pl.program_id(1)
    tn = w_ref.shape[1]

    @pl.when(m % tiles_per_seq == 0)
    def _():
        carry_ref[...] = jnp.zeros_like(carry_ref)

    part = n // tiles_per_part
    q_scale = jnp.where(part == 0, HEAD_DIM ** -0.5, 1.0).astype(F32)
    prev = carry_ref[...]
    for r in range(h_ref.shape[0] // sub):
        acc = _dot(h_ref[r * sub:(r + 1) * sub, :], w_ref[...])
        y = _silu(_causal_conv(acc, prev, cw_ref[...], GDN_CONV))
        prev = acc[sub - V7X_SUBLANES:sub, :]
        heads = []
        for hd in range(tn // HEAD_DIM):
            blk = y[:, hd * HEAD_DIM:(hd + 1) * HEAD_DIM]
            nrm = blk * lax.rsqrt(jnp.sum(blk * blk, axis=-1, keepdims=True) + L2_EPS)
            heads.append(jnp.where(part < 2, nrm * q_scale, blk))
        o_ref[r * sub:(r + 1) * sub, :] = jnp.concatenate(heads, axis=1)
    carry_ref[...] = prev


def _gdn_proj(h, w, conv_w, seq, tm=1024, tn=512, sub=256):
    m, k = h.shape
    n = w.shape[1]
    return pl.pallas_call(
        functools.partial(_gdn_proj_kernel, tiles_per_seq=seq // tm, tiles_per_part=GDN_WIDTH // tn, sub=sub),
        grid=(n // tn, m // tm),
        in_specs=[pl.BlockSpec((tm, k), lambda j, i: (i, 0)),
                  pl.BlockSpec((k, tn), lambda j, i: (0, j)),
                  pl.BlockSpec((GDN_CONV, tn), lambda j, i: (0, j))],
        out_specs=pl.BlockSpec((tm, tn), lambda j, i: (i, j)),
        out_shape=jax.ShapeDtypeStruct((m, n), F32),
        scratch_shapes=[pltpu.VMEM((V7X_SUBLANES, tn), F32)],
        compiler_params=_cparams("arbitrary", "arbitrary"),
        name="gdn_qkv_proj",
    )(h, w, conv_w)


def _moba_proj_kernel(wt_ref, ht_ref, pos_ref, freq_ref, o_ref):
    part = pl.program_id(0)
    acc = _dot(wt_ref[...], ht_ref[...])
    ang = freq_ref[...] * pos_ref[...].astype(F32)
    roped = part < 2
    cos = jnp.where(roped, jnp.cos(ang), 1.0)
    sin = jnp.where(roped, jnp.sin(ang), 0.0)
    scale = jnp.where(part == 0, MOBA_Q_SCALE, 1.0).astype(F32)
    rows = []
    for hd in range(acc.shape[0] // HEAD_DIM):
        base = hd * HEAD_DIM
        x1 = acc[base:base + ROPE_HALF]
        x2 = acc[base + ROPE_HALF:base + ROPE_DIM]
        rows.append(x1 * cos - x2 * sin)
        rows.append(x2 * cos + x1 * sin)
        rows.append(acc[base + ROPE_DIM:base + HEAD_DIM])
    o_ref[...] = (jnp.concatenate(rows, axis=0) * scale).astype(o_ref.dtype)


def _moba_proj(wt, ht, positions, inv_freq, tt=512):
    batch, k, seq = ht.shape
    rows = wt.shape[0]
    tr = MOBA_WIDTH
    return pl.pallas_call(
        _moba_proj_kernel,
        grid=(rows // tr, batch, seq // tt),
        in_specs=[pl.BlockSpec((tr, k), lambda r, b, t: (r, 0)),
                  pl.BlockSpec((None, k, tt), lambda r, b, t: (b, 0, t)),
                  pl.BlockSpec((None, 1, tt), lambda r, b, t: (b, 0, t)),
                  pl.BlockSpec((ROPE_HALF, 1), lambda r, b, t: (0, 0))],
        out_specs=pl.BlockSpec((None, tr, tt), lambda r, b, t: (b, r, t)),
        out_shape=jax.ShapeDtypeStruct((batch, rows, seq), BF16),
        compiler_params=_cparams("arbitrary", "arbitrary", "arbitrary"),
        name="moba_qkv_proj",
    )(wt, ht, positions.reshape(batch, 1, seq), inv_freq)


def _gdn_kernel(q_ref, k_ref, v_ref, z_ref, sm_ref, a_ref, dt_ref, nw_ref, o_ref,
                state_ref, lbd_ref, subd_ref, *, groups, heads_per_step):
    t = pl.program_id(2)
    g = GDN_GROUP
    c = GDN_CHUNK

    @pl.when(t == 0)
    def _():
        state_ref[...] = jnp.zeros_like(state_ref)

    ri = lax.broadcasted_iota(jnp.int32, (g, g), 0)
    ci = lax.broadcasted_iota(jnp.int32, (g, g), 1)
    shift = int(math.log2(c))
    same = jnp.right_shift(ri, shift) == jnp.right_shift(ci, shift)
    lbd_ref[...] = jnp.where(same & (ci <= ri), 1.0, 0.0)
    subd_ref[...] = jnp.where(same & (ri > ci), 1.0, 0.0)
    lane = lax.broadcasted_iota(jnp.int32, (g, V7X_LANES), 1)
    eye = jnp.where(ri == ci, 1.0, 0.0)

    def group_body(gi, carry):
        hs = range(heads_per_step)
        nch = g // c
        r0 = pl.multiple_of(gi * g, g)
        sm = sm_ref[pl.ds(r0, g), :]
        beta_all = _sigmoid(sm)
        xs = sm + dt_ref[...]
        softplus = jnp.maximum(xs, 0.0) + jnp.log1p(jnp.exp(-jnp.abs(xs)))
        g_all = -jnp.exp(a_ref[...]) * softplus
        lbd = lbd_ref[...]
        lbd16 = lbd.astype(BF16)
        strict = subd_ref[...]
        cols = [slice(hh * HEAD_DIM, (hh + 1) * HEAD_DIM) for hh in hs]
        heads = [pl.program_id(1) * heads_per_step + hh for hh in hs]

        k = [k_ref[pl.ds(r0, g), cols[hh]] for hh in hs]
        k16 = [k[hh].astype(BF16) for hh in hs]
        beta_b = [jnp.broadcast_to(jnp.sum(jnp.where(lane == heads[hh], beta_all, 0.0), axis=-1, keepdims=True),
                                   (g, HEAD_DIM)) for hh in hs]
        g_b = [jnp.broadcast_to(jnp.sum(jnp.where(lane == GDN_HEADS + heads[hh], g_all, 0.0), axis=-1, keepdims=True),
                                (g, HEAD_DIM)) for hh in hs]
        kb = [k[hh] * beta_b[hh] for hh in hs]

        cs = [_dot(lbd16, jnp.concatenate(_split3(g_b[hh]), axis=1)) for hh in hs]
        kk = [_dot_nt(kb[hh].astype(BF16), k16[hh]) for hh in hs]
        qk_raw = [_dot_nt(q_ref[pl.ds(r0, g), cols[hh]].astype(BF16), k16[hh]) for hh in hs]
        gc_b = [cs[hh][:, :HEAD_DIM] + cs[hh][:, HEAD_DIM:2 * HEAD_DIM] + cs[hh][:, 2 * HEAD_DIM:] for hh in hs]
        gamma = []
        for hh in hs:
            gc_row = gc_b[hh].T[0:1, :]
            dmat = jnp.where(lbd > 0.0, jnp.concatenate([gc_b[hh], gc_b[hh]], axis=1) - gc_row, 0.0)
            gamma.append(jnp.exp(dmat))
        eg = [jnp.exp(gc_b[hh]) for hh in hs]

        p = [-jnp.where(strict > 0.0, kk[hh] * gamma[hh], 0.0) for hh in hs]
        t_mat = [eye + p[hh] for hh in hs]
        for _ in range(5):
            p16 = [p[hh].astype(BF16) for hh in hs]
            p = [_dot(p16[hh], p16[hh]) for hh in hs]
            tp = [_dot(t_mat[hh].astype(BF16), p[hh].astype(BF16)) for hh in hs]
            t_mat = [t_mat[hh] + tp[hh] for hh in hs]
        uw = [_dot(t_mat[hh].astype(BF16),
                   jnp.concatenate([v_ref[pl.ds(r0, g), cols[hh]] * beta_b[hh], kb[hh] * eg[hh]], axis=1).astype(BF16))
              for hh in hs]
        u = [uw[hh][:, :HEAD_DIM] for hh in hs]
        w16 = [uw[hh][:, HEAD_DIM:].astype(BF16) for hh in hs]
        qk16 = [jnp.where(lbd > 0.0, qk_raw[hh] * gamma[hh], 0.0).astype(BF16) for hh in hs]
        qg = [(q_ref[pl.ds(r0, g), cols[hh]] * eg[hh]).astype(BF16) for hh in hs]
        kd_t = []
        for hh in hs:
            gc_last = jnp.concatenate(
                [jnp.broadcast_to(gc_b[hh][(ch + 1) * c - 1:(ch + 1) * c, :], (c, HEAD_DIM)) for ch in range(nch)],
                axis=0)
            kd_t.append((k[hh] * jnp.exp(gc_last - gc_b[hh])).T.astype(BF16))

        state = [state_ref[hh] for hh in hs]
        outs = [[] for _ in hs]
        zeros_c = jnp.zeros((c, HEAD_DIM), F32)
        for ch in range(nch):
            rows = slice(ch * c, (ch + 1) * c)
            s16 = [state[hh].astype(BF16) for hh in hs]
            ws = [_dot(w16[hh][rows], s16[hh]) for hh in hs]
            qs = [_dot(qg[hh][rows], s16[hh]) for hh in hs]
            vn_all = [jnp.concatenate([zeros_c] * ch + [u[hh][rows] - ws[hh]] + [zeros_c] * (nch - 1 - ch),
                                      axis=0).astype(BF16) for hh in hs]
            sv = [_dot(kd_t[hh], vn_all[hh]) for hh in hs]
            ov = [_dot(qk16[hh][rows], vn_all[hh]) for hh in hs]
            for hh in hs:
                outs[hh].append(qs[hh] + ov[hh])
                state[hh] = state[hh] * eg[hh][(ch + 1) * c - 1:(ch + 1) * c, :] + sv[hh]
        for hh in hs:
            state_ref[hh] = state[hh]
            o = jnp.concatenate(outs[hh], axis=0)
            o = o * lax.rsqrt(jnp.mean(o * o, axis=-1, keepdims=True) + NORM_EPS) * nw_ref[...]
            o_ref[pl.ds(r0, g), cols[hh]] = (o * _silu(z_ref[pl.ds(r0, g), cols[hh]])).astype(o_ref.dtype)
        return carry

    lax.fori_loop(0, groups, group_body, 0)


def _gdn(qkv, z, small, a_pad, dt_pad, norm_w, batch, seq, ts=512, heads_per_step=8):
    m = qkv.shape[0]
    tiles = seq // ts
    hp = heads_per_step
    hb = GDN_HEADS // hp
    wd = hp * HEAD_DIM
    row = lambda b, h, t: b * tiles + t
    return pl.pallas_call(
        functools.partial(_gdn_kernel, groups=ts // GDN_GROUP, heads_per_step=hp),
        grid=(batch, hb, tiles),
        in_specs=[pl.BlockSpec((ts, wd), lambda b, h, t: (row(b, h, t), h)),
                  pl.BlockSpec((ts, wd), lambda b, h, t: (row(b, h, t), hb + h)),
                  pl.BlockSpec((ts, wd), lambda b, h, t: (row(b, h, t), 2 * hb + h)),
                  pl.BlockSpec((ts, wd), lambda b, h, t: (row(b, h, t), h)),
                  pl.BlockSpec((ts, V7X_LANES), lambda b, h, t: (row(b, h, t), 0)),
                  pl.BlockSpec((1, V7X_LANES), lambda b, h, t: (0, 0)),
                  pl.BlockSpec((1, V7X_LANES), lambda b, h, t: (0, 0)),
                  pl.BlockSpec((1, HEAD_DIM), lambda b, h, t: (0, 0))],
        out_specs=pl.BlockSpec((ts, wd), lambda b, h, t: (row(b, h, t), h)),
        out_shape=jax.ShapeDtypeStruct((m, GDN_WIDTH), BF16),
        scratch_shapes=[pltpu.VMEM((hp, HEAD_DIM, HEAD_DIM), F32),
                        pltpu.VMEM((GDN_GROUP, GDN_GROUP), F32),
                        pltpu.VMEM((GDN_GROUP, GDN_GROUP), F32)],
        compiler_params=_cparams("arbitrary", "arbitrary", "arbitrary"),
        name="gated_delta_rule",
    )(qkv, qkv, qkv, z, small, a_pad, dt_pad, norm_w)


def _moba_kernel(q_ref, k_ref, v_ref, o_ref, kn_ref, km_ref, s_ref, p_ref, bias_ref, *, nblk):
    blk = MOBA_BLOCK
    for j in range(nblk):
        kt = k_ref[:, j * blk:(j + 1) * blk].astype(F32)
        kn = kt.T
        kn_ref[j * blk:(j + 1) * blk, :] = kn.astype(BF16)
        km_ref[j:j + 1, :] = jnp.sum(kn, axis=0, keepdims=True) * (1.0 / blk)

    key_i = lax.broadcasted_iota(jnp.int32, (blk, blk), 0)
    qry_i = lax.broadcasted_iota(jnp.int32, (blk, blk), 1)
    causal_bias = jnp.where(key_i <= qry_i, 0.0, NEG_INF)
    groups = blk // V7X_SUBLANES

    def scores(i):
        s_buf = s_ref.at[i % 2]
        qt = q_ref[:, i * blk:(i + 1) * blk]
        if i > MOBA_TOPK:
            gate = _dot_hi(km_ref[...], qt.astype(F32))
            n_iota = lax.broadcasted_iota(jnp.int32, (nblk, blk), 0)
            valid = n_iota < i
            for n in range(i):
                g_n = gate[n:n + 1, :]
                beats = valid & ((gate > g_n) | ((gate == g_n) & (n_iota < n)))
                cnt = jnp.sum(jnp.where(beats, 1.0, 0.0), axis=0, keepdims=True)
                bias_ref[n:n + 1, :] = jnp.where(cnt < float(MOBA_TOPK), 0.0, NEG_INF)
        m_acc = None
        for j in range(i + 1):
            s = _dot(kn_ref[j * blk:(j + 1) * blk, :], qt)
            if j == i:
                s = s + causal_bias
            elif i > MOBA_TOPK:
                s = s + bias_ref[j:j + 1, :]
            s_buf[j * blk:(j + 1) * blk, :] = s
            part = jnp.max(s.reshape(groups, V7X_SUBLANES, blk), axis=0)
            m_acc = part if m_acc is None else jnp.maximum(m_acc, part)
        return jnp.max(m_acc, axis=0, keepdims=True)

    def finish(i, m_row):
        s_buf = s_ref.at[i % 2]
        l_acc = None
        for j in range(i + 1):
            p = jnp.exp2(s_buf[j * blk:(j + 1) * blk, :] - m_row)
            part = jnp.sum(p.reshape(groups, V7X_SUBLANES, blk), axis=0)
            l_acc = part if l_acc is None else l_acc + part
            p_ref[j * blk:(j + 1) * blk, :] = p.astype(BF16)
        l_row = jnp.sum(l_acc, axis=0, keepdims=True)
        o_t = _dot(v_ref[:, 0:(i + 1) * blk], p_ref[0:(i + 1) * blk, :]) / l_row
        o_ref[i * blk:(i + 1) * blk, :] = o_t.T.astype(o_ref.dtype)

    m_next = scores(0)
    for i in range(nblk):
        m_row = m_next
        if i + 1 < nblk:
            m_next = scores(i + 1)
        finish(i, m_row)


def _moba(qkv_t, batch, seq):
    nblk = seq // MOBA_BLOCK
    hb = MOBA_HEADS
    return pl.pallas_call(
        functools.partial(_moba_kernel, nblk=nblk),
        grid=(batch, MOBA_HEADS),
        in_specs=[pl.BlockSpec((None, HEAD_DIM, seq), lambda b, h: (b, h, 0)),
                  pl.BlockSpec((None, HEAD_DIM, seq), lambda b, h: (b, hb + h, 0)),
                  pl.BlockSpec((None, HEAD_DIM, seq), lambda b, h: (b, 2 * hb + h, 0))],
        out_specs=pl.BlockSpec((seq, HEAD_DIM), lambda b, h: (b, h)),
        out_shape=jax.ShapeDtypeStruct((batch * seq, MOBA_WIDTH), BF16),
        scratch_shapes=[pltpu.VMEM((seq, HEAD_DIM), BF16),
                        pltpu.VMEM((nblk, HEAD_DIM), F32),
                        pltpu.VMEM((2, seq, MOBA_BLOCK), F32),
                        pltpu.VMEM((seq, MOBA_BLOCK), BF16),
                        pltpu.VMEM((nblk, MOBA_BLOCK), F32)],
        compiler_params=_cparams("arbitrary", "arbitrary"),
        name="moba_attention",
    )(qkv_t, qkv_t, qkv_t)


def _merge_kernel(a_ref, b_ref, wa_ref, wb_ref, ga_ref, gb_ref, o_ref):
    ya = _dot(a_ref[...], wa_ref[...])
    yb = _dot(b_ref[...], wb_ref[...])
    o_ref[...] = (ga_ref[...] * ya + gb_ref[...] * yb).astype(o_ref.dtype)


def _merge(ya_in, yb_in, wa, wb, gates, tm=512, tn=1024):
    m, k = ya_in.shape
    n = wa.shape[1]
    nb = n // tn
    return pl.pallas_call(
        _merge_kernel,
        grid=(nb, m // tm),
        in_specs=[pl.BlockSpec((tm, k), lambda j, i: (i, 0)),
                  pl.BlockSpec((tm, k), lambda j, i: (i, 0)),
                  pl.BlockSpec((k, tn), lambda j, i: (0, j)),
                  pl.BlockSpec((k, tn), lambda j, i: (0, j)),
                  pl.BlockSpec((tm, tn), lambda j, i: (i, j)),
                  pl.BlockSpec((tm, tn), lambda j, i: (i, nb + j))],
        out_specs=pl.BlockSpec((tm, tn), lambda j, i: (i, j)),
        out_shape=jax.ShapeDtypeStruct((m, n), BF16),
        compiler_params=_cparams("arbitrary", "arbitrary"),
        name="branch_merge",
    )(ya_in, yb_in, wa, wb, gates, gates)


def _outproj_kernel(a_ref, w_ref, x_ref, nw_ref, x1_ref, h_ref):
    x1 = x_ref[...] + _dot(a_ref[...], w_ref[...])
    x1_ref[...] = x1
    h = x1 * lax.rsqrt(jnp.mean(x1 * x1, axis=-1, keepdims=True) + NORM_EPS) * nw_ref[...]
    h_ref[...] = h.astype(h_ref.dtype)


def _outproj(a, w, x2d, nw, tm=256):
    m, k = a.shape
    n = w.shape[1]
    return pl.pallas_call(
        _outproj_kernel,
        grid=(m // tm,),
        in_specs=[pl.BlockSpec((tm, k), lambda i: (i, 0)),
                  pl.BlockSpec((k, n), lambda i: (0, 0)),
                  pl.BlockSpec((tm, n), lambda i: (i, 0)),
                  pl.BlockSpec((1, n), lambda i: (0, 0))],
        out_specs=[pl.BlockSpec((tm, n), lambda i: (i, 0)),
                   pl.BlockSpec((tm, n), lambda i: (i, 0))],
        out_shape=[jax.ShapeDtypeStruct((m, n), F32),
                   jax.ShapeDtypeStruct((m, n), BF16)],
        compiler_params=_cparams("arbitrary"),
        name="out_proj_norm2",
    )(a, w, x2d, nw.reshape(1, n))


def _upproj_kernel(h_ref, wg_ref, wv_ref, cg_ref, cv_ref, bg_ref, bv_ref, o_ref,
                   carry_g, carry_v, *, tiles_per_seq, sub):
    m = pl.program_id(1)

    @pl.when(m % tiles_per_seq == 0)
    def _():
        carry_g[...] = jnp.zeros_like(carry_g)
        carry_v[...] = jnp.zeros_like(carry_v)

    prev_g = carry_g[...]
    prev_v = carry_v[...]
    for r in range(h_ref.shape[0] // sub):
        h = h_ref[r * sub:(r + 1) * sub, :]
        ug = _dot(h, wg_ref[...])
        uv = _dot(h, wv_ref[...])
        yg = _causal_conv(ug, prev_g, cg_ref[...], FFN_CONV) + bg_ref[...]
        yv = _causal_conv(uv, prev_v, cv_ref[...], FFN_CONV) + bv_ref[...]
        prev_g = ug[sub - V7X_SUBLANES:sub, :]
        prev_v = uv[sub - V7X_SUBLANES:sub, :]
        o_ref[r * sub:(r + 1) * sub, :] = (_silu(yg) * yv).astype(o_ref.dtype)
    carry_g[...] = prev_g
    carry_v[...] = prev_v


def _upproj(h, w_up, conv_w, conv_b, seq, tm=1024, tn=512, sub=256):
    m, k = h.shape
    nb = D_FF // tn
    return pl.pallas_call(
        functools.partial(_upproj_kernel, tiles_per_seq=seq // tm, sub=sub),
        grid=(nb, m // tm),
        in_specs=[pl.BlockSpec((tm, k), lambda j, i: (i, 0)),
                  pl.BlockSpec((k, tn), lambda j, i: (0, j)),
                  pl.BlockSpec((k, tn), lambda j, i: (0, nb + j)),
                  pl.BlockSpec((FFN_CONV, tn), lambda j, i: (0, j)),
                  pl.BlockSpec((FFN_CONV, tn), lambda j, i: (0, nb + j)),
                  pl.BlockSpec((1, tn), lambda j, i: (0, j)),
                  pl.BlockSpec((1, tn), lambda j, i: (0, nb + j))],
        out_specs=pl.BlockSpec((tm, tn), lambda j, i: (i, j)),
        out_shape=jax.ShapeDtypeStruct((m, D_FF), BF16),
        scratch_shapes=[pltpu.VMEM((V7X_SUBLANES, tn), F32),
                        pltpu.VMEM((V7X_SUBLANES, tn), F32)],
        compiler_params=_cparams("arbitrary", "arbitrary"),
        name="up_proj_conv_gate",
    )(h, w_up, w_up, conv_w, conv_w, conv_b, conv_b)


def _downproj_kernel(a_ref, w_ref, x_ref, o_ref):
    o_ref[...] = x_ref[...] + _dot(a_ref[...], w_ref[...])


def _downproj(a, w, x1, tm=512, tn=1024):
    m, k = a.shape
    n = w.shape[1]
    return pl.pallas_call(
        _downproj_kernel,
        grid=(n // tn, m // tm),
        in_specs=[pl.BlockSpec((tm, k), lambda j, i: (i, 0)),
                  pl.BlockSpec((k, tn), lambda j, i: (0, j)),
                  pl.BlockSpec((tm, tn), lambda j, i: (i, j))],
        out_specs=pl.BlockSpec((tm, tn), lambda j, i: (i, j)),
        out_shape=jax.ShapeDtypeStruct((m, n), F32),
        compiler_params=_cparams("arbitrary", "arbitrary"),
        name="down_proj",
    )(a, w, x1)


def _downproj_norm_kernel(a_ref, w_ref, x_ref, nw_ref, o_ref):
    x2 = x_ref[...] + _dot(a_ref[...], w_ref[...])
    o_ref[...] = x2 * lax.rsqrt(jnp.mean(x2 * x2, axis=-1, keepdims=True) + NORM_EPS) * nw_ref[...]


def _downproj_norm(a, w, x1, nw, tm=512):
    m, k = a.shape
    n = w.shape[1]
    return pl.pallas_call(
        _downproj_norm_kernel,
        grid=(m // tm,),
        in_specs=[pl.BlockSpec((tm, k), lambda i: (i, 0)),
                  pl.BlockSpec((k, n), lambda i: (0, 0)),
                  pl.BlockSpec((tm, n), lambda i: (i, 0)),
                  pl.BlockSpec((1, n), lambda i: (0, 0))],
        out_specs=pl.BlockSpec((tm, n), lambda i: (i, 0)),
        out_shape=jax.ShapeDtypeStruct((m, n), F32),
        compiler_params=_cparams("arbitrary"),
        name="down_proj_final_norm",
    )(a, w, x1, nw.reshape(1, n))


def _pad_lanes(vec, offset):
    return jnp.zeros((1, V7X_LANES), F32).at[0, offset:offset + vec.shape[0]].set(vec.astype(F32))


def kernel(x, positions, ln1, w_in, gdn_conv, gdn_a_log, gdn_dt_bias, gdn_norm, w_branch_a, w_branch_b,
           w_out, ln2, w_up, ffn_conv, ffn_conv_bias, w_down, final_norm):
    batch, seq, d = x.shape
    m = batch * seq
    depth = ln1.shape[0]
    o_qkv_a = 3 * GDN_WIDTH
    o_small = o_qkv_a + 2 * GDN_HEADS
    o_z = o_small + GDN_WIDTH
    o_qkv_b = o_z + 3 * MOBA_WIDTH
    inv_freq = (ROPE_THETA ** (-jnp.arange(ROPE_HALF, dtype=F32) / ROPE_HALF)).reshape(ROPE_HALF, 1)

    x2d = x.reshape(m, d)
    for l in range(depth):
        w = w_in[l]
        w_qkv_a = w[:, :o_qkv_a].astype(BF16)
        w_small = jnp.pad(w[:, o_qkv_a:o_small], ((0, 0), (0, V7X_LANES - 2 * GDN_HEADS))).astype(BF16)
        w_z = w[:, o_small:o_z].astype(BF16)
        w_qkv_b_t = w[:, o_z:o_qkv_b].T.astype(BF16)
        w_gates = w[:, o_qkv_b:].astype(BF16)

        h1, h1_t = _norm_with_transpose(x2d, ln1[l], batch, seq)
        qkv_a = _gdn_proj(h1, w_qkv_a, gdn_conv[l], seq)
        z_a = _proj(h1, w_z, act=None, out_dtype=F32, tm=1024, tn=1024, name="z_proj")
        small = _proj(h1, w_small, act=None, out_dtype=F32, tm=1024, tn=V7X_LANES, name="beta_decay_proj")
        gates = _proj(h1, w_gates, act="sigmoid", out_dtype=BF16, tm=1024, tn=1024, name="gate_proj")
        qkv_b_t = _moba_proj(w_qkv_b_t, h1_t, positions, inv_freq)

        gdn_out = _gdn(qkv_a, z_a, small, _pad_lanes(gdn_a_log[l], GDN_HEADS), _pad_lanes(gdn_dt_bias[l], GDN_HEADS),
                       gdn_norm[l].reshape(1, HEAD_DIM), batch, seq)
        attn = _moba(qkv_b_t, batch, seq)
        merged = _merge(gdn_out, attn, w_branch_a[l].astype(BF16), w_branch_b[l].astype(BF16), gates)
        x1, h2 = _outproj(merged, w_out[l].astype(BF16), x2d, ln2[l])
        act = _upproj(h2, w_up[l].astype(BF16), ffn_conv[l], ffn_conv_bias[l].reshape(1, 2 * D_FF), seq)
        if l == depth - 1:
            return _downproj_norm(act, w_down[l].astype(BF16), x1, final_norm).reshape(batch, seq, d)
        x2d = _downproj(act, w_down[l].astype(BF16), x1)
    return _final_norm(x2d, final_norm).reshape(batch, seq, d)
```

```python
import functools
import math

import jax
import jax.numpy as jnp
import numpy as np
from jax import lax
from jax.experimental import pallas as pl
from jax.experimental.pallas import tpu as pltpu

F32 = jnp.float32
BF16 = jnp.bfloat16

D_MODEL = 2048
GDN_HEADS = 8
HEAD_DIM = 128
GDN_WIDTH = GDN_HEADS * HEAD_DIM
GDN_CONV = 4
GDN_CHUNK = 64
MOBA_HEADS = 8
MOBA_WIDTH = MOBA_HEADS * HEAD_DIM
MOBA_BLOCK = 256
MOBA_TOPK = 3
ROPE_THETA = 500000.0
ROPE_DIM = HEAD_DIM // 4
ROPE_HALF = ROPE_DIM // 2
D_FF = 5632
FFN_CONV = 3
NORM_EPS = 1e-6
L2_EPS = 1e-6
NEG_INF = -1e30
NEG_LOG2_E = -math.log2(math.e)

V7X_LANES = 128
V7X_SUBLANES = 8
V7X_VMEM_LIMIT_BYTES = 56 * 1024 * 1024

GDN_GROUP = 4 * GDN_CHUNK
MOBA_Q_SCALE = HEAD_DIM ** -0.5 * math.log2(math.e)


def _cparams(*sem):
    return pltpu.CompilerParams(dimension_semantics=sem, vmem_limit_bytes=V7X_VMEM_LIMIT_BYTES)


def _dot(a, b):
    return jnp.dot(a, b, preferred_element_type=F32)


def _dot_nt(a, b):
    return lax.dot_general(a, b, (((1,), (1,)), ((), ())), preferred_element_type=F32)


def _dot_hi(a, b):
    return jnp.dot(a, b, preferred_element_type=F32, precision=lax.Precision.HIGHEST)


def _split3(x):
    hi = x.astype(BF16)
    r1 = x - hi.astype(F32)
    mid = r1.astype(BF16)
    lo = (r1 - mid.astype(F32)).astype(BF16)
    return hi, mid, lo


def _sigmoid(x):
    return 1.0 / (1.0 + jnp.exp2(x * NEG_LOG2_E))


def _silu(x):
    return x * _sigmoid(x)


def _shift_rows(cur, prev8, s):
    if s == 0:
        return cur
    rolled = pltpu.roll(cur, s, axis=0)
    rolled_prev = pltpu.roll(prev8, s, axis=0)
    row = lax.broadcasted_iota(jnp.int32, prev8.shape, 0)
    first = jnp.where(row < s, rolled_prev, rolled[0:V7X_SUBLANES])
    return jnp.concatenate([first, rolled[V7X_SUBLANES:]], axis=0)


def _causal_conv(cur, prev8, cw, width):
    y = cw[width - 1:width, :] * cur
    for j in range(width - 1):
        y = y + cw[j:j + 1, :] * _shift_rows(cur, prev8, width - 1 - j)
    return y


def _norm_kernel(x_ref, w_ref, h_ref):
    x = x_ref[...]
    y = x * lax.rsqrt(jnp.mean(x * x, axis=-1, keepdims=True) + NORM_EPS) * w_ref[...]
    h_ref[...] = y.astype(h_ref.dtype)


def _norm_cast(x2d, w, tm=512):
    m, d = x2d.shape
    return pl.pallas_call(
        _norm_kernel,
        grid=(m // tm,),
        in_specs=[pl.BlockSpec((tm, d), lambda i: (i, 0)),
                  pl.BlockSpec((1, d), lambda i: (0, 0))],
        out_specs=pl.BlockSpec((tm, d), lambda i: (i, 0)),
        out_shape=jax.ShapeDtypeStruct((m, d), BF16),
        compiler_params=_cparams("arbitrary"),
        name="norm1",
    )(x2d, w.reshape(1, d))


def _final_norm_kernel(x_ref, w_ref, o_ref):
    x = x_ref[...]
    o_ref[...] = x * lax.rsqrt(jnp.mean(x * x, axis=-1, keepdims=True) + NORM_EPS) * w_ref[...]


def _final_norm(x2d, w, tm=512):
    m, d = x2d.shape
    return pl.pallas_call(
        _final_norm_kernel,
        grid=(m // tm,),
        in_specs=[pl.BlockSpec((tm, d), lambda i: (i, 0)),
                  pl.BlockSpec((1, d), lambda i: (0, 0))],
        out_specs=pl.BlockSpec((tm, d), lambda i: (i, 0)),
        out_shape=jax.ShapeDtypeStruct((m, d), F32),
        compiler_params=_cparams("arbitrary"),
        name="final_norm",
    )(x2d, w.reshape(1, d))


def _proj_kernel(h_ref, w_ref, o_ref, *, act):
    acc = _dot(h_ref[...], w_ref[...])
    if act == "sigmoid":
        acc = _sigmoid(acc)
    o_ref[...] = acc.astype(o_ref.dtype)


def _proj(h, w, *, act, out_dtype, tm, tn, name):
    m, k = h.shape
    n = w.shape[1]
    return pl.pallas_call(
        functools.partial(_proj_kernel, act=act),
        grid=(n // tn, m // tm),
        in_specs=[pl.BlockSpec((tm, k), lambda j, i: (i, 0)),
                  pl.BlockSpec((k, tn), lambda j, i: (0, j))],
        out_specs=pl.BlockSpec((tm, tn), lambda j, i: (i, j)),
        out_shape=jax.ShapeDtypeStruct((m, n), out_dtype),
        compiler_params=_cparams("arbitrary", "arbitrary"),
        name=name,
    )(h, w)


def _gdn_proj_kernel(h_ref, w_ref, cw_ref, o_ref, w16_ref, carry_ref, *, tiles_per_seq, tiles_per_part, sub):
    n = pl.program_id(0)
    m = pl.program_id(1)
    tn = w_ref.shape[1]

    @pl.when(m == 0)
    def _():
        w16_ref[...] = w_ref[...].astype(BF16)

    @pl.when(m % tiles_per_seq == 0)
    def _():
        carry_ref[...] = jnp.zeros_like(carry_ref)

    part = n // tiles_per_part
    q_scale = jnp.where(part == 0, HEAD_DIM ** -0.5, 1.0).astype(F32)
    prev = carry_ref[...]
    for r in range(h_ref.shape[0] // sub):
        acc = _dot(h_ref[r * sub:(r + 1) * sub, :], w16_ref[...])
        y = _silu(_causal_conv(acc, prev, cw_ref[...], GDN_CONV))
        prev = acc[sub - V7X_SUBLANES:sub, :]
        heads = []
        for hd in range(tn // HEAD_DIM):
            blk = y[:, hd * HEAD_DIM:(hd + 1) * HEAD_DIM]
            inv = lax.rsqrt(jnp.sum(blk * blk, axis=-1, keepdims=True) + L2_EPS) * q_scale
            heads.append(blk * jnp.where(part < 2, inv, 1.0))
        o_ref[r * sub:(r + 1) * sub, :] = jnp.concatenate(heads, axis=1)
    carry_ref[...] = prev


def _gdn_proj(h, w_in, conv_w, seq, tm=1024, tn=1024, sub=128):
    m, k = h.shape
    n = 3 * GDN_WIDTH
    return pl.pallas_call(
        functools.partial(_gdn_proj_kernel, tiles_per_seq=seq // tm, tiles_per_part=GDN_WIDTH // tn, sub=sub),
        grid=(n // tn, m // tm),
        in_specs=[pl.BlockSpec((tm, k), lambda j, i: (i, 0)),
                  pl.BlockSpec((k, tn), lambda j, i: (0, j)),
                  pl.BlockSpec((GDN_CONV, tn), lambda j, i: (0, j))],
        out_specs=pl.BlockSpec((tm, tn), lambda j, i: (i, j)),
        out_shape=jax.ShapeDtypeStruct((m, n), F32),
        scratch_shapes=[pltpu.VMEM((k, tn), BF16),
                        pltpu.VMEM((V7X_SUBLANES, tn), F32)],
        compiler_params=_cparams("arbitrary", "arbitrary"),
        name="gdn_qkv_proj",
    )(h, w_in, conv_w)


def _moba_proj_kernel(h_ref, w_ref, pos_ref, freq_ref, o_ref):
    part = pl.program_id(0)
    scale = jnp.where(part == 0, MOBA_Q_SCALE, 1.0).astype(F32)
    acc = (_dot(h_ref[...], w_ref[...]) * scale).T
    ang = freq_ref[...] * pos_ref[...].astype(F32)
    roped = part < 2
    cos = jnp.where(roped, jnp.cos(ang), 1.0)
    sin = jnp.where(roped, jnp.sin(ang), 0.0)
    rows = []
    for hd in range(acc.shape[0] // HEAD_DIM):
        base = hd * HEAD_DIM
        x1 = acc[base:base + ROPE_HALF]
        x2 = acc[base + ROPE_HALF:base + ROPE_DIM]
        rows.append(x1 * cos - x2 * sin)
        rows.append(x2 * cos + x1 * sin)
        rows.append(acc[base + ROPE_DIM:base + HEAD_DIM])
    o_ref[...] = jnp.concatenate(rows, axis=0).astype(o_ref.dtype)


def _moba_proj(h, w, positions, inv_freq, batch, seq, tt=512):
    m, k = h.shape
    tr = MOBA_WIDTH
    per_seq = seq // tt
    return pl.pallas_call(
        _moba_proj_kernel,
        grid=(w.shape[1] // tr, m // tt),
        in_specs=[pl.BlockSpec((tt, k), lambda r, i: (i, 0)),
                  pl.BlockSpec((k, tr), lambda r, i: (0, r)),
                  pl.BlockSpec((None, 1, tt), lambda r, i: (i // per_seq, 0, i % per_seq)),
                  pl.BlockSpec((ROPE_HALF, 1), lambda r, i: (0, 0))],
        out_specs=pl.BlockSpec((None, tr, tt), lambda r, i: (i // per_seq, r, i % per_seq)),
        out_shape=jax.ShapeDtypeStruct((batch, w.shape[1], seq), BF16),
        compiler_params=_cparams("arbitrary", "arbitrary"),
        name="moba_qkv_proj",
    )(h, w, positions.reshape(batch, 1, seq), inv_freq)


def _gdn_kernel(q_ref, k_ref, v_ref, z_ref, sm_ref, a_ref, dt_ref, nw_ref, o_ref,
                state_ref, lbd_ref, subd_ref, *, groups, heads_per_step):
    t = pl.program_id(2)
    g = GDN_GROUP
    c = GDN_CHUNK

    @pl.when(t == 0)
    def _():
        state_ref[...] = jnp.zeros_like(state_ref)

    ri = lax.broadcasted_iota(jnp.int32, (g, g), 0)
    ci = lax.broadcasted_iota(jnp.int32, (g, g), 1)
    shift = int(math.log2(c))
    same = jnp.right_shift(ri, shift) == jnp.right_shift(ci, shift)
    lbd_ref[...] = jnp.where(same & (ci <= ri), 1.0, 0.0)
    subd_ref[...] = jnp.where(same & (ri > ci), 1.0, 0.0)
    lane = lax.broadcasted_iota(jnp.int32, (g, V7X_LANES), 1)
    eye = jnp.where(ri == ci, 1.0, 0.0)

    def group_body(gi, carry):
        hs = range(heads_per_step)
        nch = g // c
        r0 = pl.multiple_of(gi * g, g)
        sm = sm_ref[pl.ds(r0, g), :]
        beta_all = _sigmoid(sm)
        xs = sm + dt_ref[...]
        softplus = jnp.maximum(xs, 0.0) + jnp.log1p(jnp.exp(-jnp.abs(xs)))
        g_all = -jnp.exp(a_ref[...]) * softplus
        lbd = lbd_ref[...]
        lbd16 = lbd.astype(BF16)
        strict = subd_ref[...]
        cols = [slice(hh * HEAD_DIM, (hh + 1) * HEAD_DIM) for hh in hs]
        heads = [pl.program_id(1) * heads_per_step + hh for hh in hs]

        k = [k_ref[pl.ds(r0, g), cols[hh]] for hh in hs]
        k16 = [k[hh].astype(BF16) for hh in hs]
        beta_b = [jnp.broadcast_to(jnp.sum(jnp.where(lane == heads[hh], beta_all, 0.0), axis=-1, keepdims=True),
                                   (g, HEAD_DIM)) for hh in hs]
        g_b = [jnp.broadcast_to(jnp.sum(jnp.where(lane == GDN_HEADS + heads[hh], g_all, 0.0), axis=-1, keepdims=True),
                                (g, HEAD_DIM)) for hh in hs]
        kb = [k[hh] * beta_b[hh] for hh in hs]

        cs = [_dot(lbd16, jnp.concatenate(_split3(g_b[hh]), axis=1)) for hh in hs]
        kk = [_dot_nt(kb[hh].astype(BF16), k16[hh]) for hh in hs]
        qk_raw = [_dot_nt(q_ref[pl.ds(r0, g), cols[hh]].astype(BF16), k16[hh]) for hh in hs]
        gc_b = [cs[hh][:, :HEAD_DIM] + cs[hh][:, HEAD_DIM:2 * HEAD_DIM] + cs[hh][:, 2 * HEAD_DIM:] for hh in hs]
        gamma = []
        for hh in hs:
            gc_row = gc_b[hh].T[0:1, :]
            dmat = jnp.where(lbd > 0.0, jnp.concatenate([gc_b[hh], gc_b[hh]], axis=1) - gc_row, 0.0)
            gamma.append(jnp.exp(dmat))
        eg = [jnp.exp(gc_b[hh]) for hh in hs]

        p = [-jnp.where(strict > 0.0, kk[hh] * gamma[hh], 0.0) for hh in hs]
        t_mat = [eye + p[hh] for hh in hs]
        for _ in range(5):
            p16 = [p[hh].astype(BF16) for hh in hs]
            p = [_dot(p16[hh], p16[hh]) for hh in hs]
            tp = [_dot(t_mat[hh].astype(BF16), p[hh].astype(BF16)) for hh in hs]
            t_mat = [t_mat[hh] + tp[hh] for hh in hs]
        uw = [_dot(t_mat[hh].astype(BF16),
                   jnp.concatenate([v_ref[pl.ds(r0, g), cols[hh]] * beta_b[hh], kb[hh] * eg[hh]], axis=1).astype(BF16))
              for hh in hs]
        u = [uw[hh][:, :HEAD_DIM] for hh in hs]
        w16 = [uw[hh][:, HEAD_DIM:].astype(BF16) for hh in hs]
        qk16 = [jnp.where(lbd > 0.0, qk_raw[hh] * gamma[hh], 0.0).astype(BF16) for hh in hs]
        qg = [(q_ref[pl.ds(r0, g), cols[hh]] * eg[hh]).astype(BF16) for hh in hs]
        kd_t = []
        for hh in hs:
            gc_last = jnp.concatenate(
                [jnp.broadcast_to(gc_b[hh][(ch + 1) * c - 1:(ch + 1) * c, :], (c, HEAD_DIM)) for ch in range(nch)],
                axis=0)
            kd_t.append((k[hh] * jnp.exp(gc_last - gc_b[hh])).T.astype(BF16))

        state = [state_ref[hh] for hh in hs]
        outs = [[] for _ in hs]
        zeros_c = jnp.zeros((c, HEAD_DIM), F32)
        for ch in range(nch):
            rows = slice(ch * c, (ch + 1) * c)
            s16 = [state[hh].astype(BF16) for hh in hs]
            ws = [_dot(w16[hh][rows], s16[hh]) for hh in hs]
            qs = [_dot(qg[hh][rows], s16[hh]) for hh in hs]
            vn_all = [jnp.concatenate([zeros_c] * ch + [u[hh][rows] - ws[hh]] + [zeros_c] * (nch - 1 - ch),
                                      axis=0).astype(BF16) for hh in hs]
            sv = [_dot(kd_t[hh], vn_all[hh]) for hh in hs]
            ov = [_dot(qk16[hh][rows], vn_all[hh]) for hh in hs]
            for hh in hs:
                outs[hh].append(qs[hh] + ov[hh])
                state[hh] = state[hh] * eg[hh][(ch + 1) * c - 1:(ch + 1) * c, :] + sv[hh]
        for hh in hs:
            state_ref[hh] = state[hh]
            o = jnp.concatenate(outs[hh], axis=0)
            o = o * lax.rsqrt(jnp.mean(o * o, axis=-1, keepdims=True) + NORM_EPS) * nw_ref[...]
            o_ref[pl.ds(r0, g), cols[hh]] = (o * _silu(z_ref[pl.ds(r0, g), cols[hh]])).astype(o_ref.dtype)
        return carry

    lax.fori_loop(0, groups, group_body, 0)


def _gdn(qkv, z, small, a_pad, dt_pad, norm_w, batch, seq, ts=512, heads_per_step=8):
    m = qkv.shape[0]
    tiles = seq // ts
    hp = heads_per_step
    hb = GDN_HEADS // hp
    wd = hp * HEAD_DIM
    row = lambda b, h, t: b * tiles + t
    return pl.pallas_call(
        functools.partial(_gdn_kernel, groups=ts // GDN_GROUP, heads_per_step=hp),
        grid=(batch, hb, tiles),
        in_specs=[pl.BlockSpec((ts, wd), lambda b, h, t: (row(b, h, t), h)),
                  pl.BlockSpec((ts, wd), lambda b, h, t: (row(b, h, t), hb + h)),
                  pl.BlockSpec((ts, wd), lambda b, h, t: (row(b, h, t), 2 * hb + h)),
                  pl.BlockSpec((ts, wd), lambda b, h, t: (row(b, h, t), h)),
                  pl.BlockSpec((ts, V7X_LANES), lambda b, h, t: (row(b, h, t), 0)),
                  pl.BlockSpec((1, V7X_LANES), lambda b, h, t: (0, 0)),
                  pl.BlockSpec((1, V7X_LANES), lambda b, h, t: (0, 0)),
                  pl.BlockSpec((1, HEAD_DIM), lambda b, h, t: (0, 0))],
        out_specs=pl.BlockSpec((ts, wd), lambda b, h, t: (row(b, h, t), h)),
        out_shape=jax.ShapeDtypeStruct((m, GDN_WIDTH), BF16),
        scratch_shapes=[pltpu.VMEM((hp, HEAD_DIM, HEAD_DIM), F32),
                        pltpu.VMEM((GDN_GROUP, GDN_GROUP), F32),
                        pltpu.VMEM((GDN_GROUP, GDN_GROUP), F32)],
        compiler_params=_cparams("arbitrary", "arbitrary", "arbitrary"),
        name="gated_delta_rule",
    )(qkv, qkv, qkv, z, small, a_pad, dt_pad, norm_w)


def _moba_kernel(q_ref, k_ref, v_ref, o_ref, kn_ref, km_ref, s_ref, p_ref, bias_ref, *, nblk):
    blk = MOBA_BLOCK
    for j in range(nblk):
        kt = k_ref[:, j * blk:(j + 1) * blk].astype(F32)
        kn = kt.T
        kn_ref[j * blk:(j + 1) * blk, :] = kn.astype(BF16)
        km_ref[j:j + 1, :] = jnp.sum(kn, axis=0, keepdims=True) * (1.0 / blk)

    key_i = lax.broadcasted_iota(jnp.int32, (blk, blk), 0)
    qry_i = lax.broadcasted_iota(jnp.int32, (blk, blk), 1)
    causal_bias = jnp.where(key_i <= qry_i, 0.0, NEG_INF)
    groups = blk // V7X_SUBLANES

    def scores(i):
        s_buf = s_ref.at[i % 2]
        qt = q_ref[:, i * blk:(i + 1) * blk]
        if i > MOBA_TOPK:
            gate = _dot_hi(km_ref[...], qt.astype(F32))
            n_iota = lax.broadcasted_iota(jnp.int32, (nblk, blk), 0)
            valid = n_iota < i
            for n in range(i):
                g_n = gate[n:n + 1, :]
                beats = valid & ((gate > g_n) | ((gate == g_n) & (n_iota < n)))
                cnt = jnp.sum(jnp.where(beats, 1.0, 0.0), axis=0, keepdims=True)
                bias_ref[n:n + 1, :] = jnp.where(cnt < float(MOBA_TOPK), 0.0, NEG_INF)
        m_acc = None
        for j in range(i + 1):
            s = _dot(kn_ref[j * blk:(j + 1) * blk, :], qt)
            if j == i:
                s = s + causal_bias
            elif i > MOBA_TOPK:
                s = s + bias_ref[j:j + 1, :]
            s_buf[j * blk:(j + 1) * blk, :] = s
            part = jnp.max(s.reshape(groups, V7X_SUBLANES, blk), axis=0)
            m_acc = part if m_acc is None else jnp.maximum(m_acc, part)
        return jnp.max(m_acc, axis=0, keepdims=True)

    def finish(i, m_row):
        s_buf = s_ref.at[i % 2]
        l_acc = None
        for j in range(i + 1):
            p = jnp.exp2(s_buf[j * blk:(j + 1) * blk, :] - m_row)
            part = jnp.sum(p.reshape(groups, V7X_SUBLANES, blk), axis=0)
            l_acc = part if l_acc is None else l_acc + part
            p_ref[j * blk:(j + 1) * blk, :] = p.astype(BF16)
        l_row = jnp.sum(l_acc, axis=0, keepdims=True)
        o_t = _dot(v_ref[:, 0:(i + 1) * blk], p_ref[0:(i + 1) * blk, :]) / l_row
        o_ref[i * blk:(i + 1) * blk, :] = o_t.T.astype(o_ref.dtype)

    m_next = scores(0)
    for i in range(nblk):
        m_row = m_next
        if i + 1 < nblk:
            m_next = scores(i + 1)
        finish(i, m_row)


def _moba(qkv_t, batch, seq):
    nblk = seq // MOBA_BLOCK
    hb = MOBA_HEADS
    return pl.pallas_call(
        functools.partial(_moba_kernel, nblk=nblk),
        grid=(batch, MOBA_HEADS),
        in_specs=[pl.BlockSpec((None, HEAD_DIM, seq), lambda b, h: (b, h, 0)),
                  pl.BlockSpec((None, HEAD_DIM, seq), lambda b, h: (b, hb + h, 0)),
                  pl.BlockSpec((None, HEAD_DIM, seq), lambda b, h: (b, 2 * hb + h, 0))],
        out_specs=pl.BlockSpec((seq, HEAD_DIM), lambda b, h: (b, h)),
        out_shape=jax.ShapeDtypeStruct((batch * seq, MOBA_WIDTH), BF16),
        scratch_shapes=[pltpu.VMEM((seq, HEAD_DIM), BF16),
                        pltpu.VMEM((nblk, HEAD_DIM), F32),
                        pltpu.VMEM((2, seq, MOBA_BLOCK), F32),
                        pltpu.VMEM((seq, MOBA_BLOCK), BF16),
                        pltpu.VMEM((nblk, MOBA_BLOCK), F32)],
        compiler_params=_cparams("arbitrary", "arbitrary"),
        name="moba_attention",
    )(qkv_t, qkv_t, qkv_t)


def _merge_kernel(a_ref, b_ref, wa_ref, wb_ref, ga_ref, gb_ref, o_ref):
    ya = _dot(a_ref[...], wa_ref[...])
    yb = _dot(b_ref[...], wb_ref[...])
    o_ref[...] = (ga_ref[...] * ya + gb_ref[...] * yb).astype(o_ref.dtype)


def _merge(ya_in, yb_in, wa, wb, gates, tm=512, tn=1024):
    m, k = ya_in.shape
    n = wa.shape[1]
    nb = n // tn
    return pl.pallas_call(
        _merge_kernel,
        grid=(nb, m // tm),
        in_specs=[pl.BlockSpec((tm, k), lambda j, i: (i, 0)),
                  pl.BlockSpec((tm, k), lambda j, i: (i, 0)),
                  pl.BlockSpec((k, tn), lambda j, i: (0, j)),
                  pl.BlockSpec((k, tn), lambda j, i: (0, j)),
                  pl.BlockSpec((tm, tn), lambda j, i: (i, j)),
                  pl.BlockSpec((tm, tn), lambda j, i: (i, nb + j))],
        out_specs=pl.BlockSpec((tm, tn), lambda j, i: (i, j)),
        out_shape=jax.ShapeDtypeStruct((m, n), BF16),
        compiler_params=_cparams("arbitrary", "arbitrary"),
        name="branch_merge",
    )(ya_in, yb_in, wa, wb, gates, gates)


def _outproj_kernel(a_ref, w_ref, x_ref, nw_ref, x1_ref, h_ref):
    x1 = x_ref[...] + _dot(a_ref[...], w_ref[...])
    x1_ref[...] = x1
    h = x1 * lax.rsqrt(jnp.mean(x1 * x1, axis=-1, keepdims=True) + NORM_EPS) * nw_ref[...]
    h_ref[...] = h.astype(h_ref.dtype)


def _outproj(a, w, x2d, nw, tm=256):
    m, k = a.shape
    n = w.shape[1]
    return pl.pallas_call(
        _outproj_kernel,
        grid=(m // tm,),
        in_specs=[pl.BlockSpec((tm, k), lambda i: (i, 0)),
                  pl.BlockSpec((k, n), lambda i: (0, 0)),
                  pl.BlockSpec((tm, n), lambda i: (i, 0)),
                  pl.BlockSpec((1, n), lambda i: (0, 0))],
        out_specs=[pl.BlockSpec((tm, n), lambda i: (i, 0)),
                   pl.BlockSpec((tm, n), lambda i: (i, 0))],
        out_shape=[jax.ShapeDtypeStruct((m, n), F32),
                   jax.ShapeDtypeStruct((m, n), BF16)],
        compiler_params=_cparams("arbitrary"),
        name="out_proj_norm2",
    )(a, w, x2d, nw.reshape(1, n))


def _upproj_kernel(h_ref, wg_ref, wv_ref, cg_ref, cv_ref, bg_ref, bv_ref, o_ref,
                   w_ref, carry_ref, *, tiles_per_seq, sub):
    m = pl.program_id(1)
    tn = wg_ref.shape[1]

    @pl.when(m == 0)
    def _():
        w_ref[:, :tn] = wg_ref[...].astype(BF16)
        w_ref[:, tn:] = wv_ref[...].astype(BF16)

    @pl.when(m % tiles_per_seq == 0)
    def _():
        carry_ref[...] = jnp.zeros_like(carry_ref)

    cw = jnp.concatenate([cg_ref[...], cv_ref[...]], axis=1)
    bias = jnp.concatenate([bg_ref[...], bv_ref[...]], axis=1)
    prev = carry_ref[...]
    for r in range(h_ref.shape[0] // sub):
        u = _dot(h_ref[r * sub:(r + 1) * sub, :], w_ref[...])
        y = _causal_conv(u, prev, cw, FFN_CONV) + bias
        prev = u[sub - V7X_SUBLANES:sub, :]
        o_ref[r * sub:(r + 1) * sub, :] = (_silu(y[:, :tn]) * y[:, tn:]).astype(o_ref.dtype)
    carry_ref[...] = prev


def _upproj(h, w_up, conv_w, conv_b, seq, tm=1024, tn=512, sub=128):
    m, k = h.shape
    nb = D_FF // tn
    return pl.pallas_call(
        functools.partial(_upproj_kernel, tiles_per_seq=seq // tm, sub=sub),
        grid=(nb, m // tm),
        in_specs=[pl.BlockSpec((tm, k), lambda j, i: (i, 0)),
                  pl.BlockSpec((k, tn), lambda j, i: (0, j)),
                  pl.BlockSpec((k, tn), lambda j, i: (0, nb + j)),
                  pl.BlockSpec((FFN_CONV, tn), lambda j, i: (0, j)),
                  pl.BlockSpec((FFN_CONV, tn), lambda j, i: (0, nb + j)),
                  pl.BlockSpec((1, tn), lambda j, i: (0, j)),
                  pl.BlockSpec((1, tn), lambda j, i: (0, nb + j))],
        out_specs=pl.BlockSpec((tm, tn), lambda j, i: (i, j)),
        out_shape=jax.ShapeDtypeStruct((m, D_FF), BF16),
        scratch_shapes=[pltpu.VMEM((k, 2 * tn), BF16),
                        pltpu.VMEM((V7X_SUBLANES, 2 * tn), F32)],
        compiler_params=_cparams("arbitrary", "arbitrary"),
        name="up_proj_conv_gate",
    )(h, w_up, w_up, conv_w, conv_w, conv_b, conv_b)


def _downproj_kernel(a_ref, w_ref, x_ref, o_ref):
    o_ref[...] = x_ref[...] + _dot(a_ref[...], w_ref[...])


def _downproj(a, w, x1, tm=512, tn=1024):
    m, k = a.shape
    n = w.shape[1]
    return pl.pallas_call(
        _downproj_kernel,
        grid=(n // tn, m // tm),
        in_specs=[pl.BlockSpec((tm, k), lambda j, i: (i, 0)),
                  pl.BlockSpec((k, tn), lambda j, i: (0, j)),
                  pl.BlockSpec((tm, tn), lambda j, i: (i, j))],
        out_specs=pl.BlockSpec((tm, tn), lambda j, i: (i, j)),
        out_shape=jax.ShapeDtypeStruct((m, n), F32),
        compiler_params=_cparams("arbitrary", "arbitrary"),
        name="down_proj",
    )(a, w, x1)


def _downproj_norm_kernel(a_ref, w_ref, x_ref, nw_ref, o_ref):
    x2 = x_ref[...] + _dot(a_ref[...], w_ref[...])
    o_ref[...] = x2 * lax.rsqrt(jnp.mean(x2 * x2, axis=-1, keepdims=True) + NORM_EPS) * nw_ref[...]


def _downproj_norm(a, w, x1, nw, tm=512):
    m, k = a.shape
    n = w.shape[1]
    return pl.pallas_call(
        _downproj_norm_kernel,
        grid=(m // tm,),
        in_specs=[pl.BlockSpec((tm, k), lambda i: (i, 0)),
                  pl.BlockSpec((k, n), lambda i: (0, 0)),
                  pl.BlockSpec((tm, n), lambda i: (i, 0)),
                  pl.BlockSpec((1, n), lambda i: (0, 0))],
        out_specs=pl.BlockSpec((tm, n), lambda i: (i, 0)),
        out_shape=jax.ShapeDtypeStruct((m, n), F32),
        compiler_params=_cparams("arbitrary"),
        name="down_proj_final_norm",
    )(a, w, x1, nw.reshape(1, n))


def _pad_lanes(vec, offset):
    return jnp.zeros((1, V7X_LANES), F32).at[0, offset:offset + vec.shape[0]].set(vec.astype(F32))


def kernel(x, positions, ln1, w_in, gdn_conv, gdn_a_log, gdn_dt_bias, gdn_norm, w_branch_a, w_branch_b,
           w_out, ln2, w_up, ffn_conv, ffn_conv_bias, w_down, final_norm):
    batch, seq, d = x.shape
    m = batch * seq
    depth = ln1.shape[0]
    o_qkv_a = 3 * GDN_WIDTH
    o_small = o_qkv_a + 2 * GDN_HEADS
    o_z = o_small + GDN_WIDTH
    o_qkv_b = o_z + 3 * MOBA_WIDTH
    inv_freq = (ROPE_THETA ** (-jnp.arange(ROPE_HALF, dtype=F32) / ROPE_HALF)).reshape(ROPE_HALF, 1)

    x2d = x.reshape(m, d)
    for l in range(depth):
        w = w_in[l]
        w_small = jnp.pad(w[:, o_qkv_a:o_small], ((0, 0), (0, V7X_LANES - 2 * GDN_HEADS))).astype(BF16)
        w_z = w[:, o_small:o_z].astype(BF16)
        w_qkv_b = w[:, o_z:o_qkv_b].astype(BF16)
        w_gates = w[:, o_qkv_b:].astype(BF16)

        h1 = _norm_cast(x2d, ln1[l])
        qkv_a = _gdn_proj(h1, w, gdn_conv[l], seq)
        z_a = _proj(h1, w_z, act=None, out_dtype=F32, tm=1024, tn=1024, name="z_proj")
        small = _proj(h1, w_small, act=None, out_dtype=F32, tm=1024, tn=V7X_LANES, name="beta_decay_proj")
        gates = _proj(h1, w_gates, act="sigmoid", out_dtype=BF16, tm=1024, tn=1024, name="gate_proj")
        qkv_b_t = _moba_proj(h1, w_qkv_b, positions, inv_freq, batch, seq)

        gdn_out = _gdn(qkv_a, z_a, small, _pad_lanes(gdn_a_log[l], GDN_HEADS), _pad_lanes(gdn_dt_bias[l], GDN_HEADS),
                       gdn_norm[l].reshape(1, HEAD_DIM), batch, seq)
        attn = _moba(qkv_b_t, batch, seq)
        merged = _merge(gdn_out, attn, w_branch_a[l].astype(BF16), w_branch_b[l].astype(BF16), gates)
        x1, h2 = _outproj(merged, w_out[l].astype(BF16), x2d, ln2[l])
        act = _upproj(h2, w_up[l], ffn_conv[l], ffn_conv_bias[l].reshape(1, 2 * D_FF), seq)
        if l == depth - 1:
            return _downproj_norm(act, w_down[l].astype(BF16), x1, final_norm).reshape(batch, seq, d)
        x2d = _downproj(act, w_down[l].astype(BF16), x1)
    return _final_norm(x2d, final_norm).reshape(batch, seq, d)
```

```python
import functools
import math

import jax
import jax.numpy as jnp
import numpy as np
from jax import lax
from jax.experimental import pallas as pl
from jax.experimental.pallas import tpu as pltpu

F32 = jnp.float32
BF16 = jnp.bfloat16

D_MODEL = 2048
GDN_HEADS = 8
HEAD_DIM = 128
GDN_WIDTH = GDN_HEADS * HEAD_DIM
GDN_CONV = 4
GDN_CHUNK = 64
MOBA_HEADS = 8
MOBA_WIDTH = MOBA_HEADS * HEAD_DIM
MOBA_BLOCK = 256
MOBA_TOPK = 3
ROPE_THETA = 500000.0
ROPE_DIM = HEAD_DIM // 4
ROPE_HALF = ROPE_DIM // 2
D_FF = 5632
FFN_CONV = 3
NORM_EPS = 1e-6
L2_EPS = 1e-6
NEG_INF = -1e30
NEG_LOG2_E = -math.log2(math.e)

V7X_LANES = 128
V7X_SUBLANES = 8
V7X_VMEM_LIMIT_BYTES = 56 * 1024 * 1024

GDN_GROUP = 4 * GDN_CHUNK
MOBA_Q_SCALE = HEAD_DIM ** -0.5 * math.log2(math.e)


def _cparams(*sem):
    return pltpu.CompilerParams(dimension_semantics=sem, vmem_limit_bytes=V7X_VMEM_LIMIT_BYTES)


def _dot(a, b):
    return jnp.dot(a, b, preferred_element_type=F32)


def _dot_nt(a, b):
    return lax.dot_general(a, b, (((1,), (1,)), ((), ())), preferred_element_type=F32)


def _dot_hi(a, b):
    return jnp.dot(a, b, preferred_element_type=F32, precision=lax.Precision.HIGHEST)


def _split3(x):
    hi = x.astype(BF16)
    r1 = x - hi.astype(F32)
    mid = r1.astype(BF16)
    lo = (r1 - mid.astype(F32)).astype(BF16)
    return hi, mid, lo


def _sigmoid(x):
    return 1.0 / (1.0 + jnp.exp2(x * NEG_LOG2_E))


def _silu(x):
    return x * _sigmoid(x)


def _shift_rows(cur, prev8, s):
    if s == 0:
        return cur
    rolled = pltpu.roll(cur, s, axis=0)
    rolled_prev = pltpu.roll(prev8, s, axis=0)
    row = lax.broadcasted_iota(jnp.int32, prev8.shape, 0)
    first = jnp.where(row < s, rolled_prev, rolled[0:V7X_SUBLANES])
    return jnp.concatenate([first, rolled[V7X_SUBLANES:]], axis=0)


def _causal_conv(cur, prev8, cw, width):
    y = cw[width - 1:width, :] * cur
    for j in range(width - 1):
        y = y + cw[j:j + 1, :] * _shift_rows(cur, prev8, width - 1 - j)
    return y


def _norm_kernel(x_ref, w_ref, h_ref):
    x = x_ref[...]
    y = x * lax.rsqrt(jnp.mean(x * x, axis=-1, keepdims=True) + NORM_EPS) * w_ref[...]
    h_ref[...] = y.astype(h_ref.dtype)


def _norm_cast(x2d, w, tm=512):
    m, d = x2d.shape
    return pl.pallas_call(
        _norm_kernel,
        grid=(m // tm,),
        in_specs=[pl.BlockSpec((tm, d), lambda i: (i, 0)),
                  pl.BlockSpec((1, d), lambda i: (0, 0))],
        out_specs=pl.BlockSpec((tm, d), lambda i: (i, 0)),
        out_shape=jax.ShapeDtypeStruct((m, d), BF16),
        compiler_params=_cparams("arbitrary"),
        name="norm1",
    )(x2d, w.reshape(1, d))


def _final_norm_kernel(x_ref, w_ref, o_ref):
    x = x_ref[...]
    o_ref[...] = x * lax.rsqrt(jnp.mean(x * x, axis=-1, keepdims=True) + NORM_EPS) * w_ref[...]


def _final_norm(x2d, w, tm=512):
    m, d = x2d.shape
    return pl.pallas_call(
        _final_norm_kernel,
        grid=(m // tm,),
        in_specs=[pl.BlockSpec((tm, d), lambda i: (i, 0)),
                  pl.BlockSpec((1, d), lambda i: (0, 0))],
        out_specs=pl.BlockSpec((tm, d), lambda i: (i, 0)),
        out_shape=jax.ShapeDtypeStruct((m, d), F32),
        compiler_params=_cparams("arbitrary"),
        name="final_norm",
    )(x2d, w.reshape(1, d))


def _proj_kernel(h_ref, w_ref, o_ref, *, act):
    acc = _dot(h_ref[...], w_ref[...])
    if act == "sigmoid":
        acc = _sigmoid(acc)
    o_ref[...] = acc.astype(o_ref.dtype)


def _proj(h, w, *, act, out_dtype, tm, tn, name):
    m, k = h.shape
    n = w.shape[1]
    return pl.pallas_call(
        functools.partial(_proj_kernel, act=act),
        grid=(n // tn, m // tm),
        in_specs=[pl.BlockSpec((tm, k), lambda j, i: (i, 0)),
                  pl.BlockSpec((k, tn), lambda j, i: (0, j))],
        out_specs=pl.BlockSpec((tm, tn), lambda j, i: (i, j)),
        out_shape=jax.ShapeDtypeStruct((m, n), out_dtype),
        compiler_params=_cparams("arbitrary", "arbitrary"),
        name=name,
    )(h, w)


def _gdn_proj_kernel(h_ref, w_ref, cw_ref, o_ref, w16_ref, carry_ref, *, tiles_per_seq, tiles_per_part, sub):
    n = pl.program_id(0)
    m = pl.program_id(1)
    tn = w_ref.shape[1]

    @pl.when(m == 0)
    def _():
        w16_ref[...] = w_ref[...].astype(BF16)

    @pl.when(m % tiles_per_seq == 0)
    def _():
        carry_ref[...] = jnp.zeros_like(carry_ref)

    part = n // tiles_per_part
    q_scale = jnp.where(part == 0, HEAD_DIM ** -0.5, 1.0).astype(F32)
    prev = carry_ref[...]
    for r in range(h_ref.shape[0] // sub):
        acc = _dot(h_ref[r * sub:(r + 1) * sub, :], w16_ref[...])
        y = _silu(_causal_conv(acc, prev, cw_ref[...], GDN_CONV))
        prev = acc[sub - V7X_SUBLANES:sub, :]
        heads = []
        for hd in range(tn // HEAD_DIM):
            blk = y[:, hd * HEAD_DIM:(hd + 1) * HEAD_DIM]
            inv = lax.rsqrt(jnp.sum(blk * blk, axis=-1, keepdims=True) + L2_EPS) * q_scale
            heads.append(blk * jnp.where(part < 2, inv, 1.0))
        o_ref[r * sub:(r + 1) * sub, :] = jnp.concatenate(heads, axis=1)
    carry_ref[...] = prev


def _gdn_proj(h, w_in, conv_w, seq, tm=1024, tn=1024, sub=128):
    m, k = h.shape
    n = 3 * GDN_WIDTH
    return pl.pallas_call(
        functools.partial(_gdn_proj_kernel, tiles_per_seq=seq // tm, tiles_per_part=GDN_WIDTH // tn, sub=sub),
        grid=(n // tn, m // tm),
        in_specs=[pl.BlockSpec((tm, k), lambda j, i: (i, 0)),
                  pl.BlockSpec((k, tn), lambda j, i: (0, j)),
                  pl.BlockSpec((GDN_CONV, tn), lambda j, i: (0, j))],
        out_specs=pl.BlockSpec((tm, tn), lambda j, i: (i, j)),
        out_shape=jax.ShapeDtypeStruct((m, n), F32),
        scratch_shapes=[pltpu.VMEM((k, tn), BF16),
                        pltpu.VMEM((V7X_SUBLANES, tn), F32)],
        compiler_params=_cparams("arbitrary", "arbitrary"),
        name="gdn_qkv_proj",
    )(h, w_in, conv_w)


def _moba_proj_kernel(h_ref, w_ref, pos_ref, freq_ref, o_ref):
    part = pl.program_id(0)
    scale = jnp.where(part == 0, MOBA_Q_SCALE, 1.0).astype(F32)
    acc = (_dot(h_ref[...], w_ref[...]) * scale).T
    ang = freq_ref[...] * pos_ref[...].astype(F32)
    roped = part < 2
    cos = jnp.where(roped, jnp.cos(ang), 1.0)
    sin = jnp.where(roped, jnp.sin(ang), 0.0)
    rows = []
    for hd in range(acc.shape[0] // HEAD_DIM):
        base = hd * HEAD_DIM
        x1 = acc[base:base + ROPE_HALF]
        x2 = acc[base + ROPE_HALF:base + ROPE_DIM]
        rows.append(x1 * cos - x2 * sin)
        rows.append(x2 * cos + x1 * sin)
        rows.append(acc[base + ROPE_DIM:base + HEAD_DIM])
    o_ref[...] = jnp.concatenate(rows, axis=0).astype(o_ref.dtype)


def _moba_proj(h, w, positions, inv_freq, batch, seq, tt=512):
    m, k = h.shape
    tr = MOBA_WIDTH
    per_seq = seq // tt
    return pl.pallas_call(
        _moba_proj_kernel,
        grid=(w.shape[1] // tr, m // tt),
        in_specs=[pl.BlockSpec((tt, k), lambda r, i: (i, 0)),
                  pl.BlockSpec((k, tr), lambda r, i: (0, r)),
                  pl.BlockSpec((None, 1, tt), lambda r, i: (i // per_seq, 0, i % per_seq)),
                  pl.BlockSpec((ROPE_HALF, 1), lambda r, i: (0, 0))],
        out_specs=pl.BlockSpec((None, tr, tt), lambda r, i: (i // per_seq, r, i % per_seq)),
        out_shape=jax.ShapeDtypeStruct((batch, w.shape[1], seq), BF16),
        compiler_params=_cparams("arbitrary", "arbitrary"),
        name="moba_qkv_proj",
    )(h, w, positions.reshape(batch, 1, seq), inv_freq)


def _gdn_kernel(q_ref, k_ref, v_ref, z_ref, sm_ref, a_ref, dt_ref, nw_ref, o_ref,
                state_ref, lbd_ref, subd_ref, *, groups, heads_per_step):
    t = pl.program_id(2)
    g = GDN_GROUP
    c = GDN_CHUNK

    @pl.when(t == 0)
    def _():
        state_ref[...] = jnp.zeros_like(state_ref)

    ri = lax.broadcasted_iota(jnp.int32, (g, g), 0)
    ci = lax.broadcasted_iota(jnp.int32, (g, g), 1)
    shift = int(math.log2(c))
    same = jnp.right_shift(ri, shift) == jnp.right_shift(ci, shift)
    lbd_ref[...] = jnp.where(same & (ci <= ri), 1.0, 0.0)
    subd_ref[...] = jnp.where(same & (ri > ci), 1.0, 0.0)
    lane = lax.broadcasted_iota(jnp.int32, (g, V7X_LANES), 1)
    eye = jnp.where(ri == ci, 1.0, 0.0)

    def group_body(gi, carry):
        hs = range(heads_per_step)
        nch = g // c
        r0 = pl.multiple_of(gi * g, g)
        sm = sm_ref[pl.ds(r0, g), :]
        beta_all = _sigmoid(sm)
        xs = sm + dt_ref[...]
        softplus = jnp.maximum(xs, 0.0) + jnp.log1p(jnp.exp(-jnp.abs(xs)))
        g_all = -jnp.exp(a_ref[...]) * softplus
        lbd = lbd_ref[...]
        lbd16 = lbd.astype(BF16)
        strict = subd_ref[...]
        cols = [slice(hh * HEAD_DIM, (hh + 1) * HEAD_DIM) for hh in hs]
        heads = [pl.program_id(1) * heads_per_step + hh for hh in hs]

        k = [k_ref[pl.ds(r0, g), cols[hh]] for hh in hs]
        k16 = [k[hh].astype(BF16) for hh in hs]
        beta_b = [jnp.broadcast_to(jnp.sum(jnp.where(lane == heads[hh], beta_all, 0.0), axis=-1, keepdims=True),
                                   (g, HEAD_DIM)) for hh in hs]
        g_b = [jnp.broadcast_to(jnp.sum(jnp.where(lane == GDN_HEADS + heads[hh], g_all, 0.0), axis=-1, keepdims=True),
                                (g, HEAD_DIM)) for hh in hs]
        kb = [k[hh] * beta_b[hh] for hh in hs]

        cs = [_dot(lbd16, jnp.concatenate(_split3(g_b[hh]), axis=1)) for hh in hs]
        kq = [_dot_nt(jnp.concatenate([kb[hh].astype(BF16), q_ref[pl.ds(r0, g), cols[hh]].astype(BF16)], axis=0), k16[hh])
              for hh in hs]
        kk = [kq[hh][:g] for hh in hs]
        qk_raw = [kq[hh][g:] for hh in hs]
        gc_b = [cs[hh][:, :HEAD_DIM] + cs[hh][:, HEAD_DIM:2 * HEAD_DIM] + cs[hh][:, 2 * HEAD_DIM:] for hh in hs]
        gamma = []
        for hh in hs:
            gc_row = gc_b[hh].T[0:1, :]
            dmat = jnp.where(lbd > 0.0, jnp.concatenate([gc_b[hh], gc_b[hh]], axis=1) - gc_row, 0.0)
            gamma.append(jnp.exp(dmat))
        eg = [jnp.exp(gc_b[hh]) for hh in hs]

        p = [-jnp.where(strict > 0.0, kk[hh] * gamma[hh], 0.0) for hh in hs]
        t_mat = [eye + p[hh] for hh in hs]
        for _ in range(5):
            p16 = [p[hh].astype(BF16) for hh in hs]
            p = [_dot(p16[hh], p16[hh]) for hh in hs]
            tp = [_dot(t_mat[hh].astype(BF16), p[hh].astype(BF16)) for hh in hs]
            t_mat = [t_mat[hh] + tp[hh] for hh in hs]
        uw = [_dot(t_mat[hh].astype(BF16),
                   jnp.concatenate([v_ref[pl.ds(r0, g), cols[hh]] * beta_b[hh], kb[hh] * eg[hh]], axis=1).astype(BF16))
              for hh in hs]
        u = [uw[hh][:, :HEAD_DIM] for hh in hs]
        w16 = [uw[hh][:, HEAD_DIM:].astype(BF16) for hh in hs]
        qk16 = [jnp.where(lbd > 0.0, qk_raw[hh] * gamma[hh], 0.0).astype(BF16) for hh in hs]
        qg = [(q_ref[pl.ds(r0, g), cols[hh]] * eg[hh]).astype(BF16) for hh in hs]
        kd_t = []
        for hh in hs:
            gc_last = jnp.concatenate(
                [jnp.broadcast_to(gc_b[hh][(ch + 1) * c - 1:(ch + 1) * c, :], (c, HEAD_DIM)) for ch in range(nch)],
                axis=0)
            kd_t.append((k[hh] * jnp.exp(gc_last - gc_b[hh])).T.astype(BF16))

        state = [state_ref[hh] for hh in hs]
        outs = [[] for _ in hs]
        zeros_c = jnp.zeros((c, HEAD_DIM), F32)
        for ch in range(nch):
            rows = slice(ch * c, (ch + 1) * c)
            s16 = [state[hh].astype(BF16) for hh in hs]
            wq = [_dot(jnp.concatenate([w16[hh][rows], qg[hh][rows]], axis=0), s16[hh]) for hh in hs]
            ws = [wq[hh][:c] for hh in hs]
            qs = [wq[hh][c:] for hh in hs]
            vn_all = [jnp.concatenate([zeros_c] * ch + [u[hh][rows] - ws[hh]] + [zeros_c] * (nch - 1 - ch),
                                      axis=0).astype(BF16) for hh in hs]
            so = [_dot(jnp.concatenate([kd_t[hh], qk16[hh][rows]], axis=0), vn_all[hh]) for hh in hs]
            sv = [so[hh][:HEAD_DIM] for hh in hs]
            ov = [so[hh][HEAD_DIM:] for hh in hs]
            for hh in hs:
                outs[hh].append(qs[hh] + ov[hh])
                state[hh] = state[hh] * eg[hh][(ch + 1) * c - 1:(ch + 1) * c, :] + sv[hh]
        for hh in hs:
            state_ref[hh] = state[hh]
            o = jnp.concatenate(outs[hh], axis=0)
            o = o * lax.rsqrt(jnp.mean(o * o, axis=-1, keepdims=True) + NORM_EPS) * nw_ref[...]
            o_ref[pl.ds(r0, g), cols[hh]] = (o * _silu(z_ref[pl.ds(r0, g), cols[hh]])).astype(o_ref.dtype)
        return carry

    lax.fori_loop(0, groups, group_body, 0)


def _gdn(qkv, zs, a_pad, dt_pad, norm_w, batch, seq, ts=512, heads_per_step=8):
    m = qkv.shape[0]
    tiles = seq // ts
    hp = heads_per_step
    hb = GDN_HEADS // hp
    wd = hp * HEAD_DIM
    row = lambda b, h, t: b * tiles + t
    return pl.pallas_call(
        functools.partial(_gdn_kernel, groups=ts // GDN_GROUP, heads_per_step=hp),
        grid=(batch, hb, tiles),
        in_specs=[pl.BlockSpec((ts, wd), lambda b, h, t: (row(b, h, t), h)),
                  pl.BlockSpec((ts, wd), lambda b, h, t: (row(b, h, t), hb + h)),
                  pl.BlockSpec((ts, wd), lambda b, h, t: (row(b, h, t), 2 * hb + h)),
                  pl.BlockSpec((ts, wd), lambda b, h, t: (row(b, h, t), h)),
                  pl.BlockSpec((ts, V7X_LANES), lambda b, h, t: (row(b, h, t), GDN_WIDTH // V7X_LANES)),
                  pl.BlockSpec((1, V7X_LANES), lambda b, h, t: (0, 0)),
                  pl.BlockSpec((1, V7X_LANES), lambda b, h, t: (0, 0)),
                  pl.BlockSpec((1, HEAD_DIM), lambda b, h, t: (0, 0))],
        out_specs=pl.BlockSpec((ts, wd), lambda b, h, t: (row(b, h, t), h)),
        out_shape=jax.ShapeDtypeStruct((m, GDN_WIDTH), BF16),
        scratch_shapes=[pltpu.VMEM((hp, HEAD_DIM, HEAD_DIM), F32),
                        pltpu.VMEM((GDN_GROUP, GDN_GROUP), F32),
                        pltpu.VMEM((GDN_GROUP, GDN_GROUP), F32)],
        compiler_params=_cparams("arbitrary", "arbitrary", "arbitrary"),
        name="gated_delta_rule",
    )(qkv, qkv, qkv, zs, zs, a_pad, dt_pad, norm_w)


def _moba_kernel(q_ref, k_ref, v_ref, o_ref, kn_ref, km_ref, s_ref, p_ref, bias_ref, *, nblk):
    blk = MOBA_BLOCK
    for j in range(nblk):
        kt = k_ref[:, j * blk:(j + 1) * blk].astype(F32)
        kn = kt.T
        kn_ref[j * blk:(j + 1) * blk, :] = kn.astype(BF16)
        km_ref[j:j + 1, :] = jnp.sum(kn, axis=0, keepdims=True) * (1.0 / blk)

    key_i = lax.broadcasted_iota(jnp.int32, (blk, blk), 0)
    qry_i = lax.broadcasted_iota(jnp.int32, (blk, blk), 1)
    causal_bias = jnp.where(key_i <= qry_i, 0.0, NEG_INF)
    groups = blk // V7X_SUBLANES

    def scores(i):
        s_buf = s_ref.at[i % 2]
        qt = q_ref[:, i * blk:(i + 1) * blk]
        if i > MOBA_TOPK:
            gate = _dot_hi(km_ref[...], qt.astype(F32))
            n_iota = lax.broadcasted_iota(jnp.int32, (nblk, blk), 0)
            valid = n_iota < i
            for n in range(i):
                g_n = gate[n:n + 1, :]
                beats = valid & ((gate > g_n) | ((gate == g_n) & (n_iota < n)))
                cnt = jnp.sum(jnp.where(beats, 1.0, 0.0), axis=0, keepdims=True)
                bias_ref[n:n + 1, :] = jnp.where(cnt < float(MOBA_TOPK), 0.0, NEG_INF)
        m_acc = None
        for j in range(i + 1):
            s = _dot(kn_ref[j * blk:(j + 1) * blk, :], qt)
            if j == i:
                s = s + causal_bias
            elif i > MOBA_TOPK:
                s = s + bias_ref[j:j + 1, :]
            s_buf[j * blk:(j + 1) * blk, :] = s
            part = jnp.max(s.reshape(groups, V7X_SUBLANES, blk), axis=0)
            m_acc = part if m_acc is None else jnp.maximum(m_acc, part)
        return jnp.max(m_acc, axis=0, keepdims=True)

    def finish(i, m_row):
        s_buf = s_ref.at[i % 2]
        l_acc = None
        for j in range(i + 1):
            p = jnp.exp2(s_buf[j * blk:(j + 1) * blk, :] - m_row)
            part = jnp.sum(p.reshape(groups, V7X_SUBLANES, blk), axis=0)
            l_acc = part if l_acc is None else l_acc + part
            p_ref[j * blk:(j + 1) * blk, :] = p.astype(BF16)
        l_row = jnp.sum(l_acc, axis=0, keepdims=True)
        o_t = _dot(v_ref[:, 0:(i + 1) * blk], p_ref[0:(i + 1) * blk, :]) / l_row
        o_ref[i * blk:(i + 1) * blk, :] = o_t.T.astype(o_ref.dtype)

    m_next = scores(0)
    for i in range(nblk):
        m_row = m_next
        if i + 1 < nblk:
            m_next = scores(i + 1)
        finish(i, m_row)


def _moba(qkv_t, batch, seq):
    nblk = seq // MOBA_BLOCK
    hb = MOBA_HEADS
    return pl.pallas_call(
        functools.partial(_moba_kernel, nblk=nblk),
        grid=(batch, MOBA_HEADS),
        in_specs=[pl.BlockSpec((None, HEAD_DIM, seq), lambda b, h: (b, h, 0)),
                  pl.BlockSpec((None, HEAD_DIM, seq), lambda b, h: (b, hb + h, 0)),
                  pl.BlockSpec((None, HEAD_DIM, seq), lambda b, h: (b, 2 * hb + h, 0))],
        out_specs=pl.BlockSpec((seq, HEAD_DIM), lambda b, h: (b, h)),
        out_shape=jax.ShapeDtypeStruct((batch * seq, MOBA_WIDTH), BF16),
        scratch_shapes=[pltpu.VMEM((seq, HEAD_DIM), BF16),
                        pltpu.VMEM((nblk, HEAD_DIM), F32),
                        pltpu.VMEM((2, seq, MOBA_BLOCK), F32),
                        pltpu.VMEM((seq, MOBA_BLOCK), BF16),
                        pltpu.VMEM((nblk, MOBA_BLOCK), F32)],
        compiler_params=_cparams("arbitrary", "arbitrary"),
        name="moba_attention",
    )(qkv_t, qkv_t, qkv_t)


def _mix_kernel(a_ref, b_ref, ga_ref, gb_ref, wa_ref, wb_ref, wo_ref, x_ref, nw_ref, x1_ref, h_ref, *, sub):
    for r in range(a_ref.shape[0] // sub):
        rows = slice(r * sub, (r + 1) * sub)
        ya = _dot(a_ref[rows, :], wa_ref[...])
        yb = _dot(b_ref[rows, :], wb_ref[...])
        merged = (ga_ref[rows, :] * ya + gb_ref[rows, :] * yb).astype(BF16)
        x1 = x_ref[rows, :] + _dot(merged, wo_ref[...])
        x1_ref[rows, :] = x1
        h = x1 * lax.rsqrt(jnp.mean(x1 * x1, axis=-1, keepdims=True) + NORM_EPS) * nw_ref[...]
        h_ref[rows, :] = h.astype(h_ref.dtype)


def _mix(ya_in, yb_in, gates, wa, wb, wo, x2d, nw, tm=512, sub=256):
    m, k = ya_in.shape
    n = wo.shape[1]
    return pl.pallas_call(
        functools.partial(_mix_kernel, sub=sub),
        grid=(m // tm,),
        in_specs=[pl.BlockSpec((tm, k), lambda i: (i, 0)),
                  pl.BlockSpec((tm, k), lambda i: (i, 0)),
                  pl.BlockSpec((tm, n), lambda i: (i, 0)),
                  pl.BlockSpec((tm, n), lambda i: (i, 1)),
                  pl.BlockSpec((k, n), lambda i: (0, 0)),
                  pl.BlockSpec((k, n), lambda i: (0, 0)),
                  pl.BlockSpec((n, n), lambda i: (0, 0)),
                  pl.BlockSpec((tm, n), lambda i: (i, 0)),
                  pl.BlockSpec((1, n), lambda i: (0, 0))],
        out_specs=[pl.BlockSpec((tm, n), lambda i: (i, 0)),
                   pl.BlockSpec((tm, n), lambda i: (i, 0))],
        out_shape=[jax.ShapeDtypeStruct((m, n), F32),
                   jax.ShapeDtypeStruct((m, n), BF16)],
        compiler_params=_cparams("arbitrary"),
        name="branch_mix_out_proj_norm2",
    )(ya_in, yb_in, gates, gates, wa, wb, wo, x2d, nw.reshape(1, n))


def _upproj_kernel(h_ref, wg_ref, wv_ref, cg_ref, cv_ref, bg_ref, bv_ref, o_ref,
                   w_ref, carry_ref, *, tiles_per_seq, sub):
    m = pl.program_id(1)
    tn = wg_ref.shape[1]

    @pl.when(m == 0)
    def _():
        w_ref[:, :tn] = wg_ref[...].astype(BF16)
        w_ref[:, tn:] = wv_ref[...].astype(BF16)

    @pl.when(m % tiles_per_seq == 0)
    def _():
        carry_ref[...] = jnp.zeros_like(carry_ref)

    cw = jnp.concatenate([cg_ref[...], cv_ref[...]], axis=1)
    bias = jnp.concatenate([bg_ref[...], bv_ref[...]], axis=1)
    prev = carry_ref[...]
    for r in range(h_ref.shape[0] // sub):
        u = _dot(h_ref[r * sub:(r + 1) * sub, :], w_ref[...])
        y = _causal_conv(u, prev, cw, FFN_CONV) + bias
        prev = u[sub - V7X_SUBLANES:sub, :]
        o_ref[r * sub:(r + 1) * sub, :] = (_silu(y[:, :tn]) * y[:, tn:]).astype(o_ref.dtype)
    carry_ref[...] = prev


def _upproj(h, w_up, conv_w, conv_b, seq, tm=1024, tn=512, sub=256):
    m, k = h.shape
    nb = D_FF // tn
    return pl.pallas_call(
        functools.partial(_upproj_kernel, tiles_per_seq=seq // tm, sub=sub),
        grid=(nb, m // tm),
        in_specs=[pl.BlockSpec((tm, k), lambda j, i: (i, 0)),
                  pl.BlockSpec((k, tn), lambda j, i: (0, j)),
                  pl.BlockSpec((k, tn), lambda j, i: (0, nb + j)),
                  pl.BlockSpec((FFN_CONV, tn), lambda j, i: (0, j)),
                  pl.BlockSpec((FFN_CONV, tn), lambda j, i: (0, nb + j)),
                  pl.BlockSpec((1, tn), lambda j, i: (0, j)),
                  pl.BlockSpec((1, tn), lambda j, i: (0, nb + j))],
        out_specs=pl.BlockSpec((tm, tn), lambda j, i: (i, j)),
        out_shape=jax.ShapeDtypeStruct((m, D_FF), BF16),
        scratch_shapes=[pltpu.VMEM((k, 2 * tn), BF16),
                        pltpu.VMEM((V7X_SUBLANES, 2 * tn), F32)],
        compiler_params=_cparams("arbitrary", "arbitrary"),
        name="up_proj_conv_gate",
    )(h, w_up, w_up, conv_w, conv_w, conv_b, conv_b)


def _downproj_kernel(a_ref, w_ref, x_ref, o_ref):
    o_ref[...] = x_ref[...] + _dot(a_ref[...], w_ref[...])


def _downproj(a, w, x1, tm=512, tn=1024):
    m, k = a.shape
    n = w.shape[1]
    return pl.pallas_call(
        _downproj_kernel,
        grid=(n // tn, m // tm),
        in_specs=[pl.BlockSpec((tm, k), lambda j, i: (i, 0)),
                  pl.BlockSpec((k, tn), lambda j, i: (0, j)),
                  pl.BlockSpec((tm, tn), lambda j, i: (i, j))],
        out_specs=pl.BlockSpec((tm, tn), lambda j, i: (i, j)),
        out_shape=jax.ShapeDtypeStruct((m, n), F32),
        compiler_params=_cparams("arbitrary", "arbitrary"),
        name="down_proj",
    )(a, w, x1)


def _downproj_norm_kernel(a_ref, w_ref, x_ref, nw_ref, o_ref):
    x2 = x_ref[...] + _dot(a_ref[...], w_ref[...])
    o_ref[...] = x2 * lax.rsqrt(jnp.mean(x2 * x2, axis=-1, keepdims=True) + NORM_EPS) * nw_ref[...]


def _downproj_norm(a, w, x1, nw, tm=512):
    m, k = a.shape
    n = w.shape[1]
    return pl.pallas_call(
        _downproj_norm_kernel,
        grid=(m // tm,),
        in_specs=[pl.BlockSpec((tm, k), lambda i: (i, 0)),
                  pl.BlockSpec((k, n), lambda i: (0, 0)),
                  pl.BlockSpec((tm, n), lambda i: (i, 0)),
                  pl.BlockSpec((1, n), lambda i: (0, 0))],
        out_specs=pl.BlockSpec((tm, n), lambda i: (i, 0)),
        out_shape=jax.ShapeDtypeStruct((m, n), F32),
        compiler_params=_cparams("arbitrary"),
        name="down_proj_final_norm",
    )(a, w, x1, nw.reshape(1, n))


def _pad_lanes(vec, offset):
    return jnp.zeros((1, V7X_LANES), F32).at[0, offset:offset + vec.shape[0]].set(vec.astype(F32))


def kernel(x, positions, ln1, w_in, gdn_conv, gdn_a_log, gdn_dt_bias, gdn_norm, w_branch_a, w_branch_b,
           w_out, ln2, w_up, ffn_conv, ffn_conv_bias, w_down, final_norm):
    batch, seq, d = x.shape
    m = batch * seq
    depth = ln1.shape[0]
    o_qkv_a = 3 * GDN_WIDTH
    o_small = o_qkv_a + 2 * GDN_HEADS
    o_z = o_small + GDN_WIDTH
    o_qkv_b = o_z + 3 * MOBA_WIDTH
    inv_freq = (ROPE_THETA ** (-jnp.arange(ROPE_HALF, dtype=F32) / ROPE_HALF)).reshape(ROPE_HALF, 1)

    x2d = x.reshape(m, d)
    for l in range(depth):
        w = w_in[l]
        w_zs = jnp.concatenate(
            [w[:, o_small:o_z], jnp.pad(w[:, o_qkv_a:o_small], ((0, 0), (0, V7X_LANES - 2 * GDN_HEADS)))],
            axis=1).astype(BF16)
        w_qkv_b = w[:, o_z:o_qkv_b].astype(BF16)
        w_gates = w[:, o_qkv_b:].astype(BF16)

        h1 = _norm_cast(x2d, ln1[l])
        qkv_a = _gdn_proj(h1, w, gdn_conv[l], seq)
        zs = _proj(h1, w_zs, act=None, out_dtype=F32, tm=1024, tn=GDN_WIDTH + V7X_LANES, name="z_beta_decay_proj")
        gates = _proj(h1, w_gates, act="sigmoid", out_dtype=BF16, tm=1024, tn=2048, name="gate_proj")
        qkv_b_t = _moba_proj(h1, w_qkv_b, positions, inv_freq, batch, seq)

        gdn_out = _gdn(qkv_a, zs, _pad_lanes(gdn_a_log[l], GDN_HEADS), _pad_lanes(gdn_dt_bias[l], GDN_HEADS),
                       gdn_norm[l].reshape(1, HEAD_DIM), batch, seq)
        attn = _moba(qkv_b_t, batch, seq)
        x1, h2 = _mix(gdn_out, attn, gates, w_branch_a[l].astype(BF16), w_branch_b[l].astype(BF16),
                      w_out[l].astype(BF16), x2d, ln2[l])
        act = _upproj(h2, w_up[l], ffn_conv[l], ffn_conv_bias[l].reshape(1, 2 * D_FF), seq)
        if l == depth - 1:
            return _downproj_norm(act, w_down[l].astype(BF16), x1, final_norm).reshape(batch, seq, d)
        x2d = _downproj(act, w_down[l].astype(BF16), x1)
    return _final_norm(x2d, final_norm).reshape(batch, seq, d)
```

```python
import functools
import math

import jax
import jax.numpy as jnp
import numpy as np
from jax import lax
from jax.experimental import pallas as pl
from jax.experimental.pallas import tpu as pltpu

F32 = jnp.float32
BF16 = jnp.bfloat16

D_MODEL = 2048
GDN_HEADS = 8
HEAD_DIM = 128
GDN_WIDTH = GDN_HEADS * HEAD_DIM
GDN_CONV = 4
GDN_CHUNK = 64
MOBA_HEADS = 8
MOBA_WIDTH = MOBA_HEADS * HEAD_DIM
MOBA_BLOCK = 256
MOBA_TOPK = 3
ROPE_THETA = 500000.0
ROPE_DIM = HEAD_DIM // 4
ROPE_HALF = ROPE_DIM // 2
D_FF = 5632
FFN_CONV = 3
NORM_EPS = 1e-6
L2_EPS = 1e-6
NEG_INF = -1e30
NEG_LOG2_E = -math.log2(math.e)

V7X_LANES = 128
V7X_SUBLANES = 8
V7X_VMEM_LIMIT_BYTES = 56 * 1024 * 1024

GDN_GROUP = 4 * GDN_CHUNK
MOBA_Q_SCALE = HEAD_DIM ** -0.5 * math.log2(math.e)


def _cparams(*sem):
    return pltpu.CompilerParams(dimension_semantics=sem, vmem_limit_bytes=V7X_VMEM_LIMIT_BYTES)


def _dot(a, b):
    return jnp.dot(a, b, preferred_element_type=F32)


def _dot_nt(a, b):
    return lax.dot_general(a, b, (((1,), (1,)), ((), ())), preferred_element_type=F32)


def _dot_hi(a, b):
    return jnp.dot(a, b, preferred_element_type=F32, precision=lax.Precision.HIGHEST)


def _split3(x):
    hi = x.astype(BF16)
    r1 = x - hi.astype(F32)
    mid = r1.astype(BF16)
    lo = (r1 - mid.astype(F32)).astype(BF16)
    return hi, mid, lo


def _sigmoid(x):
    return 1.0 / (1.0 + jnp.exp2(x * NEG_LOG2_E))


def _silu(x):
    return x * _sigmoid(x)


def _shift_rows(cur, prev8, s):
    if s == 0:
        return cur
    rolled = pltpu.roll(cur, s, axis=0)
    rolled_prev = pltpu.roll(prev8, s, axis=0)
    row = lax.broadcasted_iota(jnp.int32, prev8.shape, 0)
    first = jnp.where(row < s, rolled_prev, rolled[0:V7X_SUBLANES])
    return jnp.concatenate([first, rolled[V7X_SUBLANES:]], axis=0)


def _causal_conv(cur, prev8, cw, width):
    y = cw[width - 1:width, :] * cur
    for j in range(width - 1):
        y = y + cw[j:j + 1, :] * _shift_rows(cur, prev8, width - 1 - j)
    return y


def _norm_kernel(x_ref, w_ref, h_ref):
    x = x_ref[...]
    y = x * lax.rsqrt(jnp.mean(x * x, axis=-1, keepdims=True) + NORM_EPS) * w_ref[...]
    h_ref[...] = y.astype(h_ref.dtype)


def _norm_cast(x2d, w, tm=512):
    m, d = x2d.shape
    return pl.pallas_call(
        _norm_kernel,
        grid=(m // tm,),
        in_specs=[pl.BlockSpec((tm, d), lambda i: (i, 0)),
                  pl.BlockSpec((1, d), lambda i: (0, 0))],
        out_specs=pl.BlockSpec((tm, d), lambda i: (i, 0)),
        out_shape=jax.ShapeDtypeStruct((m, d), BF16),
        compiler_params=_cparams("arbitrary"),
        name="norm1",
    )(x2d, w.reshape(1, d))


def _final_norm_kernel(x_ref, w_ref, o_ref):
    x = x_ref[...]
    o_ref[...] = x * lax.rsqrt(jnp.mean(x * x, axis=-1, keepdims=True) + NORM_EPS) * w_ref[...]


def _final_norm(x2d, w, tm=512):
    m, d = x2d.shape
    return pl.pallas_call(
        _final_norm_kernel,
        grid=(m // tm,),
        in_specs=[pl.BlockSpec((tm, d), lambda i: (i, 0)),
                  pl.BlockSpec((1, d), lambda i: (0, 0))],
        out_specs=pl.BlockSpec((tm, d), lambda i: (i, 0)),
        out_shape=jax.ShapeDtypeStruct((m, d), F32),
        compiler_params=_cparams("arbitrary"),
        name="final_norm",
    )(x2d, w.reshape(1, d))


def _proj_t_kernel(h_ref, wt_ref, o_ref, w16_ref, *, act):
    @pl.when(pl.program_id(1) == 0)
    def _():
        w16_ref[...] = wt_ref[...].T.astype(BF16)

    acc = _dot(h_ref[...], w16_ref[...])
    if act == "sigmoid":
        acc = _sigmoid(acc)
    o_ref[...] = acc.astype(o_ref.dtype)


def _proj_t(h, wt, row0, n, *, act, out_dtype, tm, tn, name):
    m, k = h.shape
    return pl.pallas_call(
        functools.partial(_proj_t_kernel, act=act),
        grid=(n // tn, m // tm),
        in_specs=[pl.BlockSpec((tm, k), lambda j, i: (i, 0)),
                  pl.BlockSpec((pl.Element(tn), pl.Element(k)), lambda j, i: (pl.multiple_of(row0 + j * tn, 16), 0))],
        out_specs=pl.BlockSpec((tm, tn), lambda j, i: (i, j)),
        out_shape=jax.ShapeDtypeStruct((m, n), out_dtype),
        scratch_shapes=[pltpu.VMEM((k, tn), BF16)],
        compiler_params=_cparams("arbitrary", "arbitrary"),
        name=name,
    )(h, wt)


def _gdn_proj_kernel(h_ref, w_ref, cw_ref, o_ref, w16_ref, carry_ref, *, tiles_per_seq, tiles_per_part, sub):
    n = pl.program_id(0)
    m = pl.program_id(1)
    tn = w_ref.shape[0]

    @pl.when(m == 0)
    def _():
        w16_ref[...] = w_ref[...].T.astype(BF16)

    @pl.when(m % tiles_per_seq == 0)
    def _():
        carry_ref[...] = jnp.zeros_like(carry_ref)

    part = n // tiles_per_part
    q_scale = jnp.where(part == 0, HEAD_DIM ** -0.5, 1.0).astype(F32)
    prev = carry_ref[...]
    for r in range(h_ref.shape[0] // sub):
        acc = _dot(h_ref[r * sub:(r + 1) * sub, :], w16_ref[...])
        y = _silu(_causal_conv(acc, prev, cw_ref[...], GDN_CONV))
        prev = acc[sub - V7X_SUBLANES:sub, :]
        heads = []
        for hd in range(tn // HEAD_DIM):
            blk = y[:, hd * HEAD_DIM:(hd + 1) * HEAD_DIM]
            inv = lax.rsqrt(jnp.sum(blk * blk, axis=-1, keepdims=True) + L2_EPS) * q_scale
            heads.append(blk * jnp.where(part < 2, inv, 1.0))
        o_ref[r * sub:(r + 1) * sub, :] = jnp.concatenate(heads, axis=1)
    carry_ref[...] = prev


def _gdn_proj(h, w_in, conv_w, seq, tm=1024, tn=1024, sub=128):
    m, k = h.shape
    n = 3 * GDN_WIDTH
    return pl.pallas_call(
        functools.partial(_gdn_proj_kernel, tiles_per_seq=seq // tm, tiles_per_part=GDN_WIDTH // tn, sub=sub),
        grid=(n // tn, m // tm),
        in_specs=[pl.BlockSpec((tm, k), lambda j, i: (i, 0)),
                  pl.BlockSpec((tn, k), lambda j, i: (j, 0)),
                  pl.BlockSpec((GDN_CONV, tn), lambda j, i: (0, j))],
        out_specs=pl.BlockSpec((tm, tn), lambda j, i: (i, j)),
        out_shape=jax.ShapeDtypeStruct((m, n), F32),
        scratch_shapes=[pltpu.VMEM((k, tn), BF16),
                        pltpu.VMEM((V7X_SUBLANES, tn), F32)],
        compiler_params=_cparams("arbitrary", "arbitrary"),
        name="gdn_qkv_proj",
    )(h, w_in, conv_w)


def _moba_proj_kernel(h_ref, wt_ref, pos_ref, freq_ref, o_ref, w16_ref):
    part = pl.program_id(0)

    @pl.when(pl.program_id(1) == 0)
    def _():
        w16_ref[...] = wt_ref[...].astype(BF16)

    scale = jnp.where(part == 0, MOBA_Q_SCALE, 1.0).astype(F32)
    acc = _dot_nt(w16_ref[...], h_ref[...]) * scale
    ang = freq_ref[...] * pos_ref[...].astype(F32)
    roped = part < 2
    cos = jnp.where(roped, jnp.cos(ang), 1.0)
    sin = jnp.where(roped, jnp.sin(ang), 0.0)
    rows = []
    for hd in range(acc.shape[0] // HEAD_DIM):
        base = hd * HEAD_DIM
        x1 = acc[base:base + ROPE_HALF]
        x2 = acc[base + ROPE_HALF:base + ROPE_DIM]
        rows.append(x1 * cos - x2 * sin)
        rows.append(x2 * cos + x1 * sin)
        rows.append(acc[base + ROPE_DIM:base + HEAD_DIM])
    o_ref[...] = jnp.concatenate(rows, axis=0).astype(o_ref.dtype)


def _moba_proj(h, wt, row0, positions, inv_freq, batch, seq, tt=512):
    m, k = h.shape
    tr = MOBA_WIDTH
    n = 3 * MOBA_WIDTH
    per_seq = seq // tt
    return pl.pallas_call(
        _moba_proj_kernel,
        grid=(n // tr, m // tt),
        in_specs=[pl.BlockSpec((tt, k), lambda r, i: (i, 0)),
                  pl.BlockSpec((pl.Element(tr), pl.Element(k)), lambda r, i: (pl.multiple_of(row0 + r * tr, 16), 0)),
                  pl.BlockSpec((None, 1, tt), lambda r, i: (i // per_seq, 0, i % per_seq)),
                  pl.BlockSpec((ROPE_HALF, 1), lambda r, i: (0, 0))],
        out_specs=pl.BlockSpec((None, tr, tt), lambda r, i: (i // per_seq, r, i % per_seq)),
        out_shape=jax.ShapeDtypeStruct((batch, n, seq), BF16),
        scratch_shapes=[pltpu.VMEM((tr, k), BF16)],
        compiler_params=_cparams("arbitrary", "arbitrary"),
        name="moba_qkv_proj",
    )(h, wt, positions.reshape(batch, 1, seq), inv_freq)


def _gdn_kernel(q_ref, k_ref, v_ref, z_ref, sm_ref, a_ref, dt_ref, nw_ref, o_ref,
                state_ref, lbd_ref, subd_ref, *, groups, heads_per_step):
    t = pl.program_id(2)
    g = GDN_GROUP
    c = GDN_CHUNK

    @pl.when(t == 0)
    def _():
        state_ref[...] = jnp.zeros_like(state_ref)

    ri = lax.broadcasted_iota(jnp.int32, (g, g), 0)
    ci = lax.broadcasted_iota(jnp.int32, (g, g), 1)
    shift = int(math.log2(c))
    same = jnp.right_shift(ri, shift) == jnp.right_shift(ci, shift)
    lbd_ref[...] = jnp.where(same & (ci <= ri), 1.0, 0.0)
    subd_ref[...] = jnp.where(same & (ri > ci), 1.0, 0.0)
    lane = lax.broadcasted_iota(jnp.int32, (g, V7X_LANES), 1)
    eye = jnp.where(ri == ci, 1.0, 0.0)

    def group_body(gi, carry):
        hs = range(heads_per_step)
        nch = g // c
        r0 = pl.multiple_of(gi * g, g)
        sm = sm_ref[pl.ds(r0, g), :]
        beta_all = _sigmoid(sm)
        xs = sm + dt_ref[...]
        softplus = jnp.maximum(xs, 0.0) + jnp.log1p(jnp.exp(-jnp.abs(xs)))
        g_all = -jnp.exp(a_ref[...]) * softplus
        lbd = lbd_ref[...]
        lbd16 = lbd.astype(BF16)
        strict = subd_ref[...]
        cols = [slice(hh * HEAD_DIM, (hh + 1) * HEAD_DIM) for hh in hs]
        heads = [pl.program_id(1) * heads_per_step + hh for hh in hs]

        k = [k_ref[pl.ds(r0, g), cols[hh]] for hh in hs]
        k16 = [k[hh].astype(BF16) for hh in hs]
        beta_b = [jnp.broadcast_to(jnp.sum(jnp.where(lane == heads[hh], beta_all, 0.0), axis=-1, keepdims=True),
                                   (g, HEAD_DIM)) for hh in hs]
        g_b = [jnp.broadcast_to(jnp.sum(jnp.where(lane == GDN_HEADS + heads[hh], g_all, 0.0), axis=-1, keepdims=True),
                                (g, HEAD_DIM)) for hh in hs]
        kb = [k[hh] * beta_b[hh] for hh in hs]

        cs = [_dot(lbd16, jnp.concatenate(_split3(g_b[hh]), axis=1)) for hh in hs]
        kq = [_dot_nt(jnp.concatenate([kb[hh].astype(BF16), q_ref[pl.ds(r0, g), cols[hh]].astype(BF16)], axis=0), k16[hh])
              for hh in hs]
        kk = [kq[hh][:g] for hh in hs]
        qk_raw = [kq[hh][g:] for hh in hs]
        gc_b = [cs[hh][:, :HEAD_DIM] + cs[hh][:, HEAD_DIM:2 * HEAD_DIM] + cs[hh][:, 2 * HEAD_DIM:] for hh in hs]
        gamma = []
        for hh in hs:
            gc_row = gc_b[hh].T[0:1, :]
            dmat = jnp.where(lbd > 0.0, jnp.concatenate([gc_b[hh], gc_b[hh]], axis=1) - gc_row, 0.0)
            gamma.append(jnp.exp(dmat))
        eg = [jnp.exp(gc_b[hh]) for hh in hs]

        p = [-jnp.where(strict > 0.0, kk[hh] * gamma[hh], 0.0) for hh in hs]
        t_mat = [eye + p[hh] for hh in hs]
        for _ in range(5):
            p16 = [p[hh].astype(BF16) for hh in hs]
            p = [_dot(p16[hh], p16[hh]) for hh in hs]
            tp = [_dot(t_mat[hh].astype(BF16), p[hh].astype(BF16)) for hh in hs]
            t_mat = [t_mat[hh] + tp[hh] for hh in hs]
        uw = [_dot(t_mat[hh].astype(BF16),
                   jnp.concatenate([v_ref[pl.ds(r0, g), cols[hh]] * beta_b[hh], kb[hh] * eg[hh]], axis=1).astype(BF16))
              for hh in hs]
        u = [uw[hh][:, :HEAD_DIM] for hh in hs]
        w16 = [uw[hh][:, HEAD_DIM:].astype(BF16) for hh in hs]
        qk16 = [jnp.where(lbd > 0.0, qk_raw[hh] * gamma[hh], 0.0).astype(BF16) for hh in hs]
        qg = [(q_ref[pl.ds(r0, g), cols[hh]] * eg[hh]).astype(BF16) for hh in hs]
        kd_t = []
        for hh in hs:
            gc_last = jnp.concatenate(
                [jnp.broadcast_to(gc_b[hh][(ch + 1) * c - 1:(ch + 1) * c, :], (c, HEAD_DIM)) for ch in range(nch)],
                axis=0)
            kd_t.append((k[hh] * jnp.exp(gc_last - gc_b[hh])).T.astype(BF16))

        state = [state_ref[hh] for hh in hs]
        outs = [[] for _ in hs]
        zeros_c = jnp.zeros((c, HEAD_DIM), F32)
        for ch in range(nch):
            rows = slice(ch * c, (ch + 1) * c)
            s16 = [state[hh].astype(BF16) for hh in hs]
            wq = [_dot(jnp.concatenate([w16[hh][rows], qg[hh][rows]], axis=0), s16[hh]) for hh in hs]
            ws = [wq[hh][:c] for hh in hs]
            qs = [wq[hh][c:] for hh in hs]
            vn_all = [jnp.concatenate([zeros_c] * ch + [u[hh][rows] - ws[hh]] + [zeros_c] * (nch - 1 - ch),
                                      axis=0).astype(BF16) for hh in hs]
            so = [_dot(jnp.concatenate([kd_t[hh], qk16[hh][rows]], axis=0), vn_all[hh]) for hh in hs]
            sv = [so[hh][:HEAD_DIM] for hh in hs]
            ov = [so[hh][HEAD_DIM:] for hh in hs]
            for hh in hs:
                outs[hh].append(qs[hh] + ov[hh])
                state[hh] = state[hh] * eg[hh][(ch + 1) * c - 1:(ch + 1) * c, :] + sv[hh]
        for hh in hs:
            state_ref[hh] = state[hh]
            o = jnp.concatenate(outs[hh], axis=0)
            o = o * lax.rsqrt(jnp.mean(o * o, axis=-1, keepdims=True) + NORM_EPS) * nw_ref[...]
            o_ref[pl.ds(r0, g), cols[hh]] = (o * _silu(z_ref[pl.ds(r0, g), cols[hh]])).astype(o_ref.dtype)
        return carry

    lax.fori_loop(0, groups, group_body, 0)


def _gdn(qkv, zs, a_pad, dt_pad, norm_w, batch, seq, ts=512, heads_per_step=8):
    m = qkv.shape[0]
    tiles = seq // ts
    hp = heads_per_step
    hb = GDN_HEADS // hp
    wd = hp * HEAD_DIM
    row = lambda b, h, t: b * tiles + t
    return pl.pallas_call(
        functools.partial(_gdn_kernel, groups=ts // GDN_GROUP, heads_per_step=hp),
        grid=(batch, hb, tiles),
        in_specs=[pl.BlockSpec((ts, wd), lambda b, h, t: (row(b, h, t), h)),
                  pl.BlockSpec((ts, wd), lambda b, h, t: (row(b, h, t), hb + h)),
                  pl.BlockSpec((ts, wd), lambda b, h, t: (row(b, h, t), 2 * hb + h)),
                  pl.BlockSpec((ts, wd), lambda b, h, t: (row(b, h, t), h)),
                  pl.BlockSpec((ts, V7X_LANES), lambda b, h, t: (row(b, h, t), GDN_WIDTH // V7X_LANES)),
                  pl.BlockSpec((1, V7X_LANES), lambda b, h, t: (0, 0)),
                  pl.BlockSpec((1, V7X_LANES), lambda b, h, t: (0, 0)),
                  pl.BlockSpec((1, HEAD_DIM), lambda b, h, t: (0, 0))],
        out_specs=pl.BlockSpec((ts, wd), lambda b, h, t: (row(b, h, t), h)),
        out_shape=jax.ShapeDtypeStruct((m, GDN_WIDTH), BF16),
        scratch_shapes=[pltpu.VMEM((hp, HEAD_DIM, HEAD_DIM), F32),
                        pltpu.VMEM((GDN_GROUP, GDN_GROUP), F32),
                        pltpu.VMEM((GDN_GROUP, GDN_GROUP), F32)],
        compiler_params=_cparams("arbitrary", "arbitrary", "arbitrary"),
        name="gated_delta_rule",
    )(qkv, qkv, qkv, zs, zs, a_pad, dt_pad, norm_w)


def _moba_kernel(q_ref, k_ref, v_ref, o_ref, kn_ref, km_ref, s_ref, p_ref, bias_ref, *, nblk):
    blk = MOBA_BLOCK
    for j in range(nblk):
        kt = k_ref[:, j * blk:(j + 1) * blk].astype(F32)
        kn = kt.T
        kn_ref[j * blk:(j + 1) * blk, :] = kn.astype(BF16)
        km_ref[j:j + 1, :] = jnp.sum(kn, axis=0, keepdims=True) * (1.0 / blk)

    key_i = lax.broadcasted_iota(jnp.int32, (blk, blk), 0)
    qry_i = lax.broadcasted_iota(jnp.int32, (blk, blk), 1)
    causal_bias = jnp.where(key_i <= qry_i, 0.0, NEG_INF)
    groups = blk // V7X_SUBLANES

    def scores(i):
        s_buf = s_ref.at[i % 2]
        qt = q_ref[:, i * blk:(i + 1) * blk]
        if i > MOBA_TOPK:
            gate = _dot_hi(km_ref[...], qt.astype(F32))
            n_iota = lax.broadcasted_iota(jnp.int32, (nblk, blk), 0)
            valid = n_iota < i
            for n in range(i):
                g_n = gate[n:n + 1, :]
                beats = valid & ((gate > g_n) | ((gate == g_n) & (n_iota < n)))
                cnt = jnp.sum(jnp.where(beats, 1.0, 0.0), axis=0, keepdims=True)
                bias_ref[n:n + 1, :] = jnp.where(cnt < float(MOBA_TOPK), 0.0, NEG_INF)
        m_acc = None
        for j in range(i + 1):
            s = _dot(kn_ref[j * blk:(j + 1) * blk, :], qt)
            if j == i:
                s = s + causal_bias
            elif i > MOBA_TOPK:
                s = s + bias_ref[j:j + 1, :]
            s_buf[j * blk:(j + 1) * blk, :] = s
            part = jnp.max(s.reshape(groups, V7X_SUBLANES, blk), axis=0)
            m_acc = part if m_acc is None else jnp.maximum(m_acc, part)
        return jnp.max(m_acc, axis=0, keepdims=True)

    def finish(i, m_row):
        s_buf = s_ref.at[i % 2]
        l_acc = None
        for j in range(i + 1):
            p = jnp.exp2(s_buf[j * blk:(j + 1) * blk, :] - m_row)
            part = jnp.sum(p.reshape(groups, V7X_SUBLANES, blk), axis=0)
            l_acc = part if l_acc is None else l_acc + part
            p_ref[j * blk:(j + 1) * blk, :] = p.astype(BF16)
        l_row = jnp.sum(l_acc, axis=0, keepdims=True)
        o_t = _dot(v_ref[:, 0:(i + 1) * blk], p_ref[0:(i + 1) * blk, :]) / l_row
        o_ref[i * blk:(i + 1) * blk, :] = o_t.T.astype(o_ref.dtype)

    m_next = scores(0)
    for i in range(nblk):
        m_row = m_next
        if i + 1 < nblk:
            m_next = scores(i + 1)
        finish(i, m_row)


def _moba(qkv_t, batch, seq):
    nblk = seq // MOBA_BLOCK
    hb = MOBA_HEADS
    return pl.pallas_call(
        functools.partial(_moba_kernel, nblk=nblk),
        grid=(batch, MOBA_HEADS),
        in_specs=[pl.BlockSpec((None, HEAD_DIM, seq), lambda b, h: (b, h, 0)),
                  pl.BlockSpec((None, HEAD_DIM, seq), lambda b, h: (b, hb + h, 0)),
                  pl.BlockSpec((None, HEAD_DIM, seq), lambda b, h: (b, 2 * hb + h, 0))],
        out_specs=pl.BlockSpec((seq, HEAD_DIM), lambda b, h: (b, h)),
        out_shape=jax.ShapeDtypeStruct((batch * seq, MOBA_WIDTH), BF16),
        scratch_shapes=[pltpu.VMEM((seq, HEAD_DIM), BF16),
                        pltpu.VMEM((nblk, HEAD_DIM), F32),
                        pltpu.VMEM((2, seq, MOBA_BLOCK), F32),
                        pltpu.VMEM((seq, MOBA_BLOCK), BF16),
                        pltpu.VMEM((nblk, MOBA_BLOCK), F32)],
        compiler_params=_cparams("arbitrary", "arbitrary"),
        name="moba_attention",
    )(qkv_t, qkv_t, qkv_t)


def _mix_kernel(a_ref, b_ref, ga_ref, gb_ref, wa_ref, wb_ref, wo_ref, x_ref, nw_ref, x1_ref, h_ref, *, sub):
    for r in range(a_ref.shape[0] // sub):
        rows = slice(r * sub, (r + 1) * sub)
        ya = _dot(a_ref[rows, :], wa_ref[...])
        yb = _dot(b_ref[rows, :], wb_ref[...])
        merged = (ga_ref[rows, :] * ya + gb_ref[rows, :] * yb).astype(BF16)
        x1 = x_ref[rows, :] + _dot(merged, wo_ref[...])
        x1_ref[rows, :] = x1
        h = x1 * lax.rsqrt(jnp.mean(x1 * x1, axis=-1, keepdims=True) + NORM_EPS) * nw_ref[...]
        h_ref[rows, :] = h.astype(h_ref.dtype)


def _mix(ya_in, yb_in, gates, wa, wb, wo, x2d, nw, tm=512, sub=256):
    m, k = ya_in.shape
    n = wo.shape[1]
    return pl.pallas_call(
        functools.partial(_mix_kernel, sub=sub),
        grid=(m // tm,),
        in_specs=[pl.BlockSpec((tm, k), lambda i: (i, 0)),
                  pl.BlockSpec((tm, k), lambda i: (i, 0)),
                  pl.BlockSpec((tm, n), lambda i: (i, 0)),
                  pl.BlockSpec((tm, n), lambda i: (i, 1)),
                  pl.BlockSpec((k, n), lambda i: (0, 0)),
                  pl.BlockSpec((k, n), lambda i: (0, 0)),
                  pl.BlockSpec((n, n), lambda i: (0, 0)),
                  pl.BlockSpec((tm, n), lambda i: (i, 0)),
                  pl.BlockSpec((1, n), lambda i: (0, 0))],
        out_specs=[pl.BlockSpec((tm, n), lambda i: (i, 0)),
                   pl.BlockSpec((tm, n), lambda i: (i, 0))],
        out_shape=[jax.ShapeDtypeStruct((m, n), F32),
                   jax.ShapeDtypeStruct((m, n), BF16)],
        compiler_params=_cparams("arbitrary"),
        name="branch_mix_out_proj_norm2",
    )(ya_in, yb_in, gates, gates, wa, wb, wo, x2d, nw.reshape(1, n))


def _upproj_kernel(h_ref, wg_ref, wv_ref, cg_ref, cv_ref, bg_ref, bv_ref, o_ref,
                   w_ref, carry_ref, *, tiles_per_seq, sub):
    m = pl.program_id(1)
    tn = wg_ref.shape[1]

    @pl.when(m == 0)
    def _():
        w_ref[:, :tn] = wg_ref[...].astype(BF16)
        w_ref[:, tn:] = wv_ref[...].astype(BF16)

    @pl.when(m % tiles_per_seq == 0)
    def _():
        carry_ref[...] = jnp.zeros_like(carry_ref)

    cw = jnp.concatenate([cg_ref[...], cv_ref[...]], axis=1)
    bias = jnp.concatenate([bg_ref[...], bv_ref[...]], axis=1)
    prev = carry_ref[...]
    for r in range(h_ref.shape[0] // sub):
        u = _dot(h_ref[r * sub:(r + 1) * sub, :], w_ref[...])
        y = _causal_conv(u, prev, cw, FFN_CONV) + bias
        prev = u[sub - V7X_SUBLANES:sub, :]
        o_ref[r * sub:(r + 1) * sub, :] = (_silu(y[:, :tn]) * y[:, tn:]).astype(o_ref.dtype)
    carry_ref[...] = prev


def _upproj(h, w_up, conv_w, conv_b, seq, tm=1024, tn=512, sub=256):
    m, k = h.shape
    nb = D_FF // tn
    return pl.pallas_call(
        functools.partial(_upproj_kernel, tiles_per_seq=seq // tm, sub=sub),
        grid=(nb, m // tm),
        in_specs=[pl.BlockSpec((tm, k), lambda j, i: (i, 0)),
                  pl.BlockSpec((k, tn), lambda j, i: (0, j)),
                  pl.BlockSpec((k, tn), lambda j, i: (0, nb + j)),
                  pl.BlockSpec((FFN_CONV, tn), lambda j, i: (0, j)),
                  pl.BlockSpec((FFN_CONV, tn), lambda j, i: (0, nb + j)),
                  pl.BlockSpec((1, tn), lambda j, i: (0, j)),
                  pl.BlockSpec((1, tn), lambda j, i: (0, nb + j))],
        out_specs=pl.BlockSpec((tm, tn), lambda j, i: (i, j)),
        out_shape=jax.ShapeDtypeStruct((m, D_FF), BF16),
        scratch_shapes=[pltpu.VMEM((k, 2 * tn), BF16),
                        pltpu.VMEM((V7X_SUBLANES, 2 * tn), F32)],
        compiler_params=_cparams("arbitrary", "arbitrary"),
        name="up_proj_conv_gate",
    )(h, w_up, w_up, conv_w, conv_w, conv_b, conv_b)


def _downproj_kernel(a_ref, w_ref, x_ref, o_ref):
    o_ref[...] = x_ref[...] + _dot(a_ref[...], w_ref[...])


def _downproj(a, w, x1, tm=512, tn=1024):
    m, k = a.shape
    n = w.shape[1]
    return pl.pallas_call(
        _downproj_kernel,
        grid=(n // tn, m // tm),
        in_specs=[pl.BlockSpec((tm, k), lambda j, i: (i, 0)),
                  pl.BlockSpec((k, tn), lambda j, i: (0, j)),
                  pl.BlockSpec((tm, tn), lambda j, i: (i, j))],
        out_specs=pl.BlockSpec((tm, tn), lambda j, i: (i, j)),
        out_shape=jax.ShapeDtypeStruct((m, n), F32),
        compiler_params=_cparams("arbitrary", "arbitrary"),
        name="down_proj",
    )(a, w, x1)


def _downproj_norm_kernel(a_ref, w_ref, x_ref, nw_ref, o_ref):
    x2 = x_ref[...] + _dot(a_ref[...], w_ref[...])
    o_ref[...] = x2 * lax.rsqrt(jnp.mean(x2 * x2, axis=-1, keepdims=True) + NORM_EPS) * nw_ref[...]


def _downproj_norm(a, w, x1, nw, tm=512):
    m, k = a.shape
    n = w.shape[1]
    return pl.pallas_call(
        _downproj_norm_kernel,
        grid=(m // tm,),
        in_specs=[pl.BlockSpec((tm, k), lambda i: (i, 0)),
                  pl.BlockSpec((k, n), lambda i: (0, 0)),
                  pl.BlockSpec((tm, n), lambda i: (i, 0)),
                  pl.BlockSpec((1, n), lambda i: (0, 0))],
        out_specs=pl.BlockSpec((tm, n), lambda i: (i, 0)),
        out_shape=jax.ShapeDtypeStruct((m, n), F32),
        compiler_params=_cparams("arbitrary"),
        name="down_proj_final_norm",
    )(a, w, x1, nw.reshape(1, n))


def _pad_lanes(vec, offset):
    return jnp.zeros((1, V7X_LANES), F32).at[0, offset:offset + vec.shape[0]].set(vec.astype(F32))


def kernel(x, positions, ln1, w_in, gdn_conv, gdn_a_log, gdn_dt_bias, gdn_norm, w_branch_a, w_branch_b,
           w_out, ln2, w_up, ffn_conv, ffn_conv_bias, w_down, final_norm):
    batch, seq, d = x.shape
    m = batch * seq
    depth = ln1.shape[0]
    o_qkv_a = 3 * GDN_WIDTH
    o_small = o_qkv_a + 2 * GDN_HEADS
    o_z = o_small + GDN_WIDTH
    o_qkv_b = o_z + 3 * MOBA_WIDTH
    inv_freq = (ROPE_THETA ** (-jnp.arange(ROPE_HALF, dtype=F32) / ROPE_HALF)).reshape(ROPE_HALF, 1)

    x2d = x.reshape(m, d)
    for l in range(depth):
        wt = jnp.swapaxes(w_in[l], 0, 1)
        wt_zs = jnp.concatenate(
            [wt[o_small:o_z], jnp.pad(wt[o_qkv_a:o_small], ((0, V7X_LANES - 2 * GDN_HEADS), (0, 0)))], axis=0)

        h1 = _norm_cast(x2d, ln1[l])
        qkv_a = _gdn_proj(h1, wt, gdn_conv[l], seq)
        zs = _proj_t(h1, wt_zs, 0, GDN_WIDTH + V7X_LANES, act=None, out_dtype=F32, tm=1024,
                     tn=GDN_WIDTH + V7X_LANES, name="z_beta_decay_proj")
        gates = _proj_t(h1, wt, o_qkv_b, 2 * D_MODEL, act="sigmoid", out_dtype=BF16, tm=1024, tn=1024, name="gate_proj")
        qkv_b_t = _moba_proj(h1, wt, o_z, positions, inv_freq, batch, seq)

        gdn_out = _gdn(qkv_a, zs, _pad_lanes(gdn_a_log[l], GDN_HEADS), _pad_lanes(gdn_dt_bias[l], GDN_HEADS),
                       gdn_norm[l].reshape(1, HEAD_DIM), batch, seq)
        attn = _moba(qkv_b_t, batch, seq)
        x1, h2 = _mix(gdn_out, attn, gates, w_branch_a[l].astype(BF16), w_branch_b[l].astype(BF16),
                      w_out[l].astype(BF16), x2d, ln2[l])
        act = _upproj(h2, w_up[l], ffn_conv[l], ffn_conv_bias[l].reshape(1, 2 * D_FF), seq)
        if l == depth - 1:
            return _downproj_norm(act, w_down[l].astype(BF16), x1, final_norm).reshape(batch, seq, d)
        x2d = _downproj(act, w_down[l].astype(BF16), x1)
    return _final_norm(x2d, final_norm).reshape(batch, seq, d)
```

```python
import functools
import math

import jax
import jax.numpy as jnp
import numpy as np
from jax import lax
from jax.experimental import pallas as pl
from jax.experimental.pallas import tpu as pltpu

F32 = jnp.float32
BF16 = jnp.bfloat16

D_MODEL = 2048
GDN_HEADS = 8
HEAD_DIM = 128
GDN_WIDTH = GDN_HEADS * HEAD_DIM
GDN_CONV = 4
GDN_CHUNK = 64
MOBA_HEADS = 8
MOBA_WIDTH = MOBA_HEADS * HEAD_DIM
MOBA_BLOCK = 256
MOBA_TOPK = 3
ROPE_THETA = 500000.0
ROPE_DIM = HEAD_DIM // 4
ROPE_HALF = ROPE_DIM // 2
D_FF = 5632
FFN_CONV = 3
NORM_EPS = 1e-6
L2_EPS = 1e-6
NEG_INF = -1e30
NEG_LOG2_E = -math.log2(math.e)

V7X_LANES = 128
V7X_SUBLANES = 8
V7X_VMEM_LIMIT_BYTES = 56 * 1024 * 1024

GDN_GROUP = 4 * GDN_CHUNK
MOBA_Q_SCALE = HEAD_DIM ** -0.5 * math.log2(math.e)


def _cparams(*sem):
    return pltpu.CompilerParams(dimension_semantics=sem, vmem_limit_bytes=V7X_VMEM_LIMIT_BYTES)


def _dot(a, b):
    return jnp.dot(a, b, preferred_element_type=F32)


def _dot_nt(a, b):
    return lax.dot_general(a, b, (((1,), (1,)), ((), ())), preferred_element_type=F32)


def _dot_hi(a, b):
    return jnp.dot(a, b, preferred_element_type=F32, precision=lax.Precision.HIGHEST)


def _split3(x):
    hi = x.astype(BF16)
    r1 = x - hi.astype(F32)
    mid = r1.astype(BF16)
    lo = (r1 - mid.astype(F32)).astype(BF16)
    return hi, mid, lo


def _sigmoid(x):
    return 1.0 / (1.0 + jnp.exp2(x * NEG_LOG2_E))


def _silu(x):
    return x * _sigmoid(x)


def _shift_rows(cur, prev8, s):
    if s == 0:
        return cur
    rolled = pltpu.roll(cur, s, axis=0)
    rolled_prev = pltpu.roll(prev8, s, axis=0)
    row = lax.broadcasted_iota(jnp.int32, prev8.shape, 0)
    first = jnp.where(row < s, rolled_prev, rolled[0:V7X_SUBLANES])
    return jnp.concatenate([first, rolled[V7X_SUBLANES:]], axis=0)


def _causal_conv(cur, prev8, cw, width):
    y = cw[width - 1:width, :] * cur
    for j in range(width - 1):
        y = y + cw[j:j + 1, :] * _shift_rows(cur, prev8, width - 1 - j)
    return y


def _norm_kernel(x_ref, w_ref, h_ref):
    x = x_ref[...]
    y = x * lax.rsqrt(jnp.mean(x * x, axis=-1, keepdims=True) + NORM_EPS) * w_ref[...]
    h_ref[...] = y.astype(h_ref.dtype)


def _norm_cast(x2d, w, tm=512):
    m, d = x2d.shape
    return pl.pallas_call(
        _norm_kernel,
        grid=(m // tm,),
        in_specs=[pl.BlockSpec((tm, d), lambda i: (i, 0)),
                  pl.BlockSpec((1, d), lambda i: (0, 0))],
        out_specs=pl.BlockSpec((tm, d), lambda i: (i, 0)),
        out_shape=jax.ShapeDtypeStruct((m, d), BF16),
        compiler_params=_cparams("arbitrary"),
        name="norm1",
    )(x2d, w.reshape(1, d))


def _final_norm_kernel(x_ref, w_ref, o_ref):
    x = x_ref[...]
    o_ref[...] = x * lax.rsqrt(jnp.mean(x * x, axis=-1, keepdims=True) + NORM_EPS) * w_ref[...]


def _final_norm(x2d, w, tm=512):
    m, d = x2d.shape
    return pl.pallas_call(
        _final_norm_kernel,
        grid=(m // tm,),
        in_specs=[pl.BlockSpec((tm, d), lambda i: (i, 0)),
                  pl.BlockSpec((1, d), lambda i: (0, 0))],
        out_specs=pl.BlockSpec((tm, d), lambda i: (i, 0)),
        out_shape=jax.ShapeDtypeStruct((m, d), F32),
        compiler_params=_cparams("arbitrary"),
        name="final_norm",
    )(x2d, w.reshape(1, d))


def _proj_t_kernel(h_ref, wt_ref, o_ref, w16_ref, *, act):
    @pl.when(pl.program_id(1) == 0)
    def _():
        w16_ref[...] = wt_ref[...].T.astype(BF16)

    acc = _dot(h_ref[...], w16_ref[...])
    if act == "sigmoid":
        acc = _sigmoid(acc)
    o_ref[...] = acc.astype(o_ref.dtype)


def _proj_t(h, wt, row0, n, *, act, out_dtype, tm, tn, name):
    m, k = h.shape
    return pl.pallas_call(
        functools.partial(_proj_t_kernel, act=act),
        grid=(n // tn, m // tm),
        in_specs=[pl.BlockSpec((tm, k), lambda j, i: (i, 0)),
                  pl.BlockSpec((pl.Element(tn), pl.Element(k)), lambda j, i: (pl.multiple_of(row0 + j * tn, 16), 0))],
        out_specs=pl.BlockSpec((tm, tn), lambda j, i: (i, j)),
        out_shape=jax.ShapeDtypeStruct((m, n), out_dtype),
        scratch_shapes=[pltpu.VMEM((k, tn), BF16)],
        compiler_params=_cparams("arbitrary", "arbitrary"),
        name=name,
    )(h, wt)


def _gdn_proj_kernel(h_ref, w_ref, cw_ref, o_ref, w16_ref, carry_ref, *, tiles_per_seq, tiles_per_part, sub):
    n = pl.program_id(0)
    m = pl.program_id(1)
    tn = w_ref.shape[0]

    @pl.when(m == 0)
    def _():
        w16_ref[...] = w_ref[...].T.astype(BF16)

    @pl.when(m % tiles_per_seq == 0)
    def _():
        carry_ref[...] = jnp.zeros_like(carry_ref)

    part = n // tiles_per_part
    q_scale = jnp.where(part == 0, HEAD_DIM ** -0.5, 1.0).astype(F32)
    prev = carry_ref[...]
    for r in range(h_ref.shape[0] // sub):
        acc = _dot(h_ref[r * sub:(r + 1) * sub, :], w16_ref[...])
        y = _silu(_causal_conv(acc, prev, cw_ref[...], GDN_CONV))
        prev = acc[sub - V7X_SUBLANES:sub, :]
        heads = []
        for hd in range(tn // HEAD_DIM):
            blk = y[:, hd * HEAD_DIM:(hd + 1) * HEAD_DIM]
            inv = lax.rsqrt(jnp.sum(blk * blk, axis=-1, keepdims=True) + L2_EPS) * q_scale
            heads.append(blk * jnp.where(part < 2, inv, 1.0))
        o_ref[r * sub:(r + 1) * sub, :] = jnp.concatenate(heads, axis=1)
    carry_ref[...] = prev


def _gdn_proj(h, w_in, conv_w, seq, tm=1024, tn=1024, sub=128):
    m, k = h.shape
    n = 3 * GDN_WIDTH
    return pl.pallas_call(
        functools.partial(_gdn_proj_kernel, tiles_per_seq=seq // tm, tiles_per_part=GDN_WIDTH // tn, sub=sub),
        grid=(n // tn, m // tm),
        in_specs=[pl.BlockSpec((tm, k), lambda j, i: (i, 0)),
                  pl.BlockSpec((tn, k), lambda j, i: (j, 0)),
                  pl.BlockSpec((GDN_CONV, tn), lambda j, i: (0, j))],
        out_specs=pl.BlockSpec((tm, tn), lambda j, i: (i, j)),
        out_shape=jax.ShapeDtypeStruct((m, n), F32),
        scratch_shapes=[pltpu.VMEM((k, tn), BF16),
                        pltpu.VMEM((V7X_SUBLANES, tn), F32)],
        compiler_params=_cparams("arbitrary", "arbitrary"),
        name="gdn_qkv_proj",
    )(h, w_in, conv_w)


def _moba_proj_kernel(h_ref, wt_ref, pos_ref, freq_ref, o_ref, w16_ref):
    part = pl.program_id(0)

    @pl.when(pl.program_id(1) == 0)
    def _():
        w16_ref[...] = wt_ref[...].astype(BF16)

    scale = jnp.where(part == 0, MOBA_Q_SCALE, 1.0).astype(F32)
    acc = _dot_nt(w16_ref[...], h_ref[...]) * scale
    ang = freq_ref[...] * pos_ref[...].astype(F32)
    roped = part < 2
    cos = jnp.where(roped, jnp.cos(ang), 1.0)
    sin = jnp.where(roped, jnp.sin(ang), 0.0)
    rows = []
    for hd in range(acc.shape[0] // HEAD_DIM):
        base = hd * HEAD_DIM
        x1 = acc[base:base + ROPE_HALF]
        x2 = acc[base + ROPE_HALF:base + ROPE_DIM]
        rows.append(x1 * cos - x2 * sin)
        rows.append(x2 * cos + x1 * sin)
        rows.append(acc[base + ROPE_DIM:base + HEAD_DIM])
    o_ref[...] = jnp.concatenate(rows, axis=0).astype(o_ref.dtype)


def _moba_proj(h, wt, row0, positions, inv_freq, batch, seq, tt=512):
    m, k = h.shape
    tr = MOBA_WIDTH
    n = 3 * MOBA_WIDTH
    per_seq = seq // tt
    return pl.pallas_call(
        _moba_proj_kernel,
        grid=(n // tr, m // tt),
        in_specs=[pl.BlockSpec((tt, k), lambda r, i: (i, 0)),
                  pl.BlockSpec((pl.Element(tr), pl.Element(k)), lambda r, i: (pl.multiple_of(row0 + r * tr, 16), 0)),
                  pl.BlockSpec((None, 1, tt), lambda r, i: (i // per_seq, 0, i % per_seq)),
                  pl.BlockSpec((ROPE_HALF, 1), lambda r, i: (0, 0))],
        out_specs=pl.BlockSpec((None, tr, tt), lambda r, i: (i // per_seq, r, i % per_seq)),
        out_shape=jax.ShapeDtypeStruct((batch, n, seq), BF16),
        scratch_shapes=[pltpu.VMEM((tr, k), BF16)],
        compiler_params=_cparams("arbitrary", "arbitrary"),
        name="moba_qkv_proj",
    )(h, wt, positions.reshape(batch, 1, seq), inv_freq)


def _gdn_kernel(q_ref, k_ref, v_ref, z_ref, sm_ref, a_ref, dt_ref, nw_ref, o_ref,
                state_ref, lbd_ref, subd_ref, *, groups, heads_per_step):
    t = pl.program_id(2)
    g = GDN_GROUP
    c = GDN_CHUNK

    @pl.when(t == 0)
    def _():
        state_ref[...] = jnp.zeros_like(state_ref)

    ri = lax.broadcasted_iota(jnp.int32, (g, g), 0)
    ci = lax.broadcasted_iota(jnp.int32, (g, g), 1)
    shift = int(math.log2(c))
    same = jnp.right_shift(ri, shift) == jnp.right_shift(ci, shift)
    lbd_ref[...] = jnp.where(same & (ci <= ri), 1.0, 0.0)
    subd_ref[...] = jnp.where(same & (ri > ci), 1.0, 0.0)
    lane = lax.broadcasted_iota(jnp.int32, (g, V7X_LANES), 1)
    eye = jnp.where(ri == ci, 1.0, 0.0)

    def group_body(gi, carry):
        hs = range(heads_per_step)
        nch = g // c
        r0 = pl.multiple_of(gi * g, g)
        sm = sm_ref[pl.ds(r0, g), :]
        beta_all = _sigmoid(sm)
        xs = sm + dt_ref[...]
        softplus = jnp.maximum(xs, 0.0) + jnp.log1p(jnp.exp(-jnp.abs(xs)))
        g_all = -jnp.exp(a_ref[...]) * softplus
        lbd = lbd_ref[...]
        lbd16 = lbd.astype(BF16)
        strict = subd_ref[...]
        cols = [slice(hh * HEAD_DIM, (hh + 1) * HEAD_DIM) for hh in hs]
        heads = [pl.program_id(1) * heads_per_step + hh for hh in hs]

        k = [k_ref[pl.ds(r0, g), cols[hh]] for hh in hs]
        k16 = [k[hh].astype(BF16) for hh in hs]
        beta_b = [jnp.broadcast_to(jnp.sum(jnp.where(lane == heads[hh], beta_all, 0.0), axis=-1, keepdims=True),
                                   (g, HEAD_DIM)) for hh in hs]
        g_b = [jnp.broadcast_to(jnp.sum(jnp.where(lane == GDN_HEADS + heads[hh], g_all, 0.0), axis=-1, keepdims=True),
                                (g, HEAD_DIM)) for hh in hs]
        kb = [k[hh] * beta_b[hh] for hh in hs]

        cs = [_dot(lbd16, jnp.concatenate(_split3(g_b[hh]), axis=1)) for hh in hs]
        kq = [_dot_nt(jnp.concatenate([kb[hh].astype(BF16), q_ref[pl.ds(r0, g), cols[hh]].astype(BF16)], axis=0), k16[hh])
              for hh in hs]
        kk = [kq[hh][:g] for hh in hs]
        qk_raw = [kq[hh][g:] for hh in hs]
        gc_b = [cs[hh][:, :HEAD_DIM] + cs[hh][:, HEAD_DIM:2 * HEAD_DIM] + cs[hh][:, 2 * HEAD_DIM:] for hh in hs]
        gamma = []
        for hh in hs:
            gc_row = gc_b[hh].T[0:1, :]
            dmat = jnp.where(lbd > 0.0, jnp.concatenate([gc_b[hh], gc_b[hh]], axis=1) - gc_row, 0.0)
            gamma.append(jnp.exp(dmat))
        eg = [jnp.exp(gc_b[hh]) for hh in hs]

        p = [-jnp.where(strict > 0.0, kk[hh] * gamma[hh], 0.0) for hh in hs]
        t_mat = [eye + p[hh] for hh in hs]
        for _ in range(5):
            p16 = [p[hh].astype(BF16) for hh in hs]
            p = [_dot(p16[hh], p16[hh]) for hh in hs]
            tp = [_dot(t_mat[hh].astype(BF16), p[hh].astype(BF16)) for hh in hs]
            t_mat = [t_mat[hh] + tp[hh] for hh in hs]
        uw = [_dot(t_mat[hh].astype(BF16),
                   jnp.concatenate([v_ref[pl.ds(r0, g), cols[hh]] * beta_b[hh], kb[hh] * eg[hh]], axis=1).astype(BF16))
              for hh in hs]
        u = [uw[hh][:, :HEAD_DIM] for hh in hs]
        w16 = [uw[hh][:, HEAD_DIM:].astype(BF16) for hh in hs]
        qk16 = [jnp.where(lbd > 0.0, qk_raw[hh] * gamma[hh], 0.0).astype(BF16) for hh in hs]
        qg = [(q_ref[pl.ds(r0, g), cols[hh]] * eg[hh]).astype(BF16) for hh in hs]
        kd_t = []
        for hh in hs:
            gc_last = jnp.concatenate(
                [jnp.broadcast_to(gc_b[hh][(ch + 1) * c - 1:(ch + 1) * c, :], (c, HEAD_DIM)) for ch in range(nch)],
                axis=0)
            kd_t.append((k[hh] * jnp.exp(gc_last - gc_b[hh])).T.astype(BF16))

        state = [state_ref[hh] for hh in hs]
        outs = [[] for _ in hs]
        zeros_c = jnp.zeros((c, HEAD_DIM), F32)
        for ch in range(nch):
            rows = slice(ch * c, (ch + 1) * c)
            s16 = [state[hh].astype(BF16) for hh in hs]
            wq = [_dot(jnp.concatenate([w16[hh][rows], qg[hh][rows]], axis=0), s16[hh]) for hh in hs]
            ws = [wq[hh][:c] for hh in hs]
            qs = [wq[hh][c:] for hh in hs]
            vn_all = [jnp.concatenate([zeros_c] * ch + [u[hh][rows] - ws[hh]] + [zeros_c] * (nch - 1 - ch),
                                      axis=0).astype(BF16) for hh in hs]
            so = [_dot(jnp.concatenate([kd_t[hh], qk16[hh][rows]], axis=0), vn_all[hh]) for hh in hs]
            sv = [so[hh][:HEAD_DIM] for hh in hs]
            ov = [so[hh][HEAD_DIM:] for hh in hs]
            for hh in hs:
                outs[hh].append(qs[hh] + ov[hh])
                state[hh] = state[hh] * eg[hh][(ch + 1) * c - 1:(ch + 1) * c, :] + sv[hh]
        for hh in hs:
            state_ref[hh] = state[hh]
            o = jnp.concatenate(outs[hh], axis=0)
            o = o * lax.rsqrt(jnp.mean(o * o, axis=-1, keepdims=True) + NORM_EPS) * nw_ref[...]
            o_ref[pl.ds(r0, g), cols[hh]] = (o * _silu(z_ref[pl.ds(r0, g), cols[hh]])).astype(o_ref.dtype)
        return carry

    lax.fori_loop(0, groups, group_body, 0)


def _gdn(qkv, zs, a_pad, dt_pad, norm_w, batch, seq, ts=512, heads_per_step=8):
    m = qkv.shape[0]
    tiles = seq // ts
    hp = heads_per_step
    hb = GDN_HEADS // hp
    wd = hp * HEAD_DIM
    row = lambda b, h, t: b * tiles + t
    return pl.pallas_call(
        functools.partial(_gdn_kernel, groups=ts // GDN_GROUP, heads_per_step=hp),
        grid=(batch, hb, tiles),
        in_specs=[pl.BlockSpec((ts, wd), lambda b, h, t: (row(b, h, t), h)),
                  pl.BlockSpec((ts, wd), lambda b, h, t: (row(b, h, t), hb + h)),
                  pl.BlockSpec((ts, wd), lambda b, h, t: (row(b, h, t), 2 * hb + h)),
                  pl.BlockSpec((ts, wd), lambda b, h, t: (row(b, h, t), h)),
                  pl.BlockSpec((ts, V7X_LANES), lambda b, h, t: (row(b, h, t), GDN_WIDTH // V7X_LANES)),
                  pl.BlockSpec((1, V7X_LANES), lambda b, h, t: (0, 0)),
                  pl.BlockSpec((1, V7X_LANES), lambda b, h, t: (0, 0)),
                  pl.BlockSpec((1, HEAD_DIM), lambda b, h, t: (0, 0))],
        out_specs=pl.BlockSpec((ts, wd), lambda b, h, t: (row(b, h, t), h)),
        out_shape=jax.ShapeDtypeStruct((m, GDN_WIDTH), BF16),
        scratch_shapes=[pltpu.VMEM((hp, HEAD_DIM, HEAD_DIM), F32),
                        pltpu.VMEM((GDN_GROUP, GDN_GROUP), F32),
                        pltpu.VMEM((GDN_GROUP, GDN_GROUP), F32)],
        compiler_params=_cparams("arbitrary", "arbitrary", "arbitrary"),
        name="gated_delta_rule",
    )(qkv, qkv, qkv, zs, zs, a_pad, dt_pad, norm_w)


def _moba_kernel(q_ref, k_ref, v_ref, o_ref, kn_ref, km_ref, s_ref, p_ref, bias_ref, *, nblk):
    blk = MOBA_BLOCK
    for j in range(nblk):
        kt = k_ref[:, j * blk:(j + 1) * blk].astype(F32)
        kn = kt.T
        kn_ref[j * blk:(j + 1) * blk, :] = kn.astype(BF16)
        km_ref[j:j + 1, :] = jnp.sum(kn, axis=0, keepdims=True) * (1.0 / blk)

    key_i = lax.broadcasted_iota(jnp.int32, (blk, blk), 0)
    qry_i = lax.broadcasted_iota(jnp.int32, (blk, blk), 1)
    causal_bias = jnp.where(key_i <= qry_i, 0.0, NEG_INF)
    groups = blk // V7X_SUBLANES

    def scores(i):
        s_buf = s_ref.at[i % 2]
        qt = q_ref[:, i * blk:(i + 1) * blk]
        if i > MOBA_TOPK:
            gate = _dot_hi(km_ref[...], qt.astype(F32))
            n_iota = lax.broadcasted_iota(jnp.int32, (nblk, blk), 0)
            valid = n_iota < i
            for n in range(i):
                g_n = gate[n:n + 1, :]
                beats = valid & ((gate > g_n) | ((gate == g_n) & (n_iota < n)))
                cnt = jnp.sum(jnp.where(beats, 1.0, 0.0), axis=0, keepdims=True)
                bias_ref[n:n + 1, :] = jnp.where(cnt < float(MOBA_TOPK), 0.0, NEG_INF)
        m_acc = None
        for j in range(i + 1):
            s = _dot(kn_ref[j * blk:(j + 1) * blk, :], qt)
            if j == i:
                s = s + causal_bias
            elif i > MOBA_TOPK:
                s = s + bias_ref[j:j + 1, :]
            s_buf[j * blk:(j + 1) * blk, :] = s
            part = jnp.max(s.reshape(groups, V7X_SUBLANES, blk), axis=0)
            m_acc = part if m_acc is None else jnp.maximum(m_acc, part)
        return jnp.max(m_acc, axis=0, keepdims=True)

    def finish(i, m_row):
        s_buf = s_ref.at[i % 2]
        l_acc = None
        for j in range(i + 1):
            p = jnp.exp2(s_buf[j * blk:(j + 1) * blk, :] - m_row)
            part = jnp.sum(p.reshape(groups, V7X_SUBLANES, blk), axis=0)
            l_acc = part if l_acc is None else l_acc + part
            p_ref[j * blk:(j + 1) * blk, :] = p.astype(BF16)
        l_row = jnp.sum(l_acc, axis=0, keepdims=True)
        o_t = _dot(v_ref[:, 0:(i + 1) * blk], p_ref[0:(i + 1) * blk, :]) / l_row
        o_ref[i * blk:(i + 1) * blk, :] = o_t.T.astype(o_ref.dtype)

    m_next = scores(0)
    for i in range(nblk):
        m_row = m_next
        if i + 1 < nblk:
            m_next = scores(i + 1)
        finish(i, m_row)


def _moba(qkv_t, batch, seq):
    nblk = seq // MOBA_BLOCK
    hb = MOBA_HEADS
    return pl.pallas_call(
        functools.partial(_moba_kernel, nblk=nblk),
        grid=(batch, MOBA_HEADS),
        in_specs=[pl.BlockSpec((None, HEAD_DIM, seq), lambda b, h: (b, h, 0)),
                  pl.BlockSpec((None, HEAD_DIM, seq), lambda b, h: (b, hb + h, 0)),
                  pl.BlockSpec((None, HEAD_DIM, seq), lambda b, h: (b, 2 * hb + h, 0))],
        out_specs=pl.BlockSpec((seq, HEAD_DIM), lambda b, h: (b, h)),
        out_shape=jax.ShapeDtypeStruct((batch * seq, MOBA_WIDTH), BF16),
        scratch_shapes=[pltpu.VMEM((seq, HEAD_DIM), BF16),
                        pltpu.VMEM((nblk, HEAD_DIM), F32),
                        pltpu.VMEM((2, seq, MOBA_BLOCK), F32),
                        pltpu.VMEM((seq, MOBA_BLOCK), BF16),
                        pltpu.VMEM((nblk, MOBA_BLOCK), F32)],
        compiler_params=_cparams("arbitrary", "arbitrary"),
        name="moba_attention",
    )(qkv_t, qkv_t, qkv_t)


def _mix_kernel(a_ref, b_ref, ga_ref, gb_ref, wa_ref, wb_ref, wo_ref, x_ref, nw_ref, x1_ref, h_ref, *, sub):
    for r in range(a_ref.shape[0] // sub):
        rows = slice(r * sub, (r + 1) * sub)
        ya = _dot(a_ref[rows, :], wa_ref[...])
        yb = _dot(b_ref[rows, :], wb_ref[...])
        merged = (ga_ref[rows, :] * ya + gb_ref[rows, :] * yb).astype(BF16)
        x1 = x_ref[rows, :] + _dot(merged, wo_ref[...])
        x1_ref[rows, :] = x1
        h = x1 * lax.rsqrt(jnp.mean(x1 * x1, axis=-1, keepdims=True) + NORM_EPS) * nw_ref[...]
        h_ref[rows, :] = h.astype(h_ref.dtype)


def _mix(ya_in, yb_in, gates, wa, wb, wo, x2d, nw, tm=512, sub=512):
    m, k = ya_in.shape
    n = wo.shape[1]
    return pl.pallas_call(
        functools.partial(_mix_kernel, sub=sub),
        grid=(m // tm,),
        in_specs=[pl.BlockSpec((tm, k), lambda i: (i, 0)),
                  pl.BlockSpec((tm, k), lambda i: (i, 0)),
                  pl.BlockSpec((tm, n), lambda i: (i, 0)),
                  pl.BlockSpec((tm, n), lambda i: (i, 1)),
                  pl.BlockSpec((k, n), lambda i: (0, 0)),
                  pl.BlockSpec((k, n), lambda i: (0, 0)),
                  pl.BlockSpec((n, n), lambda i: (0, 0)),
                  pl.BlockSpec((tm, n), lambda i: (i, 0)),
                  pl.BlockSpec((1, n), lambda i: (0, 0))],
        out_specs=[pl.BlockSpec((tm, n), lambda i: (i, 0)),
                   pl.BlockSpec((tm, n), lambda i: (i, 0))],
        out_shape=[jax.ShapeDtypeStruct((m, n), F32),
                   jax.ShapeDtypeStruct((m, n), BF16)],
        compiler_params=_cparams("arbitrary"),
        name="branch_mix_out_proj_norm2",
    )(ya_in, yb_in, gates, gates, wa, wb, wo, x2d, nw.reshape(1, n))


def _upproj_kernel(h_ref, wg_ref, wv_ref, cg_ref, cv_ref, bg_ref, bv_ref, o_ref,
                   w_ref, carry_ref, *, tiles_per_seq, sub):
    m = pl.program_id(1)
    tn = wg_ref.shape[1]

    @pl.when(m == 0)
    def _():
        w_ref[:, :tn] = wg_ref[...].astype(BF16)
        w_ref[:, tn:] = wv_ref[...].astype(BF16)

    @pl.when(m % tiles_per_seq == 0)
    def _():
        carry_ref[...] = jnp.zeros_like(carry_ref)

    cw = jnp.concatenate([cg_ref[...], cv_ref[...]], axis=1)
    bias = jnp.concatenate([bg_ref[...], bv_ref[...]], axis=1)
    prev = carry_ref[...]
    for r in range(h_ref.shape[0] // sub):
        u = _dot(h_ref[r * sub:(r + 1) * sub, :], w_ref[...])
        y = _causal_conv(u, prev, cw, FFN_CONV) + bias
        prev = u[sub - V7X_SUBLANES:sub, :]
        o_ref[r * sub:(r + 1) * sub, :] = (_silu(y[:, :tn]) * y[:, tn:]).astype(o_ref.dtype)
    carry_ref[...] = prev


def _upproj(h, w_up, conv_w, conv_b, seq, tm=1024, tn=512, sub=1024):
    m, k = h.shape
    nb = D_FF // tn
    return pl.pallas_call(
        functools.partial(_upproj_kernel, tiles_per_seq=seq // tm, sub=sub),
        grid=(nb, m // tm),
        in_specs=[pl.BlockSpec((tm, k), lambda j, i: (i, 0)),
                  pl.BlockSpec((k, tn), lambda j, i: (0, j)),
                  pl.BlockSpec((k, tn), lambda j, i: (0, nb + j)),
                  pl.BlockSpec((FFN_CONV, tn), lambda j, i: (0, j)),
                  pl.BlockSpec((FFN_CONV, tn), lambda j, i: (0, nb + j)),
                  pl.BlockSpec((1, tn), lambda j, i: (0, j)),
                  pl.BlockSpec((1, tn), lambda j, i: (0, nb + j))],
        out_specs=pl.BlockSpec((tm, tn), lambda j, i: (i, j)),
        out_shape=jax.ShapeDtypeStruct((m, D_FF), BF16),
        scratch_shapes=[pltpu.VMEM((k, 2 * tn), BF16),
                        pltpu.VMEM((V7X_SUBLANES, 2 * tn), F32)],
        compiler_params=_cparams("arbitrary", "arbitrary"),
        name="up_proj_conv_gate",
    )(h, w_up, w_up, conv_w, conv_w, conv_b, conv_b)


def _downproj_kernel(a_ref, w_ref, x_ref, o_ref):
    o_ref[...] = x_ref[...] + _dot(a_ref[...], w_ref[...])


def _downproj(a, w, x1, tm=512, tn=1024):
    m, k = a.shape
    n = w.shape[1]
    return pl.pallas_call(
        _downproj_kernel,
        grid=(n // tn, m // tm),
        in_specs=[pl.BlockSpec((tm, k), lambda j, i: (i, 0)),
                  pl.BlockSpec((k, tn), lambda j, i: (0, j)),
                  pl.BlockSpec((tm, tn), lambda j, i: (i, j))],
        out_specs=pl.BlockSpec((tm, tn), lambda j, i: (i, j)),
        out_shape=jax.ShapeDtypeStruct((m, n), F32),
        compiler_params=_cparams("arbitrary", "arbitrary"),
        name="down_proj",
    )(a, w, x1)


def _downproj_norm_kernel(a_ref, w_ref, x_ref, nw_ref, o_ref):
    x2 = x_ref[...] + _dot(a_ref[...], w_ref[...])
    o_ref[...] = x2 * lax.rsqrt(jnp.mean(x2 * x2, axis=-1, keepdims=True) + NORM_EPS) * nw_ref[...]


def _downproj_norm(a, w, x1, nw, tm=512):
    m, k = a.shape
    n = w.shape[1]
    return pl.pallas_call(
        _downproj_norm_kernel,
        grid=(m // tm,),
        in_specs=[pl.BlockSpec((tm, k), lambda i: (i, 0)),
                  pl.BlockSpec((k, n), lambda i: (0, 0)),
                  pl.BlockSpec((tm, n), lambda i: (i, 0)),
                  pl.BlockSpec((1, n), lambda i: (0, 0))],
        out_specs=pl.BlockSpec((tm, n), lambda i: (i, 0)),
        out_shape=jax.ShapeDtypeStruct((m, n), F32),
        compiler_params=_cparams("arbitrary"),
        name="down_proj_final_norm",
    )(a, w, x1, nw.reshape(1, n))


def _pad_lanes(vec, offset):
    return jnp.zeros((1, V7X_LANES), F32).at[0, offset:offset + vec.shape[0]].set(vec.astype(F32))


def kernel(x, positions, ln1, w_in, gdn_conv, gdn_a_log, gdn_dt_bias, gdn_norm, w_branch_a, w_branch_b,
           w_out, ln2, w_up, ffn_conv, ffn_conv_bias, w_down, final_norm):
    batch, seq, d = x.shape
    m = batch * seq
    depth = ln1.shape[0]
    o_qkv_a = 3 * GDN_WIDTH
    o_small = o_qkv_a + 2 * GDN_HEADS
    o_z = o_small + GDN_WIDTH
    o_qkv_b = o_z + 3 * MOBA_WIDTH
    inv_freq = (ROPE_THETA ** (-jnp.arange(ROPE_HALF, dtype=F32) / ROPE_HALF)).reshape(ROPE_HALF, 1)

    x2d = x.reshape(m, d)
    for l in range(depth):
        wt = jnp.swapaxes(w_in[l], 0, 1)
        wt_zs = jnp.concatenate(
            [wt[o_small:o_z], jnp.pad(wt[o_qkv_a:o_small], ((0, V7X_LANES - 2 * GDN_HEADS), (0, 0)))], axis=0)

        h1 = _norm_cast(x2d, ln1[l])
        qkv_a = _gdn_proj(h1, wt, gdn_conv[l], seq)
        zs = _proj_t(h1, wt_zs, 0, GDN_WIDTH + V7X_LANES, act=None, out_dtype=F32, tm=1024,
                     tn=GDN_WIDTH + V7X_LANES, name="z_beta_decay_proj")
        gates = _proj_t(h1, wt, o_qkv_b, 2 * D_MODEL, act="sigmoid", out_dtype=BF16, tm=1024, tn=1024, name="gate_proj")
        qkv_b_t = _moba_proj(h1, wt, o_z, positions, inv_freq, batch, seq)

        gdn_out = _gdn(qkv_a, zs, _pad_lanes(gdn_a_log[l], GDN_HEADS), _pad_lanes(gdn_dt_bias[l], GDN_HEADS),
                       gdn_norm[l].reshape(1, HEAD_DIM), batch, seq)
        attn = _moba(qkv_b_t, batch, seq)
        x1, h2 = _mix(gdn_out, attn, gates, w_branch_a[l].astype(BF16), w_branch_b[l].astype(BF16),
                      w_out[l].astype(BF16), x2d, ln2[l])
        act = _upproj(h2, w_up[l], ffn_conv[l], ffn_conv_bias[l].reshape(1, 2 * D_FF), seq)
        if l == depth - 1:
            return _downproj_norm(act, w_down[l].astype(BF16), x1, final_norm).reshape(batch, seq, d)
        x2d = _downproj(act, w_down[l].astype(BF16), x1)
    return _final_norm(x2d, final_norm).reshape(batch, seq, d)
```

```python
import functools
import math

import jax
import jax.numpy as jnp
import numpy as np
from jax import lax
from jax.experimental import pallas as pl
from jax.experimental.pallas import tpu as pltpu

F32 = jnp.float32
BF16 = jnp.bfloat16

D_MODEL = 2048
GDN_HEADS = 8
HEAD_DIM = 128
GDN_WIDTH = GDN_HEADS * HEAD_DIM
GDN_CONV = 4
GDN_CHUNK = 64
MOBA_HEADS = 8
MOBA_WIDTH = MOBA_HEADS * HEAD_DIM
MOBA_BLOCK = 256
MOBA_TOPK = 3
ROPE_THETA = 500000.0
ROPE_DIM = HEAD_DIM // 4
ROPE_HALF = ROPE_DIM // 2
D_FF = 5632
FFN_CONV = 3
NORM_EPS = 1e-6
L2_EPS = 1e-6
NEG_INF = -1e30
NEG_LOG2_E = -math.log2(math.e)

V7X_LANES = 128
V7X_SUBLANES = 8
V7X_VMEM_LIMIT_BYTES = 56 * 1024 * 1024

GDN_GROUP = 4 * GDN_CHUNK
MOBA_Q_SCALE = HEAD_DIM ** -0.5 * math.log2(math.e)


def _cparams(*sem):
    return pltpu.CompilerParams(dimension_semantics=sem, vmem_limit_bytes=V7X_VMEM_LIMIT_BYTES)


def _dot(a, b):
    return jnp.dot(a, b, preferred_element_type=F32)


def _dot_nt(a, b):
    return lax.dot_general(a, b, (((1,), (1,)), ((), ())), preferred_element_type=F32)


def _dot_hi(a, b):
    return jnp.dot(a, b, preferred_element_type=F32, precision=lax.Precision.HIGHEST)


def _split3(x):
    hi = x.astype(BF16)
    r1 = x - hi.astype(F32)
    mid = r1.astype(BF16)
    lo = (r1 - mid.astype(F32)).astype(BF16)
    return hi, mid, lo


def _sigmoid(x):
    return 1.0 / (1.0 + jnp.exp2(x * NEG_LOG2_E))


def _silu(x):
    return x * _sigmoid(x)


def _shift_rows(cur, prev8, s):
    if s == 0:
        return cur
    rolled = pltpu.roll(cur, s, axis=0)
    rolled_prev = pltpu.roll(prev8, s, axis=0)
    row = lax.broadcasted_iota(jnp.int32, prev8.shape, 0)
    first = jnp.where(row < s, rolled_prev, rolled[0:V7X_SUBLANES])
    return jnp.concatenate([first, rolled[V7X_SUBLANES:]], axis=0)


def _causal_conv(cur, prev8, cw, width):
    y = cw[width - 1:width, :] * cur
    for j in range(width - 1):
        y = y + cw[j:j + 1, :] * _shift_rows(cur, prev8, width - 1 - j)
    return y


def _norm_kernel(x_ref, w_ref, h_ref):
    x = x_ref[...]
    y = x * lax.rsqrt(jnp.mean(x * x, axis=-1, keepdims=True) + NORM_EPS) * w_ref[...]
    h_ref[...] = y.astype(h_ref.dtype)


def _norm_cast(x2d, w, tm=512):
    m, d = x2d.shape
    return pl.pallas_call(
        _norm_kernel,
        grid=(m // tm,),
        in_specs=[pl.BlockSpec((tm, d), lambda i: (i, 0)),
                  pl.BlockSpec((1, d), lambda i: (0, 0))],
        out_specs=pl.BlockSpec((tm, d), lambda i: (i, 0)),
        out_shape=jax.ShapeDtypeStruct((m, d), BF16),
        compiler_params=_cparams("arbitrary"),
        name="norm1",
    )(x2d, w.reshape(1, d))


def _final_norm_kernel(x_ref, w_ref, o_ref):
    x = x_ref[...]
    o_ref[...] = x * lax.rsqrt(jnp.mean(x * x, axis=-1, keepdims=True) + NORM_EPS) * w_ref[...]


def _final_norm(x2d, w, tm=512):
    m, d = x2d.shape
    return pl.pallas_call(
        _final_norm_kernel,
        grid=(m // tm,),
        in_specs=[pl.BlockSpec((tm, d), lambda i: (i, 0)),
                  pl.BlockSpec((1, d), lambda i: (0, 0))],
        out_specs=pl.BlockSpec((tm, d), lambda i: (i, 0)),
        out_shape=jax.ShapeDtypeStruct((m, d), F32),
        compiler_params=_cparams("arbitrary"),
        name="final_norm",
    )(x2d, w.reshape(1, d))


def _proj_t_kernel(h_ref, wt_ref, o_ref, w16_ref, *, act):
    @pl.when(pl.program_id(1) == 0)
    def _():
        w16_ref[...] = wt_ref[...].T.astype(BF16)

    acc = _dot(h_ref[...], w16_ref[...])
    if act == "sigmoid":
        acc = _sigmoid(acc)
    o_ref[...] = acc.astype(o_ref.dtype)


def _proj_t(h, wt, row0, n, *, act, out_dtype, tm, tn, name):
    m, k = h.shape
    return pl.pallas_call(
        functools.partial(_proj_t_kernel, act=act),
        grid=(n // tn, m // tm),
        in_specs=[pl.BlockSpec((tm, k), lambda j, i: (i, 0)),
                  pl.BlockSpec((pl.Element(tn), pl.Element(k)), lambda j, i: (pl.multiple_of(row0 + j * tn, 16), 0))],
        out_specs=pl.BlockSpec((tm, tn), lambda j, i: (i, j)),
        out_shape=jax.ShapeDtypeStruct((m, n), out_dtype),
        scratch_shapes=[pltpu.VMEM((k, tn), BF16)],
        compiler_params=_cparams("arbitrary", "arbitrary"),
        name=name,
    )(h, wt)


def _gdn_proj_kernel(h_ref, w_ref, cw_ref, o_ref, w16_ref, carry_ref, *, tiles_per_seq, tiles_per_part, sub):
    n = pl.program_id(0)
    m = pl.program_id(1)
    tn = w_ref.shape[0]

    @pl.when(m == 0)
    def _():
        w16_ref[...] = w_ref[...].T.astype(BF16)

    @pl.when(m % tiles_per_seq == 0)
    def _():
        carry_ref[...] = jnp.zeros_like(carry_ref)

    part = n // tiles_per_part
    q_scale = jnp.where(part == 0, HEAD_DIM ** -0.5, 1.0).astype(F32)
    prev = carry_ref[...]
    for r in range(h_ref.shape[0] // sub):
        acc = _dot(h_ref[r * sub:(r + 1) * sub, :], w16_ref[...])
        y = _silu(_causal_conv(acc, prev, cw_ref[...], GDN_CONV))
        prev = acc[sub - V7X_SUBLANES:sub, :]
        heads = []
        for hd in range(tn // HEAD_DIM):
            blk = y[:, hd * HEAD_DIM:(hd + 1) * HEAD_DIM]
            inv = lax.rsqrt(jnp.sum(blk * blk, axis=-1, keepdims=True) + L2_EPS) * q_scale
            heads.append(blk * jnp.where(part < 2, inv, 1.0))
        o_ref[r * sub:(r + 1) * sub, :] = jnp.concatenate(heads, axis=1)
    carry_ref[...] = prev


def _gdn_proj(h, w_in, conv_w, seq, tm=1024, tn=1024, sub=1024):
    m, k = h.shape
    n = 3 * GDN_WIDTH
    return pl.pallas_call(
        functools.partial(_gdn_proj_kernel, tiles_per_seq=seq // tm, tiles_per_part=GDN_WIDTH // tn, sub=sub),
        grid=(n // tn, m // tm),
        in_specs=[pl.BlockSpec((tm, k), lambda j, i: (i, 0)),
                  pl.BlockSpec((tn, k), lambda j, i: (j, 0)),
                  pl.BlockSpec((GDN_CONV, tn), lambda j, i: (0, j))],
        out_specs=pl.BlockSpec((tm, tn), lambda j, i: (i, j)),
        out_shape=jax.ShapeDtypeStruct((m, n), F32),
        scratch_shapes=[pltpu.VMEM((k, tn), BF16),
                        pltpu.VMEM((V7X_SUBLANES, tn), F32)],
        compiler_params=_cparams("arbitrary", "arbitrary"),
        name="gdn_qkv_proj",
    )(h, w_in, conv_w)


def _moba_proj_kernel(h_ref, wt_ref, pos_ref, freq_ref, o_ref, w16_ref):
    part = pl.program_id(0)

    @pl.when(pl.program_id(1) == 0)
    def _():
        w16_ref[...] = wt_ref[...].astype(BF16)

    scale = jnp.where(part == 0, MOBA_Q_SCALE, 1.0).astype(F32)
    acc = _dot_nt(w16_ref[...], h_ref[...]) * scale
    ang = freq_ref[...] * pos_ref[...].astype(F32)
    roped = part < 2
    cos = jnp.where(roped, jnp.cos(ang), 1.0)
    sin = jnp.where(roped, jnp.sin(ang), 0.0)
    rows = []
    for hd in range(acc.shape[0] // HEAD_DIM):
        base = hd * HEAD_DIM
        x1 = acc[base:base + ROPE_HALF]
        x2 = acc[base + ROPE_HALF:base + ROPE_DIM]
        rows.append(x1 * cos - x2 * sin)
        rows.append(x2 * cos + x1 * sin)
        rows.append(acc[base + ROPE_DIM:base + HEAD_DIM])
    o_ref[...] = jnp.concatenate(rows, axis=0).astype(o_ref.dtype)


def _moba_proj(h, wt, row0, positions, inv_freq, batch, seq, tt=512):
    m, k = h.shape
    tr = MOBA_WIDTH
    n = 3 * MOBA_WIDTH
    per_seq = seq // tt
    return pl.pallas_call(
        _moba_proj_kernel,
        grid=(n // tr, m // tt),
        in_specs=[pl.BlockSpec((tt, k), lambda r, i: (i, 0)),
                  pl.BlockSpec((pl.Element(tr), pl.Element(k)), lambda r, i: (pl.multiple_of(row0 + r * tr, 16), 0)),
                  pl.BlockSpec((None, 1, tt), lambda r, i: (i // per_seq, 0, i % per_seq)),
                  pl.BlockSpec((ROPE_HALF, 1), lambda r, i: (0, 0))],
        out_specs=pl.BlockSpec((None, tr, tt), lambda r, i: (i // per_seq, r, i % per_seq)),
        out_shape=jax.ShapeDtypeStruct((batch, n, seq), BF16),
        scratch_shapes=[pltpu.VMEM((tr, k), BF16)],
        compiler_params=_cparams("arbitrary", "arbitrary"),
        name="moba_qkv_proj",
    )(h, wt, positions.reshape(batch, 1, seq), inv_freq)


def _gdn_kernel(q_ref, k_ref, v_ref, z_ref, sm_ref, a_ref, dt_ref, nw_ref, o_ref,
                state_ref, lbd_ref, subd_ref, *, groups, heads_per_step):
    t = pl.program_id(2)
    g = GDN_GROUP
    c = GDN_CHUNK

    @pl.when(t == 0)
    def _():
        state_ref[...] = jnp.zeros_like(state_ref)

    ri = lax.broadcasted_iota(jnp.int32, (g, g), 0)
    ci = lax.broadcasted_iota(jnp.int32, (g, g), 1)
    shift = int(math.log2(c))
    same = jnp.right_shift(ri, shift) == jnp.right_shift(ci, shift)
    lbd_ref[...] = jnp.where(same & (ci <= ri), 1.0, 0.0)
    subd_ref[...] = jnp.where(same & (ri > ci), 1.0, 0.0)
    lane = lax.broadcasted_iota(jnp.int32, (g, V7X_LANES), 1)
    eye = jnp.where(ri == ci, 1.0, 0.0)

    def group_body(gi, carry):
        hs = range(heads_per_step)
        nch = g // c
        r0 = pl.multiple_of(gi * g, g)
        sm = sm_ref[pl.ds(r0, g), :]
        beta_all = _sigmoid(sm)
        xs = sm + dt_ref[...]
        softplus = jnp.maximum(xs, 0.0) + jnp.log1p(jnp.exp(-jnp.abs(xs)))
        g_all = -jnp.exp(a_ref[...]) * softplus
        lbd = lbd_ref[...]
        lbd16 = lbd.astype(BF16)
        strict = subd_ref[...]
        cols = [slice(hh * HEAD_DIM, (hh + 1) * HEAD_DIM) for hh in hs]
        heads = [pl.program_id(1) * heads_per_step + hh for hh in hs]

        k = [k_ref[pl.ds(r0, g), cols[hh]] for hh in hs]
        k16 = [k[hh].astype(BF16) for hh in hs]
        beta_b = [jnp.broadcast_to(jnp.sum(jnp.where(lane == heads[hh], beta_all, 0.0), axis=-1, keepdims=True),
                                   (g, HEAD_DIM)) for hh in hs]
        g_b = [jnp.broadcast_to(jnp.sum(jnp.where(lane == GDN_HEADS + heads[hh], g_all, 0.0), axis=-1, keepdims=True),
                                (g, HEAD_DIM)) for hh in hs]
        kb = [k[hh] * beta_b[hh] for hh in hs]

        cs = [_dot(lbd16, jnp.concatenate(_split3(g_b[hh]), axis=1)) for hh in hs]
        kq = [_dot_nt(jnp.concatenate([kb[hh].astype(BF16), q_ref[pl.ds(r0, g), cols[hh]].astype(BF16)], axis=0), k16[hh])
              for hh in hs]
        kk = [kq[hh][:g] for hh in hs]
        qk_raw = [kq[hh][g:] for hh in hs]
        gc_b = [cs[hh][:, :HEAD_DIM] + cs[hh][:, HEAD_DIM:2 * HEAD_DIM] + cs[hh][:, 2 * HEAD_DIM:] for hh in hs]
        gamma = []
        for hh in hs:
            gc_row = gc_b[hh].T[0:1, :]
            dmat = jnp.where(lbd > 0.0, jnp.concatenate([gc_b[hh], gc_b[hh]], axis=1) - gc_row, 0.0)
            gamma.append(jnp.exp(dmat))
        eg = [jnp.exp(gc_b[hh]) for hh in hs]

        p = [-jnp.where(strict > 0.0, kk[hh] * gamma[hh], 0.0) for hh in hs]
        t_mat = [eye + p[hh] for hh in hs]
        p16 = [p[hh].astype(BF16) for hh in hs]
        p16 = [_dot(p16[hh], p16[hh]).astype(BF16) for hh in hs]
        for _ in range(4):
            st = [_dot(jnp.concatenate([p16[hh], t_mat[hh].astype(BF16)], axis=0), p16[hh]) for hh in hs]
            p16 = [st[hh][:g].astype(BF16) for hh in hs]
            t_mat = [t_mat[hh] + st[hh][g:] for hh in hs]
        tp = [_dot(t_mat[hh].astype(BF16), p16[hh]) for hh in hs]
        t_mat = [t_mat[hh] + tp[hh] for hh in hs]
        uw = [_dot(t_mat[hh].astype(BF16),
                   jnp.concatenate([v_ref[pl.ds(r0, g), cols[hh]] * beta_b[hh], kb[hh] * eg[hh]], axis=1).astype(BF16))
              for hh in hs]
        u = [uw[hh][:, :HEAD_DIM] for hh in hs]
        w16 = [uw[hh][:, HEAD_DIM:].astype(BF16) for hh in hs]
        qk16 = [jnp.where(lbd > 0.0, qk_raw[hh] * gamma[hh], 0.0).astype(BF16) for hh in hs]
        qg = [(q_ref[pl.ds(r0, g), cols[hh]] * eg[hh]).astype(BF16) for hh in hs]
        kd_t = []
        for hh in hs:
            gc_last = jnp.concatenate(
                [jnp.broadcast_to(gc_b[hh][(ch + 1) * c - 1:(ch + 1) * c, :], (c, HEAD_DIM)) for ch in range(nch)],
                axis=0)
            kd_t.append((k[hh] * jnp.exp(gc_last - gc_b[hh])).T.astype(BF16))

        state = [state_ref[hh] for hh in hs]
        outs = [[] for _ in hs]
        zeros_c = jnp.zeros((c, HEAD_DIM), F32)
        for ch in range(nch):
            rows = slice(ch * c, (ch + 1) * c)
            s16 = [state[hh].astype(BF16) for hh in hs]
            wq = [_dot(jnp.concatenate([w16[hh][rows], qg[hh][rows]], axis=0), s16[hh]) for hh in hs]
            ws = [wq[hh][:c] for hh in hs]
            qs = [wq[hh][c:] for hh in hs]
            vn_all = [jnp.concatenate([zeros_c] * ch + [u[hh][rows] - ws[hh]] + [zeros_c] * (nch - 1 - ch),
                                      axis=0).astype(BF16) for hh in hs]
            so = [_dot(jnp.concatenate([kd_t[hh], qk16[hh][rows]], axis=0), vn_all[hh]) for hh in hs]
            sv = [so[hh][:HEAD_DIM] for hh in hs]
            ov = [so[hh][HEAD_DIM:] for hh in hs]
            for hh in hs:
                outs[hh].append(qs[hh] + ov[hh])
                state[hh] = state[hh] * eg[hh][(ch + 1) * c - 1:(ch + 1) * c, :] + sv[hh]
        for hh in hs:
            state_ref[hh] = state[hh]
            o = jnp.concatenate(outs[hh], axis=0)
            o = o * lax.rsqrt(jnp.mean(o * o, axis=-1, keepdims=True) + NORM_EPS) * nw_ref[...]
            o_ref[pl.ds(r0, g), cols[hh]] = (o * _silu(z_ref[pl.ds(r0, g), cols[hh]])).astype(o_ref.dtype)
        return carry

    lax.fori_loop(0, groups, group_body, 0)


def _gdn(qkv, zs, a_pad, dt_pad, norm_w, batch, seq, ts=512, heads_per_step=8):
    m = qkv.shape[0]
    tiles = seq // ts
    hp = heads_per_step
    hb = GDN_HEADS // hp
    wd = hp * HEAD_DIM
    row = lambda b, h, t: b * tiles + t
    return pl.pallas_call(
        functools.partial(_gdn_kernel, groups=ts // GDN_GROUP, heads_per_step=hp),
        grid=(batch, hb, tiles),
        in_specs=[pl.BlockSpec((ts, wd), lambda b, h, t: (row(b, h, t), h)),
                  pl.BlockSpec((ts, wd), lambda b, h, t: (row(b, h, t), hb + h)),
                  pl.BlockSpec((ts, wd), lambda b, h, t: (row(b, h, t), 2 * hb + h)),
                  pl.BlockSpec((ts, wd), lambda b, h, t: (row(b, h, t), h)),
                  pl.BlockSpec((ts, V7X_LANES), lambda b, h, t: (row(b, h, t), GDN_WIDTH // V7X_LANES)),
                  pl.BlockSpec((1, V7X_LANES), lambda b, h, t: (0, 0)),
                  pl.BlockSpec((1, V7X_LANES), lambda b, h, t: (0, 0)),
                  pl.BlockSpec((1, HEAD_DIM), lambda b, h, t: (0, 0))],
        out_specs=pl.BlockSpec((ts, wd), lambda b, h, t: (row(b, h, t), h)),
        out_shape=jax.ShapeDtypeStruct((m, GDN_WIDTH), BF16),
        scratch_shapes=[pltpu.VMEM((hp, HEAD_DIM, HEAD_DIM), F32),
                        pltpu.VMEM((GDN_GROUP, GDN_GROUP), F32),
                        pltpu.VMEM((GDN_GROUP, GDN_GROUP), F32)],
        compiler_params=_cparams("arbitrary", "arbitrary", "arbitrary"),
        name="gated_delta_rule",
    )(qkv, qkv, qkv, zs, zs, a_pad, dt_pad, norm_w)


def _moba_kernel(q_ref, k_ref, v_ref, o_ref, kn_ref, km_ref, s_ref, p_ref, bias_ref, *, nblk):
    blk = MOBA_BLOCK
    for j in range(nblk):
        kt = k_ref[:, j * blk:(j + 1) * blk].astype(F32)
        kn = kt.T
        kn_ref[j * blk:(j + 1) * blk, 0:HEAD_DIM] = kn.astype(BF16)
        kn_ref[j * blk:(j + 1) * blk, HEAD_DIM:] = jnp.where(
            lax.broadcasted_iota(jnp.int32, (blk, HEAD_DIM), 1) == j, 1.0, 0.0).astype(BF16)
        km_ref[j:j + 1, :] = jnp.sum(kn, axis=0, keepdims=True) * (1.0 / blk)

    bias_ref[...] = jnp.zeros_like(bias_ref)

    key_i = lax.broadcasted_iota(jnp.int32, (blk, blk), 0)
    qry_i = lax.broadcasted_iota(jnp.int32, (blk, blk), 1)
    causal_bias = jnp.where(key_i <= qry_i, 0.0, NEG_INF)
    groups = blk // V7X_SUBLANES

    def scores(i):
        s_buf = s_ref.at[i % 2]
        qt = q_ref[:, i * blk:(i + 1) * blk]
        if i > MOBA_TOPK:
            gate = _dot_hi(km_ref[...], qt.astype(F32))
            n_iota = lax.broadcasted_iota(jnp.int32, (nblk, blk), 0)
            valid = n_iota < i
            for n in range(i):
                g_n = gate[n:n + 1, :]
                beats = valid & ((gate > g_n) | ((gate == g_n) & (n_iota < n)))
                cnt = jnp.sum(jnp.where(beats, 1.0, 0.0), axis=0, keepdims=True)
                bias_ref[n:n + 1, :] = jnp.where(cnt < float(MOBA_TOPK), 0.0, NEG_INF)
            bias_ref[i:i + 1, :] = jnp.zeros((1, blk), F32)
            qa = jnp.concatenate([qt, bias_ref[...].astype(BF16),
                                  jnp.zeros((HEAD_DIM - nblk, blk), BF16)], axis=0)
        m_acc = None
        for j in range(i + 1):
            if i > MOBA_TOPK:
                s = _dot(kn_ref[j * blk:(j + 1) * blk, :], qa)
            else:
                s = _dot(kn_ref[j * blk:(j + 1) * blk, 0:HEAD_DIM], qt)
            if j == i:
                s = s + causal_bias
            s_buf[j * blk:(j + 1) * blk, :] = s
            part = jnp.max(s.reshape(groups, V7X_SUBLANES, blk), axis=0)
            m_acc = part if m_acc is None else jnp.maximum(m_acc, part)
        return jnp.max(m_acc, axis=0, keepdims=True)

    def finish(i, m_row):
        s_buf = s_ref.at[i % 2]
        l_acc = None
        for j in range(i + 1):
            p = jnp.exp2(s_buf[j * blk:(j + 1) * blk, :] - m_row)
            part = jnp.sum(p.reshape(groups, V7X_SUBLANES, blk), axis=0)
            l_acc = part if l_acc is None else l_acc + part
            p_ref[j * blk:(j + 1) * blk, :] = p.astype(BF16)
        l_row = jnp.sum(l_acc, axis=0, keepdims=True)
        o_t = _dot(v_ref[:, 0:(i + 1) * blk], p_ref[0:(i + 1) * blk, :]) / l_row
        o_ref[i * blk:(i + 1) * blk, :] = o_t.T.astype(o_ref.dtype)

    m_next = scores(0)
    for i in range(nblk):
        m_row = m_next
        if i + 1 < nblk:
            m_next = scores(i + 1)
        finish(i, m_row)


def _moba(qkv_t, batch, seq):
    nblk = seq // MOBA_BLOCK
    hb = MOBA_HEADS
    return pl.pallas_call(
        functools.partial(_moba_kernel, nblk=nblk),
        grid=(batch, MOBA_HEADS),
        in_specs=[pl.BlockSpec((None, HEAD_DIM, seq), lambda b, h: (b, h, 0)),
                  pl.BlockSpec((None, HEAD_DIM, seq), lambda b, h: (b, hb + h, 0)),
                  pl.BlockSpec((None, HEAD_DIM, seq), lambda b, h: (b, 2 * hb + h, 0))],
        out_specs=pl.BlockSpec((seq, HEAD_DIM), lambda b, h: (b, h)),
        out_shape=jax.ShapeDtypeStruct((batch * seq, MOBA_WIDTH), BF16),
        scratch_shapes=[pltpu.VMEM((seq, 2 * HEAD_DIM), BF16),
                        pltpu.VMEM((nblk, HEAD_DIM), F32),
                        pltpu.VMEM((2, seq, MOBA_BLOCK), F32),
                        pltpu.VMEM((seq, MOBA_BLOCK), BF16),
                        pltpu.VMEM((nblk, MOBA_BLOCK), F32)],
        compiler_params=_cparams("arbitrary", "arbitrary"),
        name="moba_attention",
    )(qkv_t, qkv_t, qkv_t)


def _mix_kernel(a_ref, b_ref, ga_ref, gb_ref, wa_ref, wb_ref, wo_ref, x_ref, nw_ref, x1_ref, h_ref, *, sub):
    for r in range(a_ref.shape[0] // sub):
        rows = slice(r * sub, (r + 1) * sub)
        ya = _dot(a_ref[rows, :], wa_ref[...])
        yb = _dot(b_ref[rows, :], wb_ref[...])
        merged = (ga_ref[rows, :] * ya + gb_ref[rows, :] * yb).astype(BF16)
        x1 = x_ref[rows, :] + _dot(merged, wo_ref[...])
        x1_ref[rows, :] = x1
        h = x1 * lax.rsqrt(jnp.mean(x1 * x1, axis=-1, keepdims=True) + NORM_EPS) * nw_ref[...]
        h_ref[rows, :] = h.astype(h_ref.dtype)


def _mix(ya_in, yb_in, gates, wa, wb, wo, x2d, nw, tm=512, sub=512):
    m, k = ya_in.shape
    n = wo.shape[1]
    return pl.pallas_call(
        functools.partial(_mix_kernel, sub=sub),
        grid=(m // tm,),
        in_specs=[pl.BlockSpec((tm, k), lambda i: (i, 0)),
                  pl.BlockSpec((tm, k), lambda i: (i, 0)),
                  pl.BlockSpec((tm, n), lambda i: (i, 0)),
                  pl.BlockSpec((tm, n), lambda i: (i, 1)),
                  pl.BlockSpec((k, n), lambda i: (0, 0)),
                  pl.BlockSpec((k, n), lambda i: (0, 0)),
                  pl.BlockSpec((n, n), lambda i: (0, 0)),
                  pl.BlockSpec((tm, n), lambda i: (i, 0)),
                  pl.BlockSpec((1, n), lambda i: (0, 0))],
        out_specs=[pl.BlockSpec((tm, n), lambda i: (i, 0)),
                   pl.BlockSpec((tm, n), lambda i: (i, 0))],
        out_shape=[jax.ShapeDtypeStruct((m, n), F32),
                   jax.ShapeDtypeStruct((m, n), BF16)],
        compiler_params=_cparams("arbitrary"),
        name="branch_mix_out_proj_norm2",
    )(ya_in, yb_in, gates, gates, wa, wb, wo, x2d, nw.reshape(1, n))


def _upproj_kernel(h_ref, wg_ref, wv_ref, cg_ref, cv_ref, bg_ref, bv_ref, o_ref,
                   w_ref, carry_ref, *, tiles_per_seq, sub):
    m = pl.program_id(1)
    tn = wg_ref.shape[1]

    @pl.when(m == 0)
    def _():
        w_ref[:, :tn] = wg_ref[...].astype(BF16)
        w_ref[:, tn:] = wv_ref[...].astype(BF16)

    @pl.when(m % tiles_per_seq == 0)
    def _():
        carry_ref[...] = jnp.zeros_like(carry_ref)

    cw = jnp.concatenate([cg_ref[...], cv_ref[...]], axis=1)
    bias = jnp.concatenate([bg_ref[...], bv_ref[...]], axis=1)
    prev = carry_ref[...]
    for r in range(h_ref.shape[0] // sub):
        u = _dot(h_ref[r * sub:(r + 1) * sub, :], w_ref[...])
        y = _causal_conv(u, prev, cw, FFN_CONV) + bias
        prev = u[sub - V7X_SUBLANES:sub, :]
        o_ref[r * sub:(r + 1) * sub, :] = (_silu(y[:, :tn]) * y[:, tn:]).astype(o_ref.dtype)
    carry_ref[...] = prev


def _upproj(h, w_up, conv_w, conv_b, seq, tm=1024, tn=512, sub=1024):
    m, k = h.shape
    nb = D_FF // tn
    return pl.pallas_call(
        functools.partial(_upproj_kernel, tiles_per_seq=seq // tm, sub=sub),
        grid=(nb, m // tm),
        in_specs=[pl.BlockSpec((tm, k), lambda j, i: (i, 0)),
                  pl.BlockSpec((k, tn), lambda j, i: (0, j)),
                  pl.BlockSpec((k, tn), lambda j, i: (0, nb + j)),
                  pl.BlockSpec((FFN_CONV, tn), lambda j, i: (0, j)),
                  pl.BlockSpec((FFN_CONV, tn), lambda j, i: (0, nb + j)),
                  pl.BlockSpec((1, tn), lambda j, i: (0, j)),
                  pl.BlockSpec((1, tn), lambda j, i: (0, nb + j))],
        out_specs=pl.BlockSpec((tm, tn), lambda j, i: (i, j)),
        out_shape=jax.ShapeDtypeStruct((m, D_FF), BF16),
        scratch_shapes=[pltpu.VMEM((k, 2 * tn), BF16),
                        pltpu.VMEM((V7X_SUBLANES, 2 * tn), F32)],
        compiler_params=_cparams("arbitrary", "arbitrary"),
        name="up_proj_conv_gate",
    )(h, w_up, w_up, conv_w, conv_w, conv_b, conv_b)


def _downproj_kernel(a_ref, w_ref, x_ref, o_ref):
    o_ref[...] = x_ref[...] + _dot(a_ref[...], w_ref[...])


def _downproj(a, w, x1, tm=512, tn=1024):
    m, k = a.shape
    n = w.shape[1]
    return pl.pallas_call(
        _downproj_kernel,
        grid=(n // tn, m // tm),
        in_specs=[pl.BlockSpec((tm, k), lambda j, i: (i, 0)),
                  pl.BlockSpec((k, tn), lambda j, i: (0, j)),
                  pl.BlockSpec((tm, tn), lambda j, i: (i, j))],
        out_specs=pl.BlockSpec((tm, tn), lambda j, i: (i, j)),
        out_shape=jax.ShapeDtypeStruct((m, n), F32),
        compiler_params=_cparams("arbitrary", "arbitrary"),
        name="down_proj",
    )(a, w, x1)


def _downproj_norm_kernel(a_ref, w_ref, x_ref, nw_ref, o_ref):
    x2 = x_ref[...] + _dot(a_ref[...], w_ref[...])
    o_ref[...] = x2 * lax.rsqrt(jnp.mean(x2 * x2, axis=-1, keepdims=True) + NORM_EPS) * nw_ref[...]


def _downproj_norm(a, w, x1, nw, tm=512):
    m, k = a.shape
    n = w.shape[1]
    return pl.pallas_call(
        _downproj_norm_kernel,
        grid=(m // tm,),
        in_specs=[pl.BlockSpec((tm, k), lambda i: (i, 0)),
                  pl.BlockSpec((k, n), lambda i: (0, 0)),
                  pl.BlockSpec((tm, n), lambda i: (i, 0)),
                  pl.BlockSpec((1, n), lambda i: (0, 0))],
        out_specs=pl.BlockSpec((tm, n), lambda i: (i, 0)),
        out_shape=jax.ShapeDtypeStruct((m, n), F32),
        compiler_params=_cparams("arbitrary"),
        name="down_proj_final_norm",
    )(a, w, x1, nw.reshape(1, n))


def _pad_lanes(vec, offset):
    return jnp.zeros((1, V7X_LANES), F32).at[0, offset:offset + vec.shape[0]].set(vec.astype(F32))


def kernel(x, positions, ln1, w_in, gdn_conv, gdn_a_log, gdn_dt_bias, gdn_norm, w_branch_a, w_branch_b,
           w_out, ln2, w_up, ffn_conv, ffn_conv_bias, w_down, final_norm):
    batch, seq, d = x.shape
    m = batch * seq
    depth = ln1.shape[0]
    o_qkv_a = 3 * GDN_WIDTH
    o_small = o_qkv_a + 2 * GDN_HEADS
    o_z = o_small + GDN_WIDTH
    o_qkv_b = o_z + 3 * MOBA_WIDTH
    inv_freq = (ROPE_THETA ** (-jnp.arange(ROPE_HALF, dtype=F32) / ROPE_HALF)).reshape(ROPE_HALF, 1)

    x2d = x.reshape(m, d)
    for l in range(depth):
        wt = jnp.swapaxes(w_in[l], 0, 1)
        wt_zs = jnp.concatenate(
            [wt[o_small:o_z], jnp.pad(wt[o_qkv_a:o_small], ((0, V7X_LANES - 2 * GDN_HEADS), (0, 0)))], axis=0)

        h1 = _norm_cast(x2d, ln1[l])
        qkv_a = _gdn_proj(h1, wt, gdn_conv[l], seq)
        zs = _proj_t(h1, wt_zs, 0, GDN_WIDTH + V7X_LANES, act=None, out_dtype=F32, tm=1024,
                     tn=GDN_WIDTH + V7X_LANES, name="z_beta_decay_proj")
        gates = _proj_t(h1, wt, o_qkv_b, 2 * D_MODEL, act="sigmoid", out_dtype=BF16, tm=1024, tn=1024, name="gate_proj")
        qkv_b_t = _moba_proj(h1, wt, o_z, positions, inv_freq, batch, seq)

        gdn_out = _gdn(qkv_a, zs, _pad_lanes(gdn_a_log[l], GDN_HEADS), _pad_lanes(gdn_dt_bias[l], GDN_HEADS),
                       gdn_norm[l].reshape(1, HEAD_DIM), batch, seq)
        attn = _moba(qkv_b_t, batch, seq)
        x1, h2 = _mix(gdn_out, attn, gates, w_branch_a[l].astype(BF16), w_branch_b[l].astype(BF16),
                      w_out[l].astype(BF16), x2d, ln2[l])
        act = _upproj(h2, w_up[l], ffn_conv[l], ffn_conv_bias[l].reshape(1, 2 * D_FF), seq)
        if l == depth - 1:
            return _downproj_norm(act, w_down[l].astype(BF16), x1, final_norm).reshape(batch, seq, d)
        x2d = _downproj(act, w_down[l].astype(BF16), x1)
    return _final_norm(x2d, final_norm).reshape(batch, seq, d)
```

```python
import functools
import math

import jax
import jax.numpy as jnp
import numpy as np
from jax import lax
from jax.experimental import pallas as pl
from jax.experimental.pallas import tpu as pltpu

F32 = jnp.float32
BF16 = jnp.bfloat16

D_MODEL = 2048
GDN_HEADS = 8
HEAD_DIM = 128
GDN_WIDTH = GDN_HEADS * HEAD_DIM
GDN_CONV = 4
GDN_CHUNK = 64
MOBA_HEADS = 8
MOBA_WIDTH = MOBA_HEADS * HEAD_DIM
MOBA_BLOCK = 256
MOBA_TOPK = 3
ROPE_THETA = 500000.0
ROPE_DIM = HEAD_DIM // 4
ROPE_HALF = ROPE_DIM // 2
D_FF = 5632
FFN_CONV = 3
NORM_EPS = 1e-6
L2_EPS = 1e-6
NEG_INF = -1e30
NEG_LOG2_E = -math.log2(math.e)

V7X_LANES = 128
V7X_SUBLANES = 8
V7X_VMEM_LIMIT_BYTES = 56 * 1024 * 1024

GDN_GROUP = 4 * GDN_CHUNK
MOBA_Q_SCALE = HEAD_DIM ** -0.5 * math.log2(math.e)


def _cparams(*sem):
    return pltpu.CompilerParams(dimension_semantics=sem, vmem_limit_bytes=V7X_VMEM_LIMIT_BYTES)


def _dot(a, b):
    return jnp.dot(a, b, preferred_element_type=F32)


def _dot_nt(a, b):
    return lax.dot_general(a, b, (((1,), (1,)), ((), ())), preferred_element_type=F32)


def _dot_hi(a, b):
    return jnp.dot(a, b, preferred_element_type=F32, precision=lax.Precision.HIGHEST)


def _split3(x):
    hi = x.astype(BF16)
    r1 = x - hi.astype(F32)
    mid = r1.astype(BF16)
    lo = (r1 - mid.astype(F32)).astype(BF16)
    return hi, mid, lo


def _sigmoid(x):
    return 1.0 / (1.0 + jnp.exp2(x * NEG_LOG2_E))


def _silu(x):
    return x * _sigmoid(x)


def _shift_rows(cur, prev8, s):
    if s == 0:
        return cur
    rolled = pltpu.roll(cur, s, axis=0)
    rolled_prev = pltpu.roll(prev8, s, axis=0)
    row = lax.broadcasted_iota(jnp.int32, prev8.shape, 0)
    first = jnp.where(row < s, rolled_prev, rolled[0:V7X_SUBLANES])
    return jnp.concatenate([first, rolled[V7X_SUBLANES:]], axis=0)


def _causal_conv(cur, prev8, cw, width):
    y = cw[width - 1:width, :] * cur
    for j in range(width - 1):
        y = y + cw[j:j + 1, :] * _shift_rows(cur, prev8, width - 1 - j)
    return y


def _norm_kernel(x_ref, w_ref, h_ref):
    x = x_ref[...]
    y = x * lax.rsqrt(jnp.mean(x * x, axis=-1, keepdims=True) + NORM_EPS) * w_ref[...]
    h_ref[...] = y.astype(h_ref.dtype)


def _norm_cast(x2d, w, tm=512):
    m, d = x2d.shape
    return pl.pallas_call(
        _norm_kernel,
        grid=(m // tm,),
        in_specs=[pl.BlockSpec((tm, d), lambda i: (i, 0)),
                  pl.BlockSpec((1, d), lambda i: (0, 0))],
        out_specs=pl.BlockSpec((tm, d), lambda i: (i, 0)),
        out_shape=jax.ShapeDtypeStruct((m, d), BF16),
        compiler_params=_cparams("arbitrary"),
        name="norm1",
    )(x2d, w.reshape(1, d))


def _final_norm_kernel(x_ref, w_ref, o_ref):
    x = x_ref[...]
    o_ref[...] = x * lax.rsqrt(jnp.mean(x * x, axis=-1, keepdims=True) + NORM_EPS) * w_ref[...]


def _final_norm(x2d, w, tm=512):
    m, d = x2d.shape
    return pl.pallas_call(
        _final_norm_kernel,
        grid=(m // tm,),
        in_specs=[pl.BlockSpec((tm, d), lambda i: (i, 0)),
                  pl.BlockSpec((1, d), lambda i: (0, 0))],
        out_specs=pl.BlockSpec((tm, d), lambda i: (i, 0)),
        out_shape=jax.ShapeDtypeStruct((m, d), F32),
        compiler_params=_cparams("arbitrary"),
        name="final_norm",
    )(x2d, w.reshape(1, d))


def _proj_t_kernel(h_ref, wt_ref, o_ref, w16_ref, *, act):
    @pl.when(pl.program_id(1) == 0)
    def _():
        w16_ref[...] = wt_ref[...].T.astype(BF16)

    acc = _dot(h_ref[...], w16_ref[...])
    if act == "sigmoid":
        acc = _sigmoid(acc)
    o_ref[...] = acc.astype(o_ref.dtype)


def _proj_t(h, wt, row0, n, *, act, out_dtype, tm, tn, name):
    m, k = h.shape
    return pl.pallas_call(
        functools.partial(_proj_t_kernel, act=act),
        grid=(n // tn, m // tm),
        in_specs=[pl.BlockSpec((tm, k), lambda j, i: (i, 0)),
                  pl.BlockSpec((pl.Element(tn), pl.Element(k)), lambda j, i: (pl.multiple_of(row0 + j * tn, 16), 0))],
        out_specs=pl.BlockSpec((tm, tn), lambda j, i: (i, j)),
        out_shape=jax.ShapeDtypeStruct((m, n), out_dtype),
        scratch_shapes=[pltpu.VMEM((k, tn), BF16)],
        compiler_params=_cparams("arbitrary", "arbitrary"),
        name=name,
    )(h, wt)


def _gdn_proj_kernel(h_ref, w_ref, cw_ref, o_ref, w16_ref, carry_ref, *, tiles_per_seq, tiles_per_part, sub, l2norm):
    n = pl.program_id(0)
    m = pl.program_id(1)
    tn = w_ref.shape[0]

    @pl.when(m == 0)
    def _():
        w16_ref[...] = w_ref[...].T.astype(BF16)

    @pl.when(m % tiles_per_seq == 0)
    def _():
        carry_ref[...] = jnp.zeros_like(carry_ref)

    part = n // tiles_per_part
    q_scale = jnp.where(part == 0, HEAD_DIM ** -0.5, 1.0).astype(F32)
    prev = carry_ref[...]
    for r in range(h_ref.shape[0] // sub):
        acc = _dot(h_ref[r * sub:(r + 1) * sub, :], w16_ref[...])
        y = _silu(_causal_conv(acc, prev, cw_ref[...], GDN_CONV))
        prev = acc[sub - V7X_SUBLANES:sub, :]
        if l2norm:
            heads = []
            for hd in range(tn // HEAD_DIM):
                blk = y[:, hd * HEAD_DIM:(hd + 1) * HEAD_DIM]
                heads.append(blk * (lax.rsqrt(jnp.sum(blk * blk, axis=-1, keepdims=True) + L2_EPS) * q_scale))
            y = jnp.concatenate(heads, axis=1)
        o_ref[r * sub:(r + 1) * sub, :] = y
    carry_ref[...] = prev


def _gdn_proj(h, w_in, conv_w, seq, *, first, parts, l2norm, name, tm=1024, tn=1024, sub=1024):
    m, k = h.shape
    n = parts * GDN_WIDTH
    off = first * GDN_WIDTH // tn
    return pl.pallas_call(
        functools.partial(_gdn_proj_kernel, tiles_per_seq=seq // tm, tiles_per_part=GDN_WIDTH // tn, sub=sub,
                          l2norm=l2norm),
        grid=(n // tn, m // tm),
        in_specs=[pl.BlockSpec((tm, k), lambda j, i: (i, 0)),
                  pl.BlockSpec((tn, k), lambda j, i: (off + j, 0)),
                  pl.BlockSpec((GDN_CONV, tn), lambda j, i: (0, off + j))],
        out_specs=pl.BlockSpec((tm, tn), lambda j, i: (i, j)),
        out_shape=jax.ShapeDtypeStruct((m, n), F32),
        scratch_shapes=[pltpu.VMEM((k, tn), BF16),
                        pltpu.VMEM((V7X_SUBLANES, tn), F32)],
        compiler_params=_cparams("arbitrary", "arbitrary"),
        name=name,
    )(h, w_in, conv_w)


def _moba_proj_kernel(h_ref, wt_ref, pos_ref, freq_ref, o_ref, w16_ref):
    part = pl.program_id(0)

    @pl.when(pl.program_id(1) == 0)
    def _():
        w16_ref[...] = wt_ref[...].astype(BF16)

    scale = jnp.where(part == 0, MOBA_Q_SCALE, 1.0).astype(F32)
    acc = _dot_nt(w16_ref[...], h_ref[...]) * scale
    ang = freq_ref[...] * pos_ref[...].astype(F32)
    roped = part < 2
    cos = jnp.where(roped, jnp.cos(ang), 1.0)
    sin = jnp.where(roped, jnp.sin(ang), 0.0)
    rows = []
    for hd in range(acc.shape[0] // HEAD_DIM):
        base = hd * HEAD_DIM
        x1 = acc[base:base + ROPE_HALF]
        x2 = acc[base + ROPE_HALF:base + ROPE_DIM]
        rows.append(x1 * cos - x2 * sin)
        rows.append(x2 * cos + x1 * sin)
        rows.append(acc[base + ROPE_DIM:base + HEAD_DIM])
    o_ref[...] = jnp.concatenate(rows, axis=0).astype(o_ref.dtype)


def _moba_proj(h, wt, row0, positions, inv_freq, batch, seq, tt=512):
    m, k = h.shape
    tr = MOBA_WIDTH
    n = 3 * MOBA_WIDTH
    per_seq = seq // tt
    return pl.pallas_call(
        _moba_proj_kernel,
        grid=(n // tr, m // tt),
        in_specs=[pl.BlockSpec((tt, k), lambda r, i: (i, 0)),
                  pl.BlockSpec((pl.Element(tr), pl.Element(k)), lambda r, i: (pl.multiple_of(row0 + r * tr, 16), 0)),
                  pl.BlockSpec((None, 1, tt), lambda r, i: (i // per_seq, 0, i % per_seq)),
                  pl.BlockSpec((ROPE_HALF, 1), lambda r, i: (0, 0))],
        out_specs=pl.BlockSpec((None, tr, tt), lambda r, i: (i // per_seq, r, i % per_seq)),
        out_shape=jax.ShapeDtypeStruct((batch, n, seq), BF16),
        scratch_shapes=[pltpu.VMEM((tr, k), BF16)],
        compiler_params=_cparams("arbitrary", "arbitrary"),
        name="moba_qkv_proj",
    )(h, wt, positions.reshape(batch, 1, seq), inv_freq)


def _gdn_kernel(q_ref, k_ref, v_ref, z_ref, sm_ref, a_ref, dt_ref, nw_ref, o_ref,
                state_ref, lbd_ref, subd_ref, *, groups, heads_per_step):
    t = pl.program_id(2)
    g = GDN_GROUP
    c = GDN_CHUNK

    @pl.when(t == 0)
    def _():
        state_ref[...] = jnp.zeros_like(state_ref)

    ri = lax.broadcasted_iota(jnp.int32, (g, g), 0)
    ci = lax.broadcasted_iota(jnp.int32, (g, g), 1)
    shift = int(math.log2(c))
    same = jnp.right_shift(ri, shift) == jnp.right_shift(ci, shift)
    lbd_ref[...] = jnp.where(same & (ci <= ri), 1.0, 0.0)
    subd_ref[...] = jnp.where(same & (ri > ci), 1.0, 0.0)
    lane = lax.broadcasted_iota(jnp.int32, (g, V7X_LANES), 1)
    eye = jnp.where(ri == ci, 1.0, 0.0)

    def group_body(gi, carry):
        hs = range(heads_per_step)
        nch = g // c
        r0 = pl.multiple_of(gi * g, g)
        sm = sm_ref[pl.ds(r0, g), :]
        beta_all = _sigmoid(sm)
        xs = sm + dt_ref[...]
        softplus = jnp.maximum(xs, 0.0) + jnp.log1p(jnp.exp(-jnp.abs(xs)))
        g_all = -jnp.exp(a_ref[...]) * softplus
        lbd = lbd_ref[...]
        lbd16 = lbd.astype(BF16)
        strict = subd_ref[...]
        cols = [slice(hh * HEAD_DIM, (hh + 1) * HEAD_DIM) for hh in hs]
        heads = [pl.program_id(1) * heads_per_step + hh for hh in hs]

        k = [k_ref[pl.ds(r0, g), cols[hh]] for hh in hs]
        k16 = [k[hh].astype(BF16) for hh in hs]
        beta_b = [jnp.broadcast_to(jnp.sum(jnp.where(lane == heads[hh], beta_all, 0.0), axis=-1, keepdims=True),
                                   (g, HEAD_DIM)) for hh in hs]
        g_b = [jnp.broadcast_to(jnp.sum(jnp.where(lane == GDN_HEADS + heads[hh], g_all, 0.0), axis=-1, keepdims=True),
                                (g, HEAD_DIM)) for hh in hs]
        kb = [k[hh] * beta_b[hh] for hh in hs]

        cs = [_dot(lbd16, jnp.concatenate(_split3(g_b[hh]), axis=1)) for hh in hs]
        kq = [_dot_nt(jnp.concatenate([kb[hh].astype(BF16), q_ref[pl.ds(r0, g), cols[hh]].astype(BF16)], axis=0), k16[hh])
              for hh in hs]
        kk = [kq[hh][:g] for hh in hs]
        qk_raw = [kq[hh][g:] for hh in hs]
        gc_b = [cs[hh][:, :HEAD_DIM] + cs[hh][:, HEAD_DIM:2 * HEAD_DIM] + cs[hh][:, 2 * HEAD_DIM:] for hh in hs]
        gamma = []
        for hh in hs:
            gc_row = gc_b[hh].T[0:1, :]
            dmat = jnp.where(lbd > 0.0, jnp.concatenate([gc_b[hh], gc_b[hh]], axis=1) - gc_row, 0.0)
            gamma.append(jnp.exp(dmat))
        eg = [jnp.exp(gc_b[hh]) for hh in hs]

        p = [-jnp.where(strict > 0.0, kk[hh] * gamma[hh], 0.0) for hh in hs]
        t_mat = [eye + p[hh] for hh in hs]
        p16 = [p[hh].astype(BF16) for hh in hs]
        p16 = [_dot(p16[hh], p16[hh]).astype(BF16) for hh in hs]
        for _ in range(4):
            st = [_dot(jnp.concatenate([p16[hh], t_mat[hh].astype(BF16)], axis=0), p16[hh]) for hh in hs]
            p16 = [st[hh][:g].astype(BF16) for hh in hs]
            t_mat = [t_mat[hh] + st[hh][g:] for hh in hs]
        tp = [_dot(t_mat[hh].astype(BF16), p16[hh]) for hh in hs]
        t_mat = [t_mat[hh] + tp[hh] for hh in hs]
        uw = [_dot(t_mat[hh].astype(BF16),
                   jnp.concatenate([v_ref[pl.ds(r0, g), cols[hh]] * beta_b[hh], kb[hh] * eg[hh]], axis=1).astype(BF16))
              for hh in hs]
        u = [uw[hh][:, :HEAD_DIM] for hh in hs]
        w16 = [uw[hh][:, HEAD_DIM:].astype(BF16) for hh in hs]
        qk16 = [jnp.where(lbd > 0.0, qk_raw[hh] * gamma[hh], 0.0).astype(BF16) for hh in hs]
        qg = [(q_ref[pl.ds(r0, g), cols[hh]] * eg[hh]).astype(BF16) for hh in hs]
        kd_t = []
        for hh in hs:
            gc_last = jnp.concatenate(
                [jnp.broadcast_to(gc_b[hh][(ch + 1) * c - 1:(ch + 1) * c, :], (c, HEAD_DIM)) for ch in range(nch)],
                axis=0)
            kd_t.append((k[hh] * jnp.exp(gc_last - gc_b[hh])).T.astype(BF16))

        state = [state_ref[hh] for hh in hs]
        outs = [[] for _ in hs]
        zeros_c = jnp.zeros((c, HEAD_DIM), F32)
        for ch in range(nch):
            rows = slice(ch * c, (ch + 1) * c)
            s16 = [state[hh].astype(BF16) for hh in hs]
            wq = [_dot(jnp.concatenate([w16[hh][rows], qg[hh][rows]], axis=0), s16[hh]) for hh in hs]
            ws = [wq[hh][:c] for hh in hs]
            qs = [wq[hh][c:] for hh in hs]
            vn_all = [jnp.concatenate([zeros_c] * ch + [u[hh][rows] - ws[hh]] + [zeros_c] * (nch - 1 - ch),
                                      axis=0).astype(BF16) for hh in hs]
            so = [_dot(jnp.concatenate([kd_t[hh], qk16[hh][rows]], axis=0), vn_all[hh]) for hh in hs]
            sv = [so[hh][:HEAD_DIM] for hh in hs]
            ov = [so[hh][HEAD_DIM:] for hh in hs]
            for hh in hs:
                outs[hh].append(qs[hh] + ov[hh])
                state[hh] = state[hh] * eg[hh][(ch + 1) * c - 1:(ch + 1) * c, :] + sv[hh]
        for hh in hs:
            state_ref[hh] = state[hh]
            o = jnp.concatenate(outs[hh], axis=0)
            o = o * lax.rsqrt(jnp.mean(o * o, axis=-1, keepdims=True) + NORM_EPS) * nw_ref[...]
            o_ref[pl.ds(r0, g), cols[hh]] = (o * _silu(z_ref[pl.ds(r0, g), cols[hh]])).astype(o_ref.dtype)
        return carry

    lax.fori_loop(0, groups, group_body, 0)


def _gdn(qk, v, zs, a_pad, dt_pad, norm_w, batch, seq, ts=512, heads_per_step=8):
    m = qk.shape[0]
    tiles = seq // ts
    hp = heads_per_step
    hb = GDN_HEADS // hp
    wd = hp * HEAD_DIM
    row = lambda b, h, t: b * tiles + t
    return pl.pallas_call(
        functools.partial(_gdn_kernel, groups=ts // GDN_GROUP, heads_per_step=hp),
        grid=(batch, hb, tiles),
        in_specs=[pl.BlockSpec((ts, wd), lambda b, h, t: (row(b, h, t), h)),
                  pl.BlockSpec((ts, wd), lambda b, h, t: (row(b, h, t), hb + h)),
                  pl.BlockSpec((ts, wd), lambda b, h, t: (row(b, h, t), h)),
                  pl.BlockSpec((ts, wd), lambda b, h, t: (row(b, h, t), h)),
                  pl.BlockSpec((ts, V7X_LANES), lambda b, h, t: (row(b, h, t), GDN_WIDTH // V7X_LANES)),
                  pl.BlockSpec((1, V7X_LANES), lambda b, h, t: (0, 0)),
                  pl.BlockSpec((1, V7X_LANES), lambda b, h, t: (0, 0)),
                  pl.BlockSpec((1, HEAD_DIM), lambda b, h, t: (0, 0))],
        out_specs=pl.BlockSpec((ts, wd), lambda b, h, t: (row(b, h, t), h)),
        out_shape=jax.ShapeDtypeStruct((m, GDN_WIDTH), BF16),
        scratch_shapes=[pltpu.VMEM((hp, HEAD_DIM, HEAD_DIM), F32),
                        pltpu.VMEM((GDN_GROUP, GDN_GROUP), F32),
                        pltpu.VMEM((GDN_GROUP, GDN_GROUP), F32)],
        compiler_params=_cparams("arbitrary", "arbitrary", "arbitrary"),
        name="gated_delta_rule",
    )(qk, qk, v, zs, zs, a_pad, dt_pad, norm_w)


def _moba_kernel(q_ref, k_ref, v_ref, o_ref, kn_ref, km_ref, s_ref, p_ref, bias_ref, *, nblk):
    blk = MOBA_BLOCK
    for j in range(nblk):
        kt = k_ref[:, j * blk:(j + 1) * blk].astype(F32)
        kn = kt.T
        kn_ref[j * blk:(j + 1) * blk, 0:HEAD_DIM] = kn.astype(BF16)
        kn_ref[j * blk:(j + 1) * blk, HEAD_DIM:] = jnp.where(
            lax.broadcasted_iota(jnp.int32, (blk, HEAD_DIM), 1) == j, 1.0, 0.0).astype(BF16)
        km_ref[j:j + 1, :] = jnp.sum(kn, axis=0, keepdims=True) * (1.0 / blk)

    bias_ref[...] = jnp.zeros_like(bias_ref)

    key_i = lax.broadcasted_iota(jnp.int32, (blk, blk), 0)
    qry_i = lax.broadcasted_iota(jnp.int32, (blk, blk), 1)
    causal_bias = jnp.where(key_i <= qry_i, 0.0, NEG_INF)
    groups = blk // V7X_SUBLANES

    def scores(i):
        s_buf = s_ref.at[i % 2]
        qt = q_ref[:, i * blk:(i + 1) * blk]
        if i > MOBA_TOPK:
            gate = _dot_hi(km_ref[...], qt.astype(F32))
            n_iota = lax.broadcasted_iota(jnp.int32, (nblk, blk), 0)
            valid = n_iota < i
            for n in range(i):
                g_n = gate[n:n + 1, :]
                beats = valid & ((gate > g_n) | ((gate == g_n) & (n_iota < n)))
                cnt = jnp.sum(jnp.where(beats, 1.0, 0.0), axis=0, keepdims=True)
                bias_ref[n:n + 1, :] = jnp.where(cnt < float(MOBA_TOPK), 0.0, NEG_INF)
            bias_ref[i:i + 1, :] = jnp.zeros((1, blk), F32)
            qa = jnp.concatenate([qt, bias_ref[...].astype(BF16),
                                  jnp.zeros((HEAD_DIM - nblk, blk), BF16)], axis=0)
        m_acc = None
        for j in range(i + 1):
            if i > MOBA_TOPK:
                s = _dot(kn_ref[j * blk:(j + 1) * blk, :], qa)
            else:
                s = _dot(kn_ref[j * blk:(j + 1) * blk, 0:HEAD_DIM], qt)
            if j == i:
                s = s + causal_bias
            s_buf[j * blk:(j + 1) * blk, :] = s
            part = jnp.max(s.reshape(groups, V7X_SUBLANES, blk), axis=0)
            m_acc = part if m_acc is None else jnp.maximum(m_acc, part)
        return jnp.max(m_acc, axis=0, keepdims=True)

    def softmax_weights(i, m_row):
        s_buf = s_ref.at[i % 2]
        p_buf = p_ref.at[i % 2]
        l_acc = None
        for j in range(i + 1):
            p = jnp.exp2(s_buf[j * blk:(j + 1) * blk, :] - m_row)
            part = jnp.sum(p.reshape(groups, V7X_SUBLANES, blk), axis=0)
            l_acc = part if l_acc is None else l_acc + part
            p_buf[j * blk:(j + 1) * blk, :] = p.astype(BF16)
        return jnp.sum(l_acc, axis=0, keepdims=True)

    def weighted_values(i, l_row):
        p_buf = p_ref.at[i % 2]
        o_t = _dot(v_ref[:, 0:(i + 1) * blk], p_buf[0:(i + 1) * blk, :]) / l_row
        o_ref[i * blk:(i + 1) * blk, :] = o_t.T.astype(o_ref.dtype)

    m_row = {0: scores(0)}
    l_row = {}
    for i in range(nblk):
        if i + 1 < nblk:
            m_row[i + 1] = scores(i + 1)
        if i >= 1:
            weighted_values(i - 1, l_row.pop(i - 1))
        l_row[i] = softmax_weights(i, m_row.pop(i))
    weighted_values(nblk - 1, l_row.pop(nblk - 1))


def _moba(qkv_t, batch, seq):
    nblk = seq // MOBA_BLOCK
    hb = MOBA_HEADS
    return pl.pallas_call(
        functools.partial(_moba_kernel, nblk=nblk),
        grid=(batch, MOBA_HEADS),
        in_specs=[pl.BlockSpec((None, HEAD_DIM, seq), lambda b, h: (b, h, 0)),
                  pl.BlockSpec((None, HEAD_DIM, seq), lambda b, h: (b, hb + h, 0)),
                  pl.BlockSpec((None, HEAD_DIM, seq), lambda b, h: (b, 2 * hb + h, 0))],
        out_specs=pl.BlockSpec((seq, HEAD_DIM), lambda b, h: (b, h)),
        out_shape=jax.ShapeDtypeStruct((batch * seq, MOBA_WIDTH), BF16),
        scratch_shapes=[pltpu.VMEM((seq, 2 * HEAD_DIM), BF16),
                        pltpu.VMEM((nblk, HEAD_DIM), F32),
                        pltpu.VMEM((2, seq, MOBA_BLOCK), F32),
                        pltpu.VMEM((2, seq, MOBA_BLOCK), BF16),
                        pltpu.VMEM((nblk, MOBA_BLOCK), F32)],
        compiler_params=_cparams("arbitrary", "arbitrary"),
        name="moba_attention",
    )(qkv_t, qkv_t, qkv_t)


def _mix_kernel(a_ref, b_ref, ga_ref, gb_ref, wa_ref, wb_ref, wo_ref, x_ref, nw_ref, x1_ref, h_ref, *, sub):
    for r in range(a_ref.shape[0] // sub):
        rows = slice(r * sub, (r + 1) * sub)
        ya = _dot(a_ref[rows, :], wa_ref[...])
        yb = _dot(b_ref[rows, :], wb_ref[...])
        merged = (ga_ref[rows, :] * ya + gb_ref[rows, :] * yb).astype(BF16)
        x1 = x_ref[rows, :] + _dot(merged, wo_ref[...])
        x1_ref[rows, :] = x1
        h = x1 * lax.rsqrt(jnp.mean(x1 * x1, axis=-1, keepdims=True) + NORM_EPS) * nw_ref[...]
        h_ref[rows, :] = h.astype(h_ref.dtype)


def _mix(ya_in, yb_in, gates, wa, wb, wo, x2d, nw, tm=512, sub=512):
    m, k = ya_in.shape
    n = wo.shape[1]
    return pl.pallas_call(
        functools.partial(_mix_kernel, sub=sub),
        grid=(m // tm,),
        in_specs=[pl.BlockSpec((tm, k), lambda i: (i, 0)),
                  pl.BlockSpec((tm, k), lambda i: (i, 0)),
                  pl.BlockSpec((tm, n), lambda i: (i, 0)),
                  pl.BlockSpec((tm, n), lambda i: (i, 1)),
                  pl.BlockSpec((k, n), lambda i: (0, 0)),
                  pl.BlockSpec((k, n), lambda i: (0, 0)),
                  pl.BlockSpec((n, n), lambda i: (0, 0)),
                  pl.BlockSpec((tm, n), lambda i: (i, 0)),
                  pl.BlockSpec((1, n), lambda i: (0, 0))],
        out_specs=[pl.BlockSpec((tm, n), lambda i: (i, 0)),
                   pl.BlockSpec((tm, n), lambda i: (i, 0))],
        out_shape=[jax.ShapeDtypeStruct((m, n), F32),
                   jax.ShapeDtypeStruct((m, n), BF16)],
        compiler_params=_cparams("arbitrary"),
        name="branch_mix_out_proj_norm2",
    )(ya_in, yb_in, gates, gates, wa, wb, wo, x2d, nw.reshape(1, n))


def _upproj_kernel(h_ref, wg_ref, wv_ref, cg_ref, cv_ref, bg_ref, bv_ref, o_ref,
                   w_ref, carry_ref, *, tiles_per_seq, sub):
    m = pl.program_id(1)
    tn = wg_ref.shape[1]

    @pl.when(m == 0)
    def _():
        w_ref[:, :tn] = wg_ref[...].astype(BF16)
        w_ref[:, tn:] = wv_ref[...].astype(BF16)

    @pl.when(m % tiles_per_seq == 0)
    def _():
        carry_ref[...] = jnp.zeros_like(carry_ref)

    cw = jnp.concatenate([cg_ref[...], cv_ref[...]], axis=1)
    bias = jnp.concatenate([bg_ref[...], bv_ref[...]], axis=1)
    prev = carry_ref[...]
    for r in range(h_ref.shape[0] // sub):
        u = _dot(h_ref[r * sub:(r + 1) * sub, :], w_ref[...])
        y = _causal_conv(u, prev, cw, FFN_CONV) + bias
        prev = u[sub - V7X_SUBLANES:sub, :]
        o_ref[r * sub:(r + 1) * sub, :] = (_silu(y[:, :tn]) * y[:, tn:]).astype(o_ref.dtype)
    carry_ref[...] = prev


def _upproj(h, w_up, conv_w, conv_b, seq, tm=1024, tn=512, sub=1024):
    m, k = h.shape
    nb = D_FF // tn
    return pl.pallas_call(
        functools.partial(_upproj_kernel, tiles_per_seq=seq // tm, sub=sub),
        grid=(nb, m // tm),
        in_specs=[pl.BlockSpec((tm, k), lambda j, i: (i, 0)),
                  pl.BlockSpec((k, tn), lambda j, i: (0, j)),
                  pl.BlockSpec((k, tn), lambda j, i: (0, nb + j)),
                  pl.BlockSpec((FFN_CONV, tn), lambda j, i: (0, j)),
                  pl.BlockSpec((FFN_CONV, tn), lambda j, i: (0, nb + j)),
                  pl.BlockSpec((1, tn), lambda j, i: (0, j)),
                  pl.BlockSpec((1, tn), lambda j, i: (0, nb + j))],
        out_specs=pl.BlockSpec((tm, tn), lambda j, i: (i, j)),
        out_shape=jax.ShapeDtypeStruct((m, D_FF), BF16),
        scratch_shapes=[pltpu.VMEM((k, 2 * tn), BF16),
                        pltpu.VMEM((V7X_SUBLANES, 2 * tn), F32)],
        compiler_params=_cparams("arbitrary", "arbitrary"),
        name="up_proj_conv_gate",
    )(h, w_up, w_up, conv_w, conv_w, conv_b, conv_b)


def _downproj_kernel(a_ref, w_ref, x_ref, o_ref):
    o_ref[...] = x_ref[...] + _dot(a_ref[...], w_ref[...])


def _downproj(a, w, x1, tm=512, tn=1024):
    m, k = a.shape
    n = w.shape[1]
    return pl.pallas_call(
        _downproj_kernel,
        grid=(n // tn, m // tm),
        in_specs=[pl.BlockSpec((tm, k), lambda j, i: (i, 0)),
                  pl.BlockSpec((k, tn), lambda j, i: (0, j)),
                  pl.BlockSpec((tm, tn), lambda j, i: (i, j))],
        out_specs=pl.BlockSpec((tm, tn), lambda j, i: (i, j)),
        out_shape=jax.ShapeDtypeStruct((m, n), F32),
        compiler_params=_cparams("arbitrary", "arbitrary"),
        name="down_proj",
    )(a, w, x1)


def _downproj_norm_kernel(a_ref, w_ref, x_ref, nw_ref, o_ref):
    x2 = x_ref[...] + _dot(a_ref[...], w_ref[...])
    o_ref[...] = x2 * lax.rsqrt(jnp.mean(x2 * x2, axis=-1, keepdims=True) + NORM_EPS) * nw_ref[...]


def _downproj_norm(a, w, x1, nw, tm=512):
    m, k = a.shape
    n = w.shape[1]
    return pl.pallas_call(
        _downproj_norm_kernel,
        grid=(m // tm,),
        in_specs=[pl.BlockSpec((tm, k), lambda i: (i, 0)),
                  pl.BlockSpec((k, n), lambda i: (0, 0)),
                  pl.BlockSpec((tm, n), lambda i: (i, 0)),
                  pl.BlockSpec((1, n), lambda i: (0, 0))],
        out_specs=pl.BlockSpec((tm, n), lambda i: (i, 0)),
        out_shape=jax.ShapeDtypeStruct((m, n), F32),
        compiler_params=_cparams("arbitrary"),
        name="down_proj_final_norm",
    )(a, w, x1, nw.reshape(1, n))


def _pad_lanes(vec, offset):
    return jnp.zeros((1, V7X_LANES), F32).at[0, offset:offset + vec.shape[0]].set(vec.astype(F32))


def kernel(x, positions, ln1, w_in, gdn_conv, gdn_a_log, gdn_dt_bias, gdn_norm, w_branch_a, w_branch_b,
           w_out, ln2, w_up, ffn_conv, ffn_conv_bias, w_down, final_norm):
    batch, seq, d = x.shape
    m = batch * seq
    assert d == D_MODEL and seq % 1024 == 0 and seq % MOBA_BLOCK == 0, (x.shape,)
    assert w_in.shape[1:] == (D_MODEL, 3 * GDN_WIDTH + 2 * GDN_HEADS + GDN_WIDTH + 3 * MOBA_WIDTH + 2 * D_MODEL), w_in.shape
    depth = ln1.shape[0]
    o_qkv_a = 3 * GDN_WIDTH
    o_small = o_qkv_a + 2 * GDN_HEADS
    o_z = o_small + GDN_WIDTH
    o_qkv_b = o_z + 3 * MOBA_WIDTH
    inv_freq = (ROPE_THETA ** (-jnp.arange(ROPE_HALF, dtype=F32) / ROPE_HALF)).reshape(ROPE_HALF, 1)

    x2d = x.reshape(m, d)
    for l in range(depth):
        wt = jnp.swapaxes(w_in[l], 0, 1)
        wt_zs = jnp.concatenate(
            [wt[o_small:o_z], jnp.pad(wt[o_qkv_a:o_small], ((0, V7X_LANES - 2 * GDN_HEADS), (0, 0)))], axis=0)

        h1 = _norm_cast(x2d, ln1[l])
        qk_a = _gdn_proj(h1, wt, gdn_conv[l], seq, first=0, parts=2, l2norm=True, name="gdn_qk_proj")
        v_a = _gdn_proj(h1, wt, gdn_conv[l], seq, first=2, parts=1, l2norm=False, name="gdn_v_proj")
        zs = _proj_t(h1, wt_zs, 0, GDN_WIDTH + V7X_LANES, act=None, out_dtype=F32, tm=1024,
                     tn=GDN_WIDTH + V7X_LANES, name="z_beta_decay_proj")
        gates = _proj_t(h1, wt, o_qkv_b, 2 * D_MODEL, act="sigmoid", out_dtype=BF16, tm=1024, tn=1024, name="gate_proj")
        qkv_b_t = _moba_proj(h1, wt, o_z, positions, inv_freq, batch, seq)

        gdn_out = _gdn(qk_a, v_a, zs, _pad_lanes(gdn_a_log[l], GDN_HEADS), _pad_lanes(gdn_dt_bias[l], GDN_HEADS),
                       gdn_norm[l].reshape(1, HEAD_DIM), batch, seq)
        attn = _moba(qkv_b_t, batch, seq)
        x1, h2 = _mix(gdn_out, attn, gates, w_branch_a[l].astype(BF16), w_branch_b[l].astype(BF16),
                      w_out[l].astype(BF16), x2d, ln2[l])
        act = _upproj(h2, w_up[l], ffn_conv[l], ffn_conv_bias[l].reshape(1, 2 * D_FF), seq)
        if l == depth - 1:
            return _downproj_norm(act, w_down[l].astype(BF16), x1, final_norm).reshape(batch, seq, d)
        x2d = _downproj(act, w_down[l].astype(BF16), x1)
    return _final_norm(x2d, final_norm).reshape(batch, seq, d)
```

```python
import functools
import math

import jax
import jax.numpy as jnp
import numpy as np
from jax import lax
from jax.experimental import pallas as pl
from jax.experimental.pallas import tpu as pltpu

F32 = jnp.float32
BF16 = jnp.bfloat16

D_MODEL = 2048
GDN_HEADS = 8
HEAD_DIM = 128
GDN_WIDTH = GDN_HEADS * HEAD_DIM
GDN_CONV = 4
GDN_CHUNK = 64
MOBA_HEADS = 8
MOBA_WIDTH = MOBA_HEADS * HEAD_DIM
MOBA_BLOCK = 256
MOBA_TOPK = 3
ROPE_THETA = 500000.0
ROPE_DIM = HEAD_DIM // 4
ROPE_HALF = ROPE_DIM // 2
D_FF = 5632
FFN_CONV = 3
NORM_EPS = 1e-6
L2_EPS = 1e-6
NEG_INF = -1e30
NEG_LOG2_E = -math.log2(math.e)

V7X_LANES = 128
V7X_SUBLANES = 8
V7X_VMEM_LIMIT_BYTES = 56 * 1024 * 1024

GDN_GROUP = 4 * GDN_CHUNK
MOBA_Q_SCALE = HEAD_DIM ** -0.5 * math.log2(math.e)


def _cparams(*sem):
    return pltpu.CompilerParams(dimension_semantics=sem, vmem_limit_bytes=V7X_VMEM_LIMIT_BYTES)


def _dot(a, b):
    return jnp.dot(a, b, preferred_element_type=F32)


def _dot_nt(a, b):
    return lax.dot_general(a, b, (((1,), (1,)), ((), ())), preferred_element_type=F32)


def _dot_hi(a, b):
    return jnp.dot(a, b, preferred_element_type=F32, precision=lax.Precision.HIGHEST)


def _split3(x):
    hi = x.astype(BF16)
    r1 = x - hi.astype(F32)
    mid = r1.astype(BF16)
    lo = (r1 - mid.astype(F32)).astype(BF16)
    return hi, mid, lo


def _sigmoid(x):
    return 1.0 / (1.0 + jnp.exp2(x * NEG_LOG2_E))


def _silu(x):
    return x * _sigmoid(x)


def _shift_rows(cur, prev8, s):
    if s == 0:
        return cur
    rolled = pltpu.roll(cur, s, axis=0)
    rolled_prev = pltpu.roll(prev8, s, axis=0)
    row = lax.broadcasted_iota(jnp.int32, prev8.shape, 0)
    first = jnp.where(row < s, rolled_prev, rolled[0:V7X_SUBLANES])
    return jnp.concatenate([first, rolled[V7X_SUBLANES:]], axis=0)


def _causal_conv(cur, prev8, cw, width):
    y = cw[width - 1:width, :] * cur
    for j in range(width - 1):
        y = y + cw[j:j + 1, :] * _shift_rows(cur, prev8, width - 1 - j)
    return y


def _norm_kernel(x_ref, w_ref, h_ref):
    x = x_ref[...]
    y = x * lax.rsqrt(jnp.mean(x * x, axis=-1, keepdims=True) + NORM_EPS) * w_ref[...]
    h_ref[...] = y.astype(h_ref.dtype)


def _norm_cast(x2d, w, tm=512):
    m, d = x2d.shape
    return pl.pallas_call(
        _norm_kernel,
        grid=(m // tm,),
        in_specs=[pl.BlockSpec((tm, d), lambda i: (i, 0)),
                  pl.BlockSpec((1, d), lambda i: (0, 0))],
        out_specs=pl.BlockSpec((tm, d), lambda i: (i, 0)),
        out_shape=jax.ShapeDtypeStruct((m, d), BF16),
        compiler_params=_cparams("arbitrary"),
        name="norm1",
    )(x2d, w.reshape(1, d))


def _final_norm_kernel(x_ref, w_ref, o_ref):
    x = x_ref[...]
    o_ref[...] = x * lax.rsqrt(jnp.mean(x * x, axis=-1, keepdims=True) + NORM_EPS) * w_ref[...]


def _final_norm(x2d, w, tm=512):
    m, d = x2d.shape
    return pl.pallas_call(
        _final_norm_kernel,
        grid=(m // tm,),
        in_specs=[pl.BlockSpec((tm, d), lambda i: (i, 0)),
                  pl.BlockSpec((1, d), lambda i: (0, 0))],
        out_specs=pl.BlockSpec((tm, d), lambda i: (i, 0)),
        out_shape=jax.ShapeDtypeStruct((m, d), F32),
        compiler_params=_cparams("arbitrary"),
        name="final_norm",
    )(x2d, w.reshape(1, d))


def _proj_t_kernel(h_ref, wt_ref, o_ref, w16_ref, *, act):
    @pl.when(pl.program_id(1) == 0)
    def _():
        w16_ref[...] = wt_ref[...].T.astype(BF16)

    acc = _dot(h_ref[...], w16_ref[...])
    if act == "sigmoid":
        acc = _sigmoid(acc)
    o_ref[...] = acc.astype(o_ref.dtype)


def _proj_t(h, wt, row0, n, *, act, out_dtype, tm, tn, name):
    m, k = h.shape
    return pl.pallas_call(
        functools.partial(_proj_t_kernel, act=act),
        grid=(n // tn, m // tm),
        in_specs=[pl.BlockSpec((tm, k), lambda j, i: (i, 0)),
                  pl.BlockSpec((pl.Element(tn), pl.Element(k)), lambda j, i: (pl.multiple_of(row0 + j * tn, 16), 0))],
        out_specs=pl.BlockSpec((tm, tn), lambda j, i: (i, j)),
        out_shape=jax.ShapeDtypeStruct((m, n), out_dtype),
        scratch_shapes=[pltpu.VMEM((k, tn), BF16)],
        compiler_params=_cparams("arbitrary", "arbitrary"),
        name=name,
    )(h, wt)


def _gdn_proj_kernel(h_ref, w_ref, cw_ref, o_ref, w16_ref, carry_ref, *, tiles_per_seq, tiles_per_part, sub, l2norm):
    n = pl.program_id(0)
    m = pl.program_id(1)
    tn = w_ref.shape[0]

    @pl.when(m == 0)
    def _():
        w16_ref[...] = w_ref[...].T.astype(BF16)

    @pl.when(m % tiles_per_seq == 0)
    def _():
        carry_ref[...] = jnp.zeros_like(carry_ref)

    part = n // tiles_per_part
    q_scale = jnp.where(part == 0, HEAD_DIM ** -0.5, 1.0).astype(F32)
    prev = carry_ref[...]
    for r in range(h_ref.shape[0] // sub):
        acc = _dot(h_ref[r * sub:(r + 1) * sub, :], w16_ref[...])
        y = _silu(_causal_conv(acc, prev, cw_ref[...], GDN_CONV))
        prev = acc[sub - V7X_SUBLANES:sub, :]
        if l2norm:
            heads = []
            for hd in range(tn // HEAD_DIM):
                blk = y[:, hd * HEAD_DIM:(hd + 1) * HEAD_DIM]
                heads.append(blk * (lax.rsqrt(jnp.sum(blk * blk, axis=-1, keepdims=True) + L2_EPS) * q_scale))
            y = jnp.concatenate(heads, axis=1)
        o_ref[r * sub:(r + 1) * sub, :] = y
    carry_ref[...] = prev


def _gdn_proj(h, w_in, conv_w, seq, *, first, parts, l2norm, name, tm=1024, tn=1024, sub=1024):
    m, k = h.shape
    n = parts * GDN_WIDTH
    off = first * GDN_WIDTH // tn
    return pl.pallas_call(
        functools.partial(_gdn_proj_kernel, tiles_per_seq=seq // tm, tiles_per_part=GDN_WIDTH // tn, sub=sub,
                          l2norm=l2norm),
        grid=(n // tn, m // tm),
        in_specs=[pl.BlockSpec((tm, k), lambda j, i: (i, 0)),
                  pl.BlockSpec((tn, k), lambda j, i: (off + j, 0)),
                  pl.BlockSpec((GDN_CONV, tn), lambda j, i: (0, off + j))],
        out_specs=pl.BlockSpec((tm, tn), lambda j, i: (i, j)),
        out_shape=jax.ShapeDtypeStruct((m, n), F32),
        scratch_shapes=[pltpu.VMEM((k, tn), BF16),
                        pltpu.VMEM((V7X_SUBLANES, tn), F32)],
        compiler_params=_cparams("arbitrary", "arbitrary"),
        name=name,
    )(h, w_in, conv_w)


def _moba_proj_kernel(h_ref, wt_ref, pos_ref, freq_ref, o_ref, w16_ref):
    part = pl.program_id(0)

    @pl.when(pl.program_id(1) == 0)
    def _():
        w16_ref[...] = wt_ref[...].astype(BF16)

    scale = jnp.where(part == 0, MOBA_Q_SCALE, 1.0).astype(F32)
    acc = _dot_nt(w16_ref[...], h_ref[...]) * scale
    ang = freq_ref[...] * pos_ref[...].astype(F32)
    roped = part < 2
    cos = jnp.where(roped, jnp.cos(ang), 1.0)
    sin = jnp.where(roped, jnp.sin(ang), 0.0)
    rows = []
    for hd in range(acc.shape[0] // HEAD_DIM):
        base = hd * HEAD_DIM
        x1 = acc[base:base + ROPE_HALF]
        x2 = acc[base + ROPE_HALF:base + ROPE_DIM]
        rows.append(x1 * cos - x2 * sin)
        rows.append(x2 * cos + x1 * sin)
        rows.append(acc[base + ROPE_DIM:base + HEAD_DIM])
    o_ref[...] = jnp.concatenate(rows, axis=0).astype(o_ref.dtype)


def _moba_proj(h, wt, row0, positions, inv_freq, batch, seq, tt=2048):
    m, k = h.shape
    tr = MOBA_WIDTH
    n = 3 * MOBA_WIDTH
    per_seq = seq // tt
    return pl.pallas_call(
        _moba_proj_kernel,
        grid=(n // tr, m // tt),
        in_specs=[pl.BlockSpec((tt, k), lambda r, i: (i, 0)),
                  pl.BlockSpec((pl.Element(tr), pl.Element(k)), lambda r, i: (pl.multiple_of(row0 + r * tr, 16), 0)),
                  pl.BlockSpec((None, 1, tt), lambda r, i: (i // per_seq, 0, i % per_seq)),
                  pl.BlockSpec((ROPE_HALF, 1), lambda r, i: (0, 0))],
        out_specs=pl.BlockSpec((None, tr, tt), lambda r, i: (i // per_seq, r, i % per_seq)),
        out_shape=jax.ShapeDtypeStruct((batch, n, seq), BF16),
        scratch_shapes=[pltpu.VMEM((tr, k), BF16)],
        compiler_params=_cparams("arbitrary", "arbitrary"),
        name="moba_qkv_proj",
    )(h, wt, positions.reshape(batch, 1, seq), inv_freq)


def _gdn_kernel(q_ref, k_ref, v_ref, z_ref, sm_ref, a_ref, dt_ref, nw_ref, o_ref,
                state_ref, lbd_ref, subd_ref, *, groups, heads_per_step):
    t = pl.program_id(2)
    g = GDN_GROUP
    c = GDN_CHUNK

    @pl.when(t == 0)
    def _():
        state_ref[...] = jnp.zeros_like(state_ref)

    ri = lax.broadcasted_iota(jnp.int32, (g, g), 0)
    ci = lax.broadcasted_iota(jnp.int32, (g, g), 1)
    shift = int(math.log2(c))
    same = jnp.right_shift(ri, shift) == jnp.right_shift(ci, shift)
    lbd_ref[...] = jnp.where(same & (ci <= ri), 1.0, 0.0)
    subd_ref[...] = jnp.where(same & (ri > ci), 1.0, 0.0)
    lane = lax.broadcasted_iota(jnp.int32, (g, V7X_LANES), 1)
    eye = jnp.where(ri == ci, 1.0, 0.0)

    def group_body(gi, carry):
        hs = range(heads_per_step)
        nch = g // c
        r0 = pl.multiple_of(gi * g, g)
        sm = sm_ref[pl.ds(r0, g), :]
        beta_all = _sigmoid(sm)
        xs = sm + dt_ref[...]
        softplus = jnp.maximum(xs, 0.0) + jnp.log1p(jnp.exp(-jnp.abs(xs)))
        g_all = -jnp.exp(a_ref[...]) * softplus
        lbd = lbd_ref[...]
        lbd16 = lbd.astype(BF16)
        strict = subd_ref[...]
        cols = [slice(hh * HEAD_DIM, (hh + 1) * HEAD_DIM) for hh in hs]
        heads = [pl.program_id(1) * heads_per_step + hh for hh in hs]

        k = [k_ref[pl.ds(r0, g), cols[hh]] for hh in hs]
        k16 = [k[hh].astype(BF16) for hh in hs]
        beta_b = [jnp.broadcast_to(jnp.sum(jnp.where(lane == heads[hh], beta_all, 0.0), axis=-1, keepdims=True),
                                   (g, HEAD_DIM)) for hh in hs]
        g_b = [jnp.broadcast_to(jnp.sum(jnp.where(lane == GDN_HEADS + heads[hh], g_all, 0.0), axis=-1, keepdims=True),
                                (g, HEAD_DIM)) for hh in hs]
        kb = [k[hh] * beta_b[hh] for hh in hs]

        cs = [_dot(lbd16, jnp.concatenate(_split3(g_b[hh]), axis=1)) for hh in hs]
        kq = [_dot_nt(jnp.concatenate([kb[hh].astype(BF16), q_ref[pl.ds(r0, g), cols[hh]].astype(BF16)], axis=0), k16[hh])
              for hh in hs]
        kk = [kq[hh][:g] for hh in hs]
        qk_raw = [kq[hh][g:] for hh in hs]
        gc_b = [cs[hh][:, :HEAD_DIM] + cs[hh][:, HEAD_DIM:2 * HEAD_DIM] + cs[hh][:, 2 * HEAD_DIM:] for hh in hs]
        gamma = []
        for hh in hs:
            gc_row = gc_b[hh].T[0:1, :]
            dmat = jnp.where(lbd > 0.0, jnp.concatenate([gc_b[hh], gc_b[hh]], axis=1) - gc_row, 0.0)
            gamma.append(jnp.exp(dmat))
        eg = [jnp.exp(gc_b[hh]) for hh in hs]

        p = [-jnp.where(strict > 0.0, kk[hh] * gamma[hh], 0.0) for hh in hs]
        t_mat = [eye + p[hh] for hh in hs]
        p16 = [p[hh].astype(BF16) for hh in hs]
        p16 = [_dot(p16[hh], p16[hh]).astype(BF16) for hh in hs]
        for _ in range(4):
            st = [_dot(jnp.concatenate([p16[hh], t_mat[hh].astype(BF16)], axis=0), p16[hh]) for hh in hs]
            p16 = [st[hh][:g].astype(BF16) for hh in hs]
            t_mat = [t_mat[hh] + st[hh][g:] for hh in hs]
        tp = [_dot(t_mat[hh].astype(BF16), p16[hh]) for hh in hs]
        t_mat = [t_mat[hh] + tp[hh] for hh in hs]
        uw = [_dot(t_mat[hh].astype(BF16),
                   jnp.concatenate([v_ref[pl.ds(r0, g), cols[hh]] * beta_b[hh], kb[hh] * eg[hh]], axis=1).astype(BF16))
              for hh in hs]
        u = [uw[hh][:, :HEAD_DIM] for hh in hs]
        w16 = [uw[hh][:, HEAD_DIM:].astype(BF16) for hh in hs]
        qk16 = [jnp.where(lbd > 0.0, qk_raw[hh] * gamma[hh], 0.0).astype(BF16) for hh in hs]
        qg = [(q_ref[pl.ds(r0, g), cols[hh]] * eg[hh]).astype(BF16) for hh in hs]
        kd_t = []
        for hh in hs:
            gc_last = jnp.concatenate(
                [jnp.broadcast_to(gc_b[hh][(ch + 1) * c - 1:(ch + 1) * c, :], (c, HEAD_DIM)) for ch in range(nch)],
                axis=0)
            kd_t.append((k[hh] * jnp.exp(gc_last - gc_b[hh])).T.astype(BF16))

        state = [state_ref[hh] for hh in hs]
        outs = [[] for _ in hs]
        zeros_c = jnp.zeros((c, HEAD_DIM), F32)
        for ch in range(nch):
            rows = slice(ch * c, (ch + 1) * c)
            s16 = [state[hh].astype(BF16) for hh in hs]
            wq = [_dot(jnp.concatenate([w16[hh][rows], qg[hh][rows]], axis=0), s16[hh]) for hh in hs]
            ws = [wq[hh][:c] for hh in hs]
            qs = [wq[hh][c:] for hh in hs]
            vn_all = [jnp.concatenate([zeros_c] * ch + [u[hh][rows] - ws[hh]] + [zeros_c] * (nch - 1 - ch),
                                      axis=0).astype(BF16) for hh in hs]
            so = [_dot(jnp.concatenate([kd_t[hh], qk16[hh][rows]], axis=0), vn_all[hh]) for hh in hs]
            sv = [so[hh][:HEAD_DIM] for hh in hs]
            ov = [so[hh][HEAD_DIM:] for hh in hs]
            for hh in hs:
                outs[hh].append(qs[hh] + ov[hh])
                state[hh] = state[hh] * eg[hh][(ch + 1) * c - 1:(ch + 1) * c, :] + sv[hh]
        for hh in hs:
            state_ref[hh] = state[hh]
            o = jnp.concatenate(outs[hh], axis=0)
            o = o * lax.rsqrt(jnp.mean(o * o, axis=-1, keepdims=True) + NORM_EPS) * nw_ref[...]
            o_ref[pl.ds(r0, g), cols[hh]] = (o * _silu(z_ref[pl.ds(r0, g), cols[hh]])).astype(o_ref.dtype)
        return carry

    lax.fori_loop(0, groups, group_body, 0)


def _gdn(qk, v, zs, a_pad, dt_pad, norm_w, batch, seq, ts=512, heads_per_step=8):
    m = qk.shape[0]
    tiles = seq // ts
    hp = heads_per_step
    hb = GDN_HEADS // hp
    wd = hp * HEAD_DIM
    row = lambda b, h, t: b * tiles + t
    return pl.pallas_call(
        functools.partial(_gdn_kernel, groups=ts // GDN_GROUP, heads_per_step=hp),
        grid=(batch, hb, tiles),
        in_specs=[pl.BlockSpec((ts, wd), lambda b, h, t: (row(b, h, t), h)),
                  pl.BlockSpec((ts, wd), lambda b, h, t: (row(b, h, t), hb + h)),
                  pl.BlockSpec((ts, wd), lambda b, h, t: (row(b, h, t), h)),
                  pl.BlockSpec((ts, wd), lambda b, h, t: (row(b, h, t), h)),
                  pl.BlockSpec((ts, V7X_LANES), lambda b, h, t: (row(b, h, t), GDN_WIDTH // V7X_LANES)),
                  pl.BlockSpec((1, V7X_LANES), lambda b, h, t: (0, 0)),
                  pl.BlockSpec((1, V7X_LANES), lambda b, h, t: (0, 0)),
                  pl.BlockSpec((1, HEAD_DIM), lambda b, h, t: (0, 0))],
        out_specs=pl.BlockSpec((ts, wd), lambda b, h, t: (row(b, h, t), h)),
        out_shape=jax.ShapeDtypeStruct((m, GDN_WIDTH), BF16),
        scratch_shapes=[pltpu.VMEM((hp, HEAD_DIM, HEAD_DIM), F32),
                        pltpu.VMEM((GDN_GROUP, GDN_GROUP), F32),
                        pltpu.VMEM((GDN_GROUP, GDN_GROUP), F32)],
        compiler_params=_cparams("arbitrary", "arbitrary", "arbitrary"),
        name="gated_delta_rule",
    )(qk, qk, v, zs, zs, a_pad, dt_pad, norm_w)


def _moba_kernel(q_ref, k_ref, v_ref, o_ref, kn_ref, km_ref, s_ref, p_ref, bias_ref, *, nblk):
    blk = MOBA_BLOCK
    for j in range(nblk):
        kt = k_ref[:, j * blk:(j + 1) * blk].astype(F32)
        kn = kt.T
        kn_ref[j * blk:(j + 1) * blk, 0:HEAD_DIM] = kn.astype(BF16)
        kn_ref[j * blk:(j + 1) * blk, HEAD_DIM:] = jnp.where(
            lax.broadcasted_iota(jnp.int32, (blk, HEAD_DIM), 1) == j, 1.0, 0.0).astype(BF16)
        km_ref[j:j + 1, :] = jnp.sum(kn, axis=0, keepdims=True) * (1.0 / blk)

    bias_ref[...] = jnp.zeros_like(bias_ref)

    key_i = lax.broadcasted_iota(jnp.int32, (blk, blk), 0)
    qry_i = lax.broadcasted_iota(jnp.int32, (blk, blk), 1)
    causal_bias = jnp.where(key_i <= qry_i, 0.0, NEG_INF)
    groups = blk // V7X_SUBLANES

    def scores(i):
        s_buf = s_ref.at[i % 2]
        qt = q_ref[:, i * blk:(i + 1) * blk]
        if i > MOBA_TOPK:
            gate = _dot_hi(km_ref[...], qt.astype(F32))
            n_iota = lax.broadcasted_iota(jnp.int32, (nblk, blk), 0)
            valid = n_iota < i
            for n in range(i):
                g_n = gate[n:n + 1, :]
                beats = valid & ((gate > g_n) | ((gate == g_n) & (n_iota < n)))
                cnt = jnp.sum(jnp.where(beats, 1.0, 0.0), axis=0, keepdims=True)
                bias_ref[n:n + 1, :] = jnp.where(cnt < float(MOBA_TOPK), 0.0, NEG_INF)
            bias_ref[i:i + 1, :] = jnp.zeros((1, blk), F32)
            qa = jnp.concatenate([qt, bias_ref[...].astype(BF16),
                                  jnp.zeros((HEAD_DIM - nblk, blk), BF16)], axis=0)
        m_acc = None
        for j in range(i + 1):
            if i > MOBA_TOPK:
                s = _dot(kn_ref[j * blk:(j + 1) * blk, :], qa)
            else:
                s = _dot(kn_ref[j * blk:(j + 1) * blk, 0:HEAD_DIM], qt)
            if j == i:
                s = s + causal_bias
            s_buf[j * blk:(j + 1) * blk, :] = s
            part = jnp.max(s.reshape(groups, V7X_SUBLANES, blk), axis=0)
            m_acc = part if m_acc is None else jnp.maximum(m_acc, part)
        return jnp.max(m_acc, axis=0, keepdims=True)

    def softmax_weights(i, m_row):
        s_buf = s_ref.at[i % 2]
        p_buf = p_ref.at[i % 2]
        l_acc = None
        for j in range(i + 1):
            p = jnp.exp2(s_buf[j * blk:(j + 1) * blk, :] - m_row)
            part = jnp.sum(p.reshape(groups, V7X_SUBLANES, blk), axis=0)
            l_acc = part if l_acc is None else l_acc + part
            p_buf[j * blk:(j + 1) * blk, :] = p.astype(BF16)
        return jnp.sum(l_acc, axis=0, keepdims=True)

    def weighted_values(i, l_row):
        p_buf = p_ref.at[i % 2]
        o_t = _dot(v_ref[:, 0:(i + 1) * blk], p_buf[0:(i + 1) * blk, :]) / l_row
        o_ref[i * blk:(i + 1) * blk, :] = o_t.T.astype(o_ref.dtype)

    m_row = {0: scores(0)}
    l_row = {}
    for i in range(nblk):
        if i + 1 < nblk:
            m_row[i + 1] = scores(i + 1)
        if i >= 1:
            weighted_values(i - 1, l_row.pop(i - 1))
        l_row[i] = softmax_weights(i, m_row.pop(i))
    weighted_values(nblk - 1, l_row.pop(nblk - 1))


def _moba(qkv_t, batch, seq):
    nblk = seq // MOBA_BLOCK
    hb = MOBA_HEADS
    return pl.pallas_call(
        functools.partial(_moba_kernel, nblk=nblk),
        grid=(batch, MOBA_HEADS),
        in_specs=[pl.BlockSpec((None, HEAD_DIM, seq), lambda b, h: (b, h, 0)),
                  pl.BlockSpec((None, HEAD_DIM, seq), lambda b, h: (b, hb + h, 0)),
                  pl.BlockSpec((None, HEAD_DIM, seq), lambda b, h: (b, 2 * hb + h, 0))],
        out_specs=pl.BlockSpec((seq, HEAD_DIM), lambda b, h: (b, h)),
        out_shape=jax.ShapeDtypeStruct((batch * seq, MOBA_WIDTH), BF16),
        scratch_shapes=[pltpu.VMEM((seq, 2 * HEAD_DIM), BF16),
                        pltpu.VMEM((nblk, HEAD_DIM), F32),
                        pltpu.VMEM((2, seq, MOBA_BLOCK), F32),
                        pltpu.VMEM((2, seq, MOBA_BLOCK), BF16),
                        pltpu.VMEM((nblk, MOBA_BLOCK), F32)],
        compiler_params=_cparams("arbitrary", "arbitrary"),
        name="moba_attention",
    )(qkv_t, qkv_t, qkv_t)


def _mix_kernel(a_ref, b_ref, ga_ref, gb_ref, wa_ref, wb_ref, wo_ref, x_ref, nw_ref, x1_ref, h_ref, *, sub):
    for r in range(a_ref.shape[0] // sub):
        rows = slice(r * sub, (r + 1) * sub)
        ya = _dot(a_ref[rows, :], wa_ref[...])
        yb = _dot(b_ref[rows, :], wb_ref[...])
        merged = (ga_ref[rows, :] * ya + gb_ref[rows, :] * yb).astype(BF16)
        x1 = x_ref[rows, :] + _dot(merged, wo_ref[...])
        x1_ref[rows, :] = x1
        h = x1 * lax.rsqrt(jnp.mean(x1 * x1, axis=-1, keepdims=True) + NORM_EPS) * nw_ref[...]
        h_ref[rows, :] = h.astype(h_ref.dtype)


def _mix(ya_in, yb_in, gates, wa, wb, wo, x2d, nw, tm=512, sub=512):
    m, k = ya_in.shape
    n = wo.shape[1]
    return pl.pallas_call(
        functools.partial(_mix_kernel, sub=sub),
        grid=(m // tm,),
        in_specs=[pl.BlockSpec((tm, k), lambda i: (i, 0)),
                  pl.BlockSpec((tm, k), lambda i: (i, 0)),
                  pl.BlockSpec((tm, n), lambda i: (i, 0)),
                  pl.BlockSpec((tm, n), lambda i: (i, 1)),
                  pl.BlockSpec((k, n), lambda i: (0, 0)),
                  pl.BlockSpec((k, n), lambda i: (0, 0)),
                  pl.BlockSpec((n, n), lambda i: (0, 0)),
                  pl.BlockSpec((tm, n), lambda i: (i, 0)),
                  pl.BlockSpec((1, n), lambda i: (0, 0))],
        out_specs=[pl.BlockSpec((tm, n), lambda i: (i, 0)),
                   pl.BlockSpec((tm, n), lambda i: (i, 0))],
        out_shape=[jax.ShapeDtypeStruct((m, n), F32),
                   jax.ShapeDtypeStruct((m, n), BF16)],
        compiler_params=_cparams("arbitrary"),
        name="branch_mix_out_proj_norm2",
    )(ya_in, yb_in, gates, gates, wa, wb, wo, x2d, nw.reshape(1, n))


def _upproj_kernel(h_ref, wg_ref, wv_ref, cg_ref, cv_ref, bg_ref, bv_ref, o_ref,
                   w_ref, carry_ref, *, tiles_per_seq, sub):
    m = pl.program_id(1)
    tn = wg_ref.shape[1]

    @pl.when(m == 0)
    def _():
        w_ref[:, :tn] = wg_ref[...].astype(BF16)
        w_ref[:, tn:] = wv_ref[...].astype(BF16)

    @pl.when(m % tiles_per_seq == 0)
    def _():
        carry_ref[...] = jnp.zeros_like(carry_ref)

    cw = jnp.concatenate([cg_ref[...], cv_ref[...]], axis=1)
    bias = jnp.concatenate([bg_ref[...], bv_ref[...]], axis=1)
    prev = carry_ref[...]
    for r in range(h_ref.shape[0] // sub):
        u = _dot(h_ref[r * sub:(r + 1) * sub, :], w_ref[...])
        y = _causal_conv(u, prev, cw, FFN_CONV) + bias
        prev = u[sub - V7X_SUBLANES:sub, :]
        o_ref[r * sub:(r + 1) * sub, :] = (_silu(y[:, :tn]) * y[:, tn:]).astype(o_ref.dtype)
    carry_ref[...] = prev


def _upproj(h, w_up, conv_w, conv_b, seq, tm=1024, tn=512, sub=1024):
    m, k = h.shape
    nb = D_FF // tn
    return pl.pallas_call(
        functools.partial(_upproj_kernel, tiles_per_seq=seq // tm, sub=sub),
        grid=(nb, m // tm),
        in_specs=[pl.BlockSpec((tm, k), lambda j, i: (i, 0)),
                  pl.BlockSpec((k, tn), lambda j, i: (0, j)),
                  pl.BlockSpec((k, tn), lambda j, i: (0, nb + j)),
                  pl.BlockSpec((FFN_CONV, tn), lambda j, i: (0, j)),
                  pl.BlockSpec((FFN_CONV, tn), lambda j, i: (0, nb + j)),
                  pl.BlockSpec((1, tn), lambda j, i: (0, j)),
                  pl.BlockSpec((1, tn), lambda j, i: (0, nb + j))],
        out_specs=pl.BlockSpec((tm, tn), lambda j, i: (i, j)),
        out_shape=jax.ShapeDtypeStruct((m, D_FF), BF16),
        scratch_shapes=[pltpu.VMEM((k, 2 * tn), BF16),
                        pltpu.VMEM((V7X_SUBLANES, 2 * tn), F32)],
        compiler_params=_cparams("arbitrary", "arbitrary"),
        name="up_proj_conv_gate",
    )(h, w_up, w_up, conv_w, conv_w, conv_b, conv_b)


def _downproj_kernel(a_ref, w_ref, x_ref, o_ref):
    o_ref[...] = x_ref[...] + _dot(a_ref[...], w_ref[...])


def _downproj(a, w, x1, tm=512, tn=1024):
    m, k = a.shape
    n = w.shape[1]
    return pl.pallas_call(
        _downproj_kernel,
        grid=(n // tn, m // tm),
        in_specs=[pl.BlockSpec((tm, k), lambda j, i: (i, 0)),
                  pl.BlockSpec((k, tn), lambda j, i: (0, j)),
                  pl.BlockSpec((tm, tn), lambda j, i: (i, j))],
        out_specs=pl.BlockSpec((tm, tn), lambda j, i: (i, j)),
        out_shape=jax.ShapeDtypeStruct((m, n), F32),
        compiler_params=_cparams("arbitrary", "arbitrary"),
        name="down_proj",
    )(a, w, x1)


def _downproj_norm_kernel(a_ref, w_ref, x_ref, nw_ref, o_ref):
    x2 = x_ref[...] + _dot(a_ref[...], w_ref[...])
    o_ref[...] = x2 * lax.rsqrt(jnp.mean(x2 * x2, axis=-1, keepdims=True) + NORM_EPS) * nw_ref[...]


def _downproj_norm(a, w, x1, nw, tm=512):
    m, k = a.shape
    n = w.shape[1]
    return pl.pallas_call(
        _downproj_norm_kernel,
        grid=(m // tm,),
        in_specs=[pl.BlockSpec((tm, k), lambda i: (i, 0)),
                  pl.BlockSpec((k, n), lambda i: (0, 0)),
                  pl.BlockSpec((tm, n), lambda i: (i, 0)),
                  pl.BlockSpec((1, n), lambda i: (0, 0))],
        out_specs=pl.BlockSpec((tm, n), lambda i: (i, 0)),
        out_shape=jax.ShapeDtypeStruct((m, n), F32),
        compiler_params=_cparams("arbitrary"),
        name="down_proj_final_norm",
    )(a, w, x1, nw.reshape(1, n))


def _pad_lanes(vec, offset):
    return jnp.zeros((1, V7X_LANES), F32).at[0, offset:offset + vec.shape[0]].set(vec.astype(F32))


def kernel(x, positions, ln1, w_in, gdn_conv, gdn_a_log, gdn_dt_bias, gdn_norm, w_branch_a, w_branch_b,
           w_out, ln2, w_up, ffn_conv, ffn_conv_bias, w_down, final_norm):
    batch, seq, d = x.shape
    m = batch * seq
    assert d == D_MODEL and seq % 1024 == 0 and seq % MOBA_BLOCK == 0, (x.shape,)
    assert w_in.shape[1:] == (D_MODEL, 3 * GDN_WIDTH + 2 * GDN_HEADS + GDN_WIDTH + 3 * MOBA_WIDTH + 2 * D_MODEL), w_in.shape
    depth = ln1.shape[0]
    o_qkv_a = 3 * GDN_WIDTH
    o_small = o_qkv_a + 2 * GDN_HEADS
    o_z = o_small + GDN_WIDTH
    o_qkv_b = o_z + 3 * MOBA_WIDTH
    inv_freq = (ROPE_THETA ** (-jnp.arange(ROPE_HALF, dtype=F32) / ROPE_HALF)).reshape(ROPE_HALF, 1)

    x2d = x.reshape(m, d)
    for l in range(depth):
        wt = jnp.swapaxes(w_in[l], 0, 1)
        wt_zs = jnp.concatenate(
            [wt[o_small:o_z], jnp.pad(wt[o_qkv_a:o_small], ((0, V7X_LANES - 2 * GDN_HEADS), (0, 0)))], axis=0)

        h1 = _norm_cast(x2d, ln1[l])
        qk_a = _gdn_proj(h1, wt, gdn_conv[l], seq, first=0, parts=2, l2norm=True, name="gdn_qk_proj")
        v_a = _gdn_proj(h1, wt, gdn_conv[l], seq, first=2, parts=1, l2norm=False, name="gdn_v_proj")
        zs = _proj_t(h1, wt_zs, 0, GDN_WIDTH + V7X_LANES, act=None, out_dtype=F32, tm=1024,
                     tn=GDN_WIDTH + V7X_LANES, name="z_beta_decay_proj")
        gates = _proj_t(h1, wt, o_qkv_b, 2 * D_MODEL, act="sigmoid", out_dtype=BF16, tm=1024, tn=1024, name="gate_proj")
        qkv_b_t = _moba_proj(h1, wt, o_z, positions, inv_freq, batch, seq)

        gdn_out = _gdn(qk_a, v_a, zs, _pad_lanes(gdn_a_log[l], GDN_HEADS), _pad_lanes(gdn_dt_bias[l], GDN_HEADS),
                       gdn_norm[l].reshape(1, HEAD_DIM), batch, seq)
        attn = _moba(qkv_b_t, batch, seq)
        x1, h2 = _mix(gdn_out, attn, gates, w_branch_a[l].astype(BF16), w_branch_b[l].astype(BF16),
                      w_out[l].astype(BF16), x2d, ln2[l])
        act = _upproj(h2, w_up[l], ffn_conv[l], ffn_conv_bias[l].reshape(1, 2 * D_FF), seq)
        if l == depth - 1:
            return _downproj_norm(act, w_down[l].astype(BF16), x1, final_norm).reshape(batch, seq, d)
        x2d = _downproj(act, w_down[l].astype(BF16), x1)
    return _final_norm(x2d, final_norm).reshape(batch, seq, d)
```

```python
import functools
import math

import jax
import jax.numpy as jnp
import numpy as np
from jax import lax
from jax.experimental import pallas as pl
from jax.experimental.pallas import tpu as pltpu

F32 = jnp.float32
BF16 = jnp.bfloat16

D_MODEL = 2048
GDN_HEADS = 8
HEAD_DIM = 128
GDN_WIDTH = GDN_HEADS * HEAD_DIM
GDN_CONV = 4
GDN_CHUNK = 64
MOBA_HEADS = 8
MOBA_WIDTH = MOBA_HEADS * HEAD_DIM
MOBA_BLOCK = 256
MOBA_TOPK = 3
ROPE_THETA = 500000.0
ROPE_DIM = HEAD_DIM // 4
ROPE_HALF = ROPE_DIM // 2
D_FF = 5632
FFN_CONV = 3
NORM_EPS = 1e-6
L2_EPS = 1e-6
NEG_INF = -1e30
NEG_LOG2_E = -math.log2(math.e)

V7X_LANES = 128
V7X_SUBLANES = 8
V7X_VMEM_LIMIT_BYTES = 56 * 1024 * 1024

GDN_GROUP = 4 * GDN_CHUNK
MOBA_Q_SCALE = HEAD_DIM ** -0.5 * math.log2(math.e)


def _cparams(*sem):
    return pltpu.CompilerParams(dimension_semantics=sem, vmem_limit_bytes=V7X_VMEM_LIMIT_BYTES)


def _dot(a, b):
    return jnp.dot(a, b, preferred_element_type=F32)


def _dot_nt(a, b):
    return lax.dot_general(a, b, (((1,), (1,)), ((), ())), preferred_element_type=F32)


def _dot_hi(a, b):
    return jnp.dot(a, b, preferred_element_type=F32, precision=lax.Precision.HIGHEST)


def _split3(x):
    hi = x.astype(BF16)
    r1 = x - hi.astype(F32)
    mid = r1.astype(BF16)
    lo = (r1 - mid.astype(F32)).astype(BF16)
    return hi, mid, lo


def _sigmoid(x):
    return 1.0 / (1.0 + jnp.exp2(x * NEG_LOG2_E))


def _silu(x):
    return x * _sigmoid(x)


def _shift_rows(cur, prev8, s):
    if s == 0:
        return cur
    rolled = pltpu.roll(cur, s, axis=0)
    rolled_prev = pltpu.roll(prev8, s, axis=0)
    row = lax.broadcasted_iota(jnp.int32, prev8.shape, 0)
    first = jnp.where(row < s, rolled_prev, rolled[0:V7X_SUBLANES])
    return jnp.concatenate([first, rolled[V7X_SUBLANES:]], axis=0)


def _causal_conv(cur, prev8, cw, width):
    y = cw[width - 1:width, :] * cur
    for j in range(width - 1):
        y = y + cw[j:j + 1, :] * _shift_rows(cur, prev8, width - 1 - j)
    return y


def _norm_kernel(x_ref, w_ref, h_ref):
    x = x_ref[...]
    y = x * lax.rsqrt(jnp.mean(x * x, axis=-1, keepdims=True) + NORM_EPS) * w_ref[...]
    h_ref[...] = y.astype(h_ref.dtype)


def _norm_cast(x2d, w, tm=1024):
    m, d = x2d.shape
    return pl.pallas_call(
        _norm_kernel,
        grid=(m // tm,),
        in_specs=[pl.BlockSpec((tm, d), lambda i: (i, 0)),
                  pl.BlockSpec((1, d), lambda i: (0, 0))],
        out_specs=pl.BlockSpec((tm, d), lambda i: (i, 0)),
        out_shape=jax.ShapeDtypeStruct((m, d), BF16),
        compiler_params=_cparams("arbitrary"),
        name="norm1",
    )(x2d, w.reshape(1, d))


def _final_norm_kernel(x_ref, w_ref, o_ref):
    x = x_ref[...]
    o_ref[...] = x * lax.rsqrt(jnp.mean(x * x, axis=-1, keepdims=True) + NORM_EPS) * w_ref[...]


def _final_norm(x2d, w, tm=512):
    m, d = x2d.shape
    return pl.pallas_call(
        _final_norm_kernel,
        grid=(m // tm,),
        in_specs=[pl.BlockSpec((tm, d), lambda i: (i, 0)),
                  pl.BlockSpec((1, d), lambda i: (0, 0))],
        out_specs=pl.BlockSpec((tm, d), lambda i: (i, 0)),
        out_shape=jax.ShapeDtypeStruct((m, d), F32),
        compiler_params=_cparams("arbitrary"),
        name="final_norm",
    )(x2d, w.reshape(1, d))


def _proj_t_kernel(h_ref, wt_ref, o_ref, w16_ref, *, act):
    @pl.when(pl.program_id(1) == 0)
    def _():
        w16_ref[...] = wt_ref[...].T.astype(BF16)

    acc = _dot(h_ref[...], w16_ref[...])
    if act == "sigmoid":
        acc = _sigmoid(acc)
    o_ref[...] = acc.astype(o_ref.dtype)


def _proj_t(h, wt, row0, n, *, act, out_dtype, tm, tn, name):
    m, k = h.shape
    return pl.pallas_call(
        functools.partial(_proj_t_kernel, act=act),
        grid=(n // tn, m // tm),
        in_specs=[pl.BlockSpec((tm, k), lambda j, i: (i, 0)),
                  pl.BlockSpec((pl.Element(tn), pl.Element(k)), lambda j, i: (pl.multiple_of(row0 + j * tn, 16), 0))],
        out_specs=pl.BlockSpec((tm, tn), lambda j, i: (i, j)),
        out_shape=jax.ShapeDtypeStruct((m, n), out_dtype),
        scratch_shapes=[pltpu.VMEM((k, tn), BF16)],
        compiler_params=_cparams("arbitrary", "arbitrary"),
        name=name,
    )(h, wt)


def _gdn_proj_kernel(h_ref, w_ref, cw_ref, o_ref, w16_ref, carry_ref, *, tiles_per_seq, tiles_per_part, sub, l2norm):
    n = pl.program_id(0)
    m = pl.program_id(1)
    tn = w_ref.shape[0]

    @pl.when(m == 0)
    def _():
        w16_ref[...] = w_ref[...].T.astype(BF16)

    @pl.when(m % tiles_per_seq == 0)
    def _():
        carry_ref[...] = jnp.zeros_like(carry_ref)

    part = n // tiles_per_part
    q_scale = jnp.where(part == 0, HEAD_DIM ** -0.5, 1.0).astype(F32)
    prev = carry_ref[...]
    for r in range(h_ref.shape[0] // sub):
        acc = _dot(h_ref[r * sub:(r + 1) * sub, :], w16_ref[...])
        y = _silu(_causal_conv(acc, prev, cw_ref[...], GDN_CONV))
        prev = acc[sub - V7X_SUBLANES:sub, :]
        if l2norm:
            heads = []
            for hd in range(tn // HEAD_DIM):
                blk = y[:, hd * HEAD_DIM:(hd + 1) * HEAD_DIM]
                heads.append(blk * (lax.rsqrt(jnp.sum(blk * blk, axis=-1, keepdims=True) + L2_EPS) * q_scale))
            y = jnp.concatenate(heads, axis=1)
        o_ref[r * sub:(r + 1) * sub, :] = y
    carry_ref[...] = prev


def _gdn_proj(h, w_in, conv_w, seq, *, first, parts, l2norm, name, tm=1024, tn=1024, sub=1024):
    m, k = h.shape
    n = parts * GDN_WIDTH
    off = first * GDN_WIDTH // tn
    return pl.pallas_call(
        functools.partial(_gdn_proj_kernel, tiles_per_seq=seq // tm, tiles_per_part=GDN_WIDTH // tn, sub=sub,
                          l2norm=l2norm),
        grid=(n // tn, m // tm),
        in_specs=[pl.BlockSpec((tm, k), lambda j, i: (i, 0)),
                  pl.BlockSpec((tn, k), lambda j, i: (off + j, 0)),
                  pl.BlockSpec((GDN_CONV, tn), lambda j, i: (0, off + j))],
        out_specs=pl.BlockSpec((tm, tn), lambda j, i: (i, j)),
        out_shape=jax.ShapeDtypeStruct((m, n), F32),
        scratch_shapes=[pltpu.VMEM((k, tn), BF16),
                        pltpu.VMEM((V7X_SUBLANES, tn), F32)],
        compiler_params=_cparams("arbitrary", "arbitrary"),
        name=name,
    )(h, w_in, conv_w)


def _moba_proj_kernel(h_ref, wt_ref, pos_ref, freq_ref, o_ref, w16_ref):
    part = pl.program_id(0)

    @pl.when(pl.program_id(1) == 0)
    def _():
        w16_ref[...] = wt_ref[...].astype(BF16)

    scale = jnp.where(part == 0, MOBA_Q_SCALE, 1.0).astype(F32)
    acc = _dot_nt(w16_ref[...], h_ref[...]) * scale
    ang = freq_ref[...] * pos_ref[...].astype(F32)
    roped = part < 2
    cos = jnp.where(roped, jnp.cos(ang), 1.0)
    sin = jnp.where(roped, jnp.sin(ang), 0.0)
    rows = []
    for hd in range(acc.shape[0] // HEAD_DIM):
        base = hd * HEAD_DIM
        x1 = acc[base:base + ROPE_HALF]
        x2 = acc[base + ROPE_HALF:base + ROPE_DIM]
        rows.append(x1 * cos - x2 * sin)
        rows.append(x2 * cos + x1 * sin)
        rows.append(acc[base + ROPE_DIM:base + HEAD_DIM])
    o_ref[...] = jnp.concatenate(rows, axis=0).astype(o_ref.dtype)


def _moba_proj(h, wt, row0, positions, inv_freq, batch, seq, tt=2048):
    m, k = h.shape
    tr = MOBA_WIDTH
    n = 3 * MOBA_WIDTH
    per_seq = seq // tt
    return pl.pallas_call(
        _moba_proj_kernel,
        grid=(n // tr, m // tt),
        in_specs=[pl.BlockSpec((tt, k), lambda r, i: (i, 0)),
                  pl.BlockSpec((pl.Element(tr), pl.Element(k)), lambda r, i: (pl.multiple_of(row0 + r * tr, 16), 0)),
                  pl.BlockSpec((None, 1, tt), lambda r, i: (i // per_seq, 0, i % per_seq)),
                  pl.BlockSpec((ROPE_HALF, 1), lambda r, i: (0, 0))],
        out_specs=pl.BlockSpec((None, tr, tt), lambda r, i: (i // per_seq, r, i % per_seq)),
        out_shape=jax.ShapeDtypeStruct((batch, n, seq), BF16),
        scratch_shapes=[pltpu.VMEM((tr, k), BF16)],
        compiler_params=_cparams("arbitrary", "arbitrary"),
        name="moba_qkv_proj",
    )(h, wt, positions.reshape(batch, 1, seq), inv_freq)


def _gdn_kernel(q_ref, k_ref, v_ref, z_ref, sm_ref, a_ref, dt_ref, nw_ref, o_ref,
                state_ref, lbd_ref, subd_ref, *, groups, heads_per_step):
    t = pl.program_id(2)
    g = GDN_GROUP
    c = GDN_CHUNK

    @pl.when(t == 0)
    def _():
        state_ref[...] = jnp.zeros_like(state_ref)

    ri = lax.broadcasted_iota(jnp.int32, (g, g), 0)
    ci = lax.broadcasted_iota(jnp.int32, (g, g), 1)
    shift = int(math.log2(c))
    same = jnp.right_shift(ri, shift) == jnp.right_shift(ci, shift)
    lbd_ref[...] = jnp.where(same & (ci <= ri), 1.0, 0.0)
    subd_ref[...] = jnp.where(same & (ri > ci), 1.0, 0.0)
    lane = lax.broadcasted_iota(jnp.int32, (g, V7X_LANES), 1)
    eye = jnp.where(ri == ci, 1.0, 0.0)

    def group_body(gi, carry):
        hs = range(heads_per_step)
        nch = g // c
        r0 = pl.multiple_of(gi * g, g)
        sm = sm_ref[pl.ds(r0, g), :]
        beta_all = _sigmoid(sm)
        xs = sm + dt_ref[...]
        softplus = jnp.maximum(xs, 0.0) + jnp.log1p(jnp.exp(-jnp.abs(xs)))
        g_all = -jnp.exp(a_ref[...]) * softplus
        lbd = lbd_ref[...]
        lbd16 = lbd.astype(BF16)
        strict = subd_ref[...]
        cols = [slice(hh * HEAD_DIM, (hh + 1) * HEAD_DIM) for hh in hs]
        heads = [pl.program_id(1) * heads_per_step + hh for hh in hs]

        k = [k_ref[pl.ds(r0, g), cols[hh]] for hh in hs]
        k16 = [k[hh].astype(BF16) for hh in hs]
        beta_b = [jnp.broadcast_to(jnp.sum(jnp.where(lane == heads[hh], beta_all, 0.0), axis=-1, keepdims=True),
                                   (g, HEAD_DIM)) for hh in hs]
        g_b = [jnp.broadcast_to(jnp.sum(jnp.where(lane == GDN_HEADS + heads[hh], g_all, 0.0), axis=-1, keepdims=True),
                                (g, HEAD_DIM)) for hh in hs]
        kb = [k[hh] * beta_b[hh] for hh in hs]

        cs = [_dot(lbd16, jnp.concatenate(_split3(g_b[hh]), axis=1)) for hh in hs]
        kq = [_dot_nt(jnp.concatenate([kb[hh].astype(BF16), q_ref[pl.ds(r0, g), cols[hh]].astype(BF16)], axis=0), k16[hh])
              for hh in hs]
        kk = [kq[hh][:g] for hh in hs]
        qk_raw = [kq[hh][g:] for hh in hs]
        gc_b = [cs[hh][:, :HEAD_DIM] + cs[hh][:, HEAD_DIM:2 * HEAD_DIM] + cs[hh][:, 2 * HEAD_DIM:] for hh in hs]
        gamma = []
        for hh in hs:
            gc_row = gc_b[hh].T[0:1, :]
            dmat = jnp.where(lbd > 0.0, jnp.concatenate([gc_b[hh], gc_b[hh]], axis=1) - gc_row, 0.0)
            gamma.append(jnp.exp(dmat))
        eg = [jnp.exp(gc_b[hh]) for hh in hs]

        p = [-jnp.where(strict > 0.0, kk[hh] * gamma[hh], 0.0) for hh in hs]
        t_mat = [eye + p[hh] for hh in hs]
        p16 = [p[hh].astype(BF16) for hh in hs]
        p16 = [_dot(p16[hh], p16[hh]).astype(BF16) for hh in hs]
        for _ in range(4):
            st = [_dot(jnp.concatenate([p16[hh], t_mat[hh].astype(BF16)], axis=0), p16[hh]) for hh in hs]
            p16 = [st[hh][:g].astype(BF16) for hh in hs]
            t_mat = [t_mat[hh] + st[hh][g:] for hh in hs]
        tp = [_dot(t_mat[hh].astype(BF16), p16[hh]) for hh in hs]
        t_mat = [t_mat[hh] + tp[hh] for hh in hs]
        uw = [_dot(t_mat[hh].astype(BF16),
                   jnp.concatenate([v_ref[pl.ds(r0, g), cols[hh]] * beta_b[hh], kb[hh] * eg[hh]], axis=1).astype(BF16))
              for hh in hs]
        u = [uw[hh][:, :HEAD_DIM] for hh in hs]
        w16 = [uw[hh][:, HEAD_DIM:].astype(BF16) for hh in hs]
        qk16 = [jnp.where(lbd > 0.0, qk_raw[hh] * gamma[hh], 0.0).astype(BF16) for hh in hs]
        qg = [(q_ref[pl.ds(r0, g), cols[hh]] * eg[hh]).astype(BF16) for hh in hs]
        kd_t = []
        for hh in hs:
            gc_last = jnp.concatenate(
                [jnp.broadcast_to(gc_b[hh][(ch + 1) * c - 1:(ch + 1) * c, :], (c, HEAD_DIM)) for ch in range(nch)],
                axis=0)
            kd_t.append((k[hh] * jnp.exp(gc_last - gc_b[hh])).T.astype(BF16))

        state = [state_ref[hh] for hh in hs]
        outs = [[] for _ in hs]
        zeros_c = jnp.zeros((c, HEAD_DIM), F32)
        for ch in range(nch):
            rows = slice(ch * c, (ch + 1) * c)
            s16 = [state[hh].astype(BF16) for hh in hs]
            wq = [_dot(jnp.concatenate([w16[hh][rows], qg[hh][rows]], axis=0), s16[hh]) for hh in hs]
            ws = [wq[hh][:c] for hh in hs]
            qs = [wq[hh][c:] for hh in hs]
            vn_all = [jnp.concatenate([zeros_c] * ch + [u[hh][rows] - ws[hh]] + [zeros_c] * (nch - 1 - ch),
                                      axis=0).astype(BF16) for hh in hs]
            so = [_dot(jnp.concatenate([kd_t[hh], qk16[hh][rows]], axis=0), vn_all[hh]) for hh in hs]
            sv = [so[hh][:HEAD_DIM] for hh in hs]
            ov = [so[hh][HEAD_DIM:] for hh in hs]
            for hh in hs:
                outs[hh].append(qs[hh] + ov[hh])
                state[hh] = state[hh] * eg[hh][(ch + 1) * c - 1:(ch + 1) * c, :] + sv[hh]
        for hh in hs:
            state_ref[hh] = state[hh]
            o = jnp.concatenate(outs[hh], axis=0)
            o = o * lax.rsqrt(jnp.mean(o * o, axis=-1, keepdims=True) + NORM_EPS) * nw_ref[...]
            o_ref[pl.ds(r0, g), cols[hh]] = (o * _silu(z_ref[pl.ds(r0, g), cols[hh]])).astype(o_ref.dtype)
        return carry

    lax.fori_loop(0, groups, group_body, 0)


def _gdn(qk, v, zs, a_pad, dt_pad, norm_w, batch, seq, ts=512, heads_per_step=8):
    m = qk.shape[0]
    tiles = seq // ts
    hp = heads_per_step
    hb = GDN_HEADS // hp
    wd = hp * HEAD_DIM
    row = lambda b, h, t: b * tiles + t
    return pl.pallas_call(
        functools.partial(_gdn_kernel, groups=ts // GDN_GROUP, heads_per_step=hp),
        grid=(batch, hb, tiles),
        in_specs=[pl.BlockSpec((ts, wd), lambda b, h, t: (row(b, h, t), h)),
                  pl.BlockSpec((ts, wd), lambda b, h, t: (row(b, h, t), hb + h)),
                  pl.BlockSpec((ts, wd), lambda b, h, t: (row(b, h, t), h)),
                  pl.BlockSpec((ts, wd), lambda b, h, t: (row(b, h, t), h)),
                  pl.BlockSpec((ts, V7X_LANES), lambda b, h, t: (row(b, h, t), GDN_WIDTH // V7X_LANES)),
                  pl.BlockSpec((1, V7X_LANES), lambda b, h, t: (0, 0)),
                  pl.BlockSpec((1, V7X_LANES), lambda b, h, t: (0, 0)),
                  pl.BlockSpec((1, HEAD_DIM), lambda b, h, t: (0, 0))],
        out_specs=pl.BlockSpec((ts, wd), lambda b, h, t: (row(b, h, t), h)),
        out_shape=jax.ShapeDtypeStruct((m, GDN_WIDTH), BF16),
        scratch_shapes=[pltpu.VMEM((hp, HEAD_DIM, HEAD_DIM), F32),
                        pltpu.VMEM((GDN_GROUP, GDN_GROUP), F32),
                        pltpu.VMEM((GDN_GROUP, GDN_GROUP), F32)],
        compiler_params=_cparams("arbitrary", "arbitrary", "arbitrary"),
        name="gated_delta_rule",
    )(qk, qk, v, zs, zs, a_pad, dt_pad, norm_w)


def _moba_kernel(q_ref, k_ref, v_ref, o_ref, kn_ref, km_ref, s_ref, p_ref, bias_ref, *, nblk):
    blk = MOBA_BLOCK
    for j in range(nblk):
        kt = k_ref[:, j * blk:(j + 1) * blk].astype(F32)
        kn = kt.T
        kn_ref[j * blk:(j + 1) * blk, 0:HEAD_DIM] = kn.astype(BF16)
        kn_ref[j * blk:(j + 1) * blk, HEAD_DIM:] = jnp.where(
            lax.broadcasted_iota(jnp.int32, (blk, HEAD_DIM), 1) == j, 1.0, 0.0).astype(BF16)
        km_ref[j:j + 1, :] = jnp.sum(kn, axis=0, keepdims=True) * (1.0 / blk)

    bias_ref[...] = jnp.zeros_like(bias_ref)

    key_i = lax.broadcasted_iota(jnp.int32, (blk, blk), 0)
    qry_i = lax.broadcasted_iota(jnp.int32, (blk, blk), 1)
    causal_bias = jnp.where(key_i <= qry_i, 0.0, NEG_INF)
    groups = blk // V7X_SUBLANES

    def scores(i):
        s_buf = s_ref.at[i % 2]
        qt = q_ref[:, i * blk:(i + 1) * blk]
        if i > MOBA_TOPK:
            gate = _dot_hi(km_ref[...], qt.astype(F32))
            n_iota = lax.broadcasted_iota(jnp.int32, (nblk, blk), 0)
            valid = n_iota < i
            for n in range(i):
                g_n = gate[n:n + 1, :]
                beats = valid & ((gate > g_n) | ((gate == g_n) & (n_iota < n)))
                cnt = jnp.sum(jnp.where(beats, 1.0, 0.0), axis=0, keepdims=True)
                bias_ref[n:n + 1, :] = jnp.where(cnt < float(MOBA_TOPK), 0.0, NEG_INF)
            bias_ref[i:i + 1, :] = jnp.zeros((1, blk), F32)
            qa = jnp.concatenate([qt, bias_ref[...].astype(BF16),
                                  jnp.zeros((HEAD_DIM - nblk, blk), BF16)], axis=0)
        m_acc = None
        for j in range(i + 1):
            if i > MOBA_TOPK:
                s = _dot(kn_ref[j * blk:(j + 1) * blk, :], qa)
            else:
                s = _dot(kn_ref[j * blk:(j + 1) * blk, 0:HEAD_DIM], qt)
            if j == i:
                s = s + causal_bias
            s_buf[j * blk:(j + 1) * blk, :] = s
            part = jnp.max(s.reshape(groups, V7X_SUBLANES, blk), axis=0)
            m_acc = part if m_acc is None else jnp.maximum(m_acc, part)
        return jnp.max(m_acc, axis=0, keepdims=True)

    def softmax_weights(i, m_row):
        s_buf = s_ref.at[i % 2]
        p_buf = p_ref.at[i % 2]
        l_acc = None
        for j in range(i + 1):
            p = jnp.exp2(s_buf[j * blk:(j + 1) * blk, :] - m_row)
            part = jnp.sum(p.reshape(groups, V7X_SUBLANES, blk), axis=0)
            l_acc = part if l_acc is None else l_acc + part
            p_buf[j * blk:(j + 1) * blk, :] = p.astype(BF16)
        return jnp.sum(l_acc, axis=0, keepdims=True)

    def weighted_values(i, l_row):
        p_buf = p_ref.at[i % 2]
        o_t = _dot(v_ref[:, 0:(i + 1) * blk], p_buf[0:(i + 1) * blk, :]) / l_row
        o_ref[i * blk:(i + 1) * blk, :] = o_t.T.astype(o_ref.dtype)

    m_row = {0: scores(0)}
    l_row = {}
    for i in range(nblk):
        if i + 1 < nblk:
            m_row[i + 1] = scores(i + 1)
        if i >= 1:
            weighted_values(i - 1, l_row.pop(i - 1))
        l_row[i] = softmax_weights(i, m_row.pop(i))
    weighted_values(nblk - 1, l_row.pop(nblk - 1))


def _moba(qkv_t, batch, seq):
    nblk = seq // MOBA_BLOCK
    hb = MOBA_HEADS
    return pl.pallas_call(
        functools.partial(_moba_kernel, nblk=nblk),
        grid=(batch, MOBA_HEADS),
        in_specs=[pl.BlockSpec((None, HEAD_DIM, seq), lambda b, h: (b, h, 0)),
                  pl.BlockSpec((None, HEAD_DIM, seq), lambda b, h: (b, hb + h, 0)),
                  pl.BlockSpec((None, HEAD_DIM, seq), lambda b, h: (b, 2 * hb + h, 0))],
        out_specs=pl.BlockSpec((seq, HEAD_DIM), lambda b, h: (b, h)),
        out_shape=jax.ShapeDtypeStruct((batch * seq, MOBA_WIDTH), BF16),
        scratch_shapes=[pltpu.VMEM((seq, 2 * HEAD_DIM), BF16),
                        pltpu.VMEM((nblk, HEAD_DIM), F32),
                        pltpu.VMEM((2, seq, MOBA_BLOCK), F32),
                        pltpu.VMEM((2, seq, MOBA_BLOCK), BF16),
                        pltpu.VMEM((nblk, MOBA_BLOCK), F32)],
        compiler_params=_cparams("arbitrary", "arbitrary"),
        name="moba_attention",
    )(qkv_t, qkv_t, qkv_t)


def _mix_kernel(a_ref, b_ref, ga_ref, gb_ref, wa_ref, wb_ref, wo_ref, x_ref, nw_ref, x1_ref, h_ref, *, sub):
    for r in range(a_ref.shape[0] // sub):
        rows = slice(r * sub, (r + 1) * sub)
        ya = _dot(a_ref[rows, :], wa_ref[...])
        yb = _dot(b_ref[rows, :], wb_ref[...])
        merged = (ga_ref[rows, :] * ya + gb_ref[rows, :] * yb).astype(BF16)
        x1 = x_ref[rows, :] + _dot(merged, wo_ref[...])
        x1_ref[rows, :] = x1
        h = x1 * lax.rsqrt(jnp.mean(x1 * x1, axis=-1, keepdims=True) + NORM_EPS) * nw_ref[...]
        h_ref[rows, :] = h.astype(h_ref.dtype)


def _mix(ya_in, yb_in, gates, wa, wb, wo, x2d, nw, tm=512, sub=512):
    m, k = ya_in.shape
    n = wo.shape[1]
    return pl.pallas_call(
        functools.partial(_mix_kernel, sub=sub),
        grid=(m // tm,),
        in_specs=[pl.BlockSpec((tm, k), lambda i: (i, 0)),
                  pl.BlockSpec((tm, k), lambda i: (i, 0)),
                  pl.BlockSpec((tm, n), lambda i: (i, 0)),
                  pl.BlockSpec((tm, n), lambda i: (i, 1)),
                  pl.BlockSpec((k, n), lambda i: (0, 0)),
                  pl.BlockSpec((k, n), lambda i: (0, 0)),
                  pl.BlockSpec((n, n), lambda i: (0, 0)),
                  pl.BlockSpec((tm, n), lambda i: (i, 0)),
                  pl.BlockSpec((1, n), lambda i: (0, 0))],
        out_specs=[pl.BlockSpec((tm, n), lambda i: (i, 0)),
                   pl.BlockSpec((tm, n), lambda i: (i, 0))],
        out_shape=[jax.ShapeDtypeStruct((m, n), F32),
                   jax.ShapeDtypeStruct((m, n), BF16)],
        compiler_params=_cparams("arbitrary"),
        name="branch_mix_out_proj_norm2",
    )(ya_in, yb_in, gates, gates, wa, wb, wo, x2d, nw.reshape(1, n))


def _upproj_kernel(h_ref, wg_ref, wv_ref, cg_ref, cv_ref, bg_ref, bv_ref, o_ref,
                   w_ref, carry_ref, *, tiles_per_seq, sub):
    m = pl.program_id(1)
    tn = wg_ref.shape[1]

    @pl.when(m == 0)
    def _():
        w_ref[:, :tn] = wg_ref[...].astype(BF16)
        w_ref[:, tn:] = wv_ref[...].astype(BF16)

    @pl.when(m % tiles_per_seq == 0)
    def _():
        carry_ref[...] = jnp.zeros_like(carry_ref)

    cw = jnp.concatenate([cg_ref[...], cv_ref[...]], axis=1)
    bias = jnp.concatenate([bg_ref[...], bv_ref[...]], axis=1)
    prev = carry_ref[...]
    for r in range(h_ref.shape[0] // sub):
        u = _dot(h_ref[r * sub:(r + 1) * sub, :], w_ref[...])
        y = _causal_conv(u, prev, cw, FFN_CONV) + bias
        prev = u[sub - V7X_SUBLANES:sub, :]
        o_ref[r * sub:(r + 1) * sub, :] = (_silu(y[:, :tn]) * y[:, tn:]).astype(o_ref.dtype)
    carry_ref[...] = prev


def _upproj(h, w_up, conv_w, conv_b, seq, tm=2048, tn=512, sub=1024):
    m, k = h.shape
    nb = D_FF // tn
    return pl.pallas_call(
        functools.partial(_upproj_kernel, tiles_per_seq=seq // tm, sub=sub),
        grid=(nb, m // tm),
        in_specs=[pl.BlockSpec((tm, k), lambda j, i: (i, 0)),
                  pl.BlockSpec((k, tn), lambda j, i: (0, j)),
                  pl.BlockSpec((k, tn), lambda j, i: (0, nb + j)),
                  pl.BlockSpec((FFN_CONV, tn), lambda j, i: (0, j)),
                  pl.BlockSpec((FFN_CONV, tn), lambda j, i: (0, nb + j)),
                  pl.BlockSpec((1, tn), lambda j, i: (0, j)),
                  pl.BlockSpec((1, tn), lambda j, i: (0, nb + j))],
        out_specs=pl.BlockSpec((tm, tn), lambda j, i: (i, j)),
        out_shape=jax.ShapeDtypeStruct((m, D_FF), BF16),
        scratch_shapes=[pltpu.VMEM((k, 2 * tn), BF16),
                        pltpu.VMEM((V7X_SUBLANES, 2 * tn), F32)],
        compiler_params=_cparams("arbitrary", "arbitrary"),
        name="up_proj_conv_gate",
    )(h, w_up, w_up, conv_w, conv_w, conv_b, conv_b)


def _downproj_kernel(a_ref, w_ref, x_ref, o_ref):
    o_ref[...] = x_ref[...] + _dot(a_ref[...], w_ref[...])


def _downproj(a, w, x1, tm=512, tn=1024):
    m, k = a.shape
    n = w.shape[1]
    return pl.pallas_call(
        _downproj_kernel,
        grid=(n // tn, m // tm),
        in_specs=[pl.BlockSpec((tm, k), lambda j, i: (i, 0)),
                  pl.BlockSpec((k, tn), lambda j, i: (0, j)),
                  pl.BlockSpec((tm, tn), lambda j, i: (i, j))],
        out_specs=pl.BlockSpec((tm, tn), lambda j, i: (i, j)),
        out_shape=jax.ShapeDtypeStruct((m, n), F32),
        compiler_params=_cparams("arbitrary", "arbitrary"),
        name="down_proj",
    )(a, w, x1)


def _downproj_norm_kernel(a_ref, w_ref, x_ref, nw_ref, o_ref):
    x2 = x_ref[...] + _dot(a_ref[...], w_ref[...])
    o_ref[...] = x2 * lax.rsqrt(jnp.mean(x2 * x2, axis=-1, keepdims=True) + NORM_EPS) * nw_ref[...]


def _downproj_norm(a, w, x1, nw, tm=512):
    m, k = a.shape
    n = w.shape[1]
    return pl.pallas_call(
        _downproj_norm_kernel,
        grid=(m // tm,),
        in_specs=[pl.BlockSpec((tm, k), lambda i: (i, 0)),
                  pl.BlockSpec((k, n), lambda i: (0, 0)),
                  pl.BlockSpec((tm, n), lambda i: (i, 0)),
                  pl.BlockSpec((1, n), lambda i: (0, 0))],
        out_specs=pl.BlockSpec((tm, n), lambda i: (i, 0)),
        out_shape=jax.ShapeDtypeStruct((m, n), F32),
        compiler_params=_cparams("arbitrary"),
        name="down_proj_final_norm",
    )(a, w, x1, nw.reshape(1, n))


def _pad_lanes(vec, offset):
    return jnp.zeros((1, V7X_LANES), F32).at[0, offset:offset + vec.shape[0]].set(vec.astype(F32))


def kernel(x, positions, ln1, w_in, gdn_conv, gdn_a_log, gdn_dt_bias, gdn_norm, w_branch_a, w_branch_b,
           w_out, ln2, w_up, ffn_conv, ffn_conv_bias, w_down, final_norm):
    batch, seq, d = x.shape
    m = batch * seq
    assert d == D_MODEL and seq % 2048 == 0 and seq % MOBA_BLOCK == 0, (x.shape,)
    assert w_in.shape[1:] == (D_MODEL, 3 * GDN_WIDTH + 2 * GDN_HEADS + GDN_WIDTH + 3 * MOBA_WIDTH + 2 * D_MODEL), w_in.shape
    depth = ln1.shape[0]
    o_qkv_a = 3 * GDN_WIDTH
    o_small = o_qkv_a + 2 * GDN_HEADS
    o_z = o_small + GDN_WIDTH
    o_qkv_b = o_z + 3 * MOBA_WIDTH
    inv_freq = (ROPE_THETA ** (-jnp.arange(ROPE_HALF, dtype=F32) / ROPE_HALF)).reshape(ROPE_HALF, 1)

    x2d = x.reshape(m, d)
    for l in range(depth):
        wt = jnp.swapaxes(w_in[l], 0, 1)
        wt_zs = jnp.concatenate(
            [wt[o_small:o_z], jnp.pad(wt[o_qkv_a:o_small], ((0, V7X_LANES - 2 * GDN_HEADS), (0, 0)))], axis=0)

        h1 = _norm_cast(x2d, ln1[l])
        qk_a = _gdn_proj(h1, wt, gdn_conv[l], seq, first=0, parts=2, l2norm=True, name="gdn_qk_proj")
        v_a = _gdn_proj(h1, wt, gdn_conv[l], seq, first=2, parts=1, l2norm=False, name="gdn_v_proj")
        zs = _proj_t(h1, wt_zs, 0, GDN_WIDTH + V7X_LANES, act=None, out_dtype=F32, tm=1024,
                     tn=GDN_WIDTH + V7X_LANES, name="z_beta_decay_proj")
        gates = _proj_t(h1, wt, o_qkv_b, 2 * D_MODEL, act="sigmoid", out_dtype=BF16, tm=2048, tn=1024, name="gate_proj")
        qkv_b_t = _moba_proj(h1, wt, o_z, positions, inv_freq, batch, seq)

        gdn_out = _gdn(qk_a, v_a, zs, _pad_lanes(gdn_a_log[l], GDN_HEADS), _pad_lanes(gdn_dt_bias[l], GDN_HEADS),
                       gdn_norm[l].reshape(1, HEAD_DIM), batch, seq)
        attn = _moba(qkv_b_t, batch, seq)
        x1, h2 = _mix(gdn_out, attn, gates, w_branch_a[l].astype(BF16), w_branch_b[l].astype(BF16),
                      w_out[l].astype(BF16), x2d, ln2[l])
        act = _upproj(h2, w_up[l], ffn_conv[l], ffn_conv_bias[l].reshape(1, 2 * D_FF), seq)
        if l == depth - 1:
            return _downproj_norm(act, w_down[l].astype(BF16), x1, final_norm).reshape(batch, seq, d)
        x2d = _downproj(act, w_down[l].astype(BF16), x1)
    return _final_norm(x2d, final_norm).reshape(batch, seq, d)
```

```python
import functools
import math

import jax
import jax.numpy as jnp
import numpy as np
from jax import lax
from jax.experimental import pallas as pl
from jax.experimental.pallas import tpu as pltpu

F32 = jnp.float32
BF16 = jnp.bfloat16

D_MODEL = 2048
GDN_HEADS = 8
HEAD_DIM = 128
GDN_WIDTH = GDN_HEADS * HEAD_DIM
GDN_CONV = 4
GDN_CHUNK = 64
MOBA_HEADS = 8
MOBA_WIDTH = MOBA_HEADS * HEAD_DIM
MOBA_BLOCK = 256
MOBA_TOPK = 3
ROPE_THETA = 500000.0
ROPE_DIM = HEAD_DIM // 4
ROPE_HALF = ROPE_DIM // 2
D_FF = 5632
FFN_CONV = 3
NORM_EPS = 1e-6
L2_EPS = 1e-6
NEG_INF = -1e30
NEG_LOG2_E = -math.log2(math.e)

V7X_LANES = 128
V7X_SUBLANES = 8
V7X_VMEM_LIMIT_BYTES = 56 * 1024 * 1024

GDN_GROUP = 4 * GDN_CHUNK
MOBA_Q_SCALE = HEAD_DIM ** -0.5 * math.log2(math.e)


def _cparams(*sem):
    return pltpu.CompilerParams(dimension_semantics=sem, vmem_limit_bytes=V7X_VMEM_LIMIT_BYTES)


def _dot(a, b):
    return jnp.dot(a, b, preferred_element_type=F32)


def _dot_nt(a, b):
    return lax.dot_general(a, b, (((1,), (1,)), ((), ())), preferred_element_type=F32)


def _dot_hi(a, b):
    return jnp.dot(a, b, preferred_element_type=F32, precision=lax.Precision.HIGHEST)


def _sigmoid(x):
    return 1.0 / (1.0 + jnp.exp2(x * NEG_LOG2_E))


def _silu(x):
    return x * _sigmoid(x)


def _shift_rows(cur, prev8, s):
    if s == 0:
        return cur
    rolled = pltpu.roll(cur, s, axis=0)
    rolled_prev = pltpu.roll(prev8, s, axis=0)
    row = lax.broadcasted_iota(jnp.int32, prev8.shape, 0)
    first = jnp.where(row < s, rolled_prev, rolled[0:V7X_SUBLANES])
    return jnp.concatenate([first, rolled[V7X_SUBLANES:]], axis=0)


def _causal_conv(cur, prev8, cw, width):
    y = cw[width - 1:width, :] * cur
    for j in range(width - 1):
        y = y + cw[j:j + 1, :] * _shift_rows(cur, prev8, width - 1 - j)
    return y


def _norm_kernel(x_ref, w_ref, h_ref):
    x = x_ref[...]
    y = x * lax.rsqrt(jnp.mean(x * x, axis=-1, keepdims=True) + NORM_EPS) * w_ref[...]
    h_ref[...] = y.astype(h_ref.dtype)


def _norm_cast(x2d, w, tm=1024):
    m, d = x2d.shape
    return pl.pallas_call(
        _norm_kernel,
        grid=(m // tm,),
        in_specs=[pl.BlockSpec((tm, d), lambda i: (i, 0)),
                  pl.BlockSpec((1, d), lambda i: (0, 0))],
        out_specs=pl.BlockSpec((tm, d), lambda i: (i, 0)),
        out_shape=jax.ShapeDtypeStruct((m, d), BF16),
        compiler_params=_cparams("arbitrary"),
        name="norm1",
    )(x2d, w.reshape(1, d))


def _final_norm_kernel(x_ref, w_ref, o_ref):
    x = x_ref[...]
    o_ref[...] = x * lax.rsqrt(jnp.mean(x * x, axis=-1, keepdims=True) + NORM_EPS) * w_ref[...]


def _final_norm(x2d, w, tm=512):
    m, d = x2d.shape
    return pl.pallas_call(
        _final_norm_kernel,
        grid=(m // tm,),
        in_specs=[pl.BlockSpec((tm, d), lambda i: (i, 0)),
                  pl.BlockSpec((1, d), lambda i: (0, 0))],
        out_specs=pl.BlockSpec((tm, d), lambda i: (i, 0)),
        out_shape=jax.ShapeDtypeStruct((m, d), F32),
        compiler_params=_cparams("arbitrary"),
        name="final_norm",
    )(x2d, w.reshape(1, d))


def _proj_t_kernel(h_ref, wt_ref, o_ref, w16_ref, *, act):
    @pl.when(pl.program_id(1) == 0)
    def _():
        w16_ref[...] = wt_ref[...].T.astype(BF16)

    acc = _dot(h_ref[...], w16_ref[...])
    if act == "sigmoid":
        acc = _sigmoid(acc)
    o_ref[...] = acc.astype(o_ref.dtype)


def _proj_t(h, wt, row0, n, *, act, out_dtype, tm, tn, name):
    m, k = h.shape
    return pl.pallas_call(
        functools.partial(_proj_t_kernel, act=act),
        grid=(n // tn, m // tm),
        in_specs=[pl.BlockSpec((tm, k), lambda j, i: (i, 0)),
                  pl.BlockSpec((pl.Element(tn), pl.Element(k)), lambda j, i: (pl.multiple_of(row0 + j * tn, 16), 0))],
        out_specs=pl.BlockSpec((tm, tn), lambda j, i: (i, j)),
        out_shape=jax.ShapeDtypeStruct((m, n), out_dtype),
        scratch_shapes=[pltpu.VMEM((k, tn), BF16)],
        compiler_params=_cparams("arbitrary", "arbitrary"),
        name=name,
    )(h, wt)


def _gdn_proj_kernel(h_ref, w_ref, cw_ref, o_ref, w16_ref, carry_ref, *, tiles_per_seq, tiles_per_part, sub, l2norm):
    n = pl.program_id(0)
    m = pl.program_id(1)
    tn = w_ref.shape[0]

    @pl.when(m == 0)
    def _():
        w16_ref[...] = w_ref[...].T.astype(BF16)

    @pl.when(m % tiles_per_seq == 0)
    def _():
        carry_ref[...] = jnp.zeros_like(carry_ref)

    part = n // tiles_per_part
    q_scale = jnp.where(part == 0, HEAD_DIM ** -0.5, 1.0).astype(F32)
    prev = carry_ref[...]
    for r in range(h_ref.shape[0] // sub):
        acc = _dot(h_ref[r * sub:(r + 1) * sub, :], w16_ref[...])
        y = _silu(_causal_conv(acc, prev, cw_ref[...], GDN_CONV))
        prev = acc[sub - V7X_SUBLANES:sub, :]
        if l2norm:
            heads = []
            for hd in range(tn // HEAD_DIM):
                blk = y[:, hd * HEAD_DIM:(hd + 1) * HEAD_DIM]
                heads.append(blk * (lax.rsqrt(jnp.sum(blk * blk, axis=-1, keepdims=True) + L2_EPS) * q_scale))
            y = jnp.concatenate(heads, axis=1)
        o_ref[r * sub:(r + 1) * sub, :] = y
    carry_ref[...] = prev


def _gdn_proj(h, w_in, conv_w, seq, *, first, parts, l2norm, name, tm=1024, tn=1024, sub=1024):
    m, k = h.shape
    n = parts * GDN_WIDTH
    off = first * GDN_WIDTH // tn
    return pl.pallas_call(
        functools.partial(_gdn_proj_kernel, tiles_per_seq=seq // tm, tiles_per_part=GDN_WIDTH // tn, sub=sub,
                          l2norm=l2norm),
        grid=(n // tn, m // tm),
        in_specs=[pl.BlockSpec((tm, k), lambda j, i: (i, 0)),
                  pl.BlockSpec((tn, k), lambda j, i: (off + j, 0)),
                  pl.BlockSpec((GDN_CONV, tn), lambda j, i: (0, off + j))],
        out_specs=pl.BlockSpec((tm, tn), lambda j, i: (i, j)),
        out_shape=jax.ShapeDtypeStruct((m, n), F32),
        scratch_shapes=[pltpu.VMEM((k, tn), BF16),
                        pltpu.VMEM((V7X_SUBLANES, tn), F32)],
        compiler_params=_cparams("arbitrary", "arbitrary"),
        name=name,
    )(h, w_in, conv_w)


def _moba_proj_kernel(h_ref, wt_ref, pos_ref, freq_ref, o_ref, w16_ref):
    part = pl.program_id(0)

    @pl.when(pl.program_id(1) == 0)
    def _():
        w16_ref[...] = wt_ref[...].astype(BF16)

    scale = jnp.where(part == 0, MOBA_Q_SCALE, 1.0).astype(F32)
    acc = _dot_nt(w16_ref[...], h_ref[...]) * scale
    ang = freq_ref[...] * pos_ref[...].astype(F32)
    roped = part < 2
    cos = jnp.where(roped, jnp.cos(ang), 1.0)
    sin = jnp.where(roped, jnp.sin(ang), 0.0)
    rows = []
    for hd in range(acc.shape[0] // HEAD_DIM):
        base = hd * HEAD_DIM
        x1 = acc[base:base + ROPE_HALF]
        x2 = acc[base + ROPE_HALF:base + ROPE_DIM]
        rows.append(x1 * cos - x2 * sin)
        rows.append(x2 * cos + x1 * sin)
        rows.append(acc[base + ROPE_DIM:base + HEAD_DIM])
    o_ref[...] = jnp.concatenate(rows, axis=0).astype(o_ref.dtype)


def _moba_proj(h, wt, row0, positions, inv_freq, batch, seq, tt=2048):
    m, k = h.shape
    tr = MOBA_WIDTH
    n = 3 * MOBA_WIDTH
    per_seq = seq // tt
    return pl.pallas_call(
        _moba_proj_kernel,
        grid=(n // tr, m // tt),
        in_specs=[pl.BlockSpec((tt, k), lambda r, i: (i, 0)),
                  pl.BlockSpec((pl.Element(tr), pl.Element(k)), lambda r, i: (pl.multiple_of(row0 + r * tr, 16), 0)),
                  pl.BlockSpec((None, 1, tt), lambda r, i: (i // per_seq, 0, i % per_seq)),
                  pl.BlockSpec((ROPE_HALF, 1), lambda r, i: (0, 0))],
        out_specs=pl.BlockSpec((None, tr, tt), lambda r, i: (i // per_seq, r, i % per_seq)),
        out_shape=jax.ShapeDtypeStruct((batch, n, seq), BF16),
        scratch_shapes=[pltpu.VMEM((tr, k), BF16)],
        compiler_params=_cparams("arbitrary", "arbitrary"),
        name="moba_qkv_proj",
    )(h, wt, positions.reshape(batch, 1, seq), inv_freq)


def _gdn_kernel(q_ref, k_ref, v_ref, z_ref, sm_ref, a_ref, dt_ref, nw_ref, o_ref,
                state_ref, lbd_ref, subd_ref, *, groups, heads_per_step):
    t = pl.program_id(2)
    g = GDN_GROUP
    c = GDN_CHUNK

    @pl.when(t == 0)
    def _():
        state_ref[...] = jnp.zeros_like(state_ref)

    ri = lax.broadcasted_iota(jnp.int32, (g, g), 0)
    ci = lax.broadcasted_iota(jnp.int32, (g, g), 1)
    shift = int(math.log2(c))
    same = jnp.right_shift(ri, shift) == jnp.right_shift(ci, shift)
    lbd_ref[...] = jnp.where(same & (ci <= ri), 1.0, 0.0)
    subd_ref[...] = jnp.where(same & (ri > ci), 1.0, 0.0)
    lane = lax.broadcasted_iota(jnp.int32, (g, V7X_LANES), 1)
    eye = jnp.where(ri == ci, 1.0, 0.0)

    def group_body(gi, carry):
        hs = range(heads_per_step)
        nch = g // c
        r0 = pl.multiple_of(gi * g, g)
        sm = sm_ref[pl.ds(r0, g), :]
        beta_all = _sigmoid(sm)
        xs = sm + dt_ref[...]
        softplus = jnp.maximum(xs, 0.0) + jnp.log1p(jnp.exp(-jnp.abs(xs)))
        g_all = -jnp.exp(a_ref[...]) * softplus
        lbd = lbd_ref[...]
        strict = subd_ref[...]
        cols = [slice(hh * HEAD_DIM, (hh + 1) * HEAD_DIM) for hh in hs]
        heads = [pl.program_id(1) * heads_per_step + hh for hh in hs]

        k = [k_ref[pl.ds(r0, g), cols[hh]] for hh in hs]
        k16 = [k[hh].astype(BF16) for hh in hs]
        beta_b = [jnp.broadcast_to(jnp.sum(jnp.where(lane == heads[hh], beta_all, 0.0), axis=-1, keepdims=True),
                                   (g, HEAD_DIM)) for hh in hs]
        g_b = [jnp.broadcast_to(jnp.sum(jnp.where(lane == GDN_HEADS + heads[hh], g_all, 0.0), axis=-1, keepdims=True),
                                (g, HEAD_DIM)) for hh in hs]
        kb = [k[hh] * beta_b[hh] for hh in hs]

        pos = jnp.bitwise_and(lax.broadcasted_iota(jnp.int32, (g, HEAD_DIM), 0), c - 1)
        gc_b = []
        for hh in hs:
            acc = g_b[hh]
            step = 1
            while step < c:
                acc = acc + jnp.where(pos >= step, pltpu.roll(acc, step, axis=0), 0.0)
                step *= 2
            gc_b.append(acc)
        kq = [_dot_nt(jnp.concatenate([kb[hh].astype(BF16), q_ref[pl.ds(r0, g), cols[hh]].astype(BF16)], axis=0), k16[hh])
              for hh in hs]
        kk = [kq[hh][:g] for hh in hs]
        qk_raw = [kq[hh][g:] for hh in hs]
        gamma = []
        for hh in hs:
            gc_row = gc_b[hh].T[0:1, :]
            dmat = jnp.where(lbd > 0.0, jnp.concatenate([gc_b[hh], gc_b[hh]], axis=1) - gc_row, 0.0)
            gamma.append(jnp.exp(dmat))
        eg = [jnp.exp(gc_b[hh]) for hh in hs]

        p = [-jnp.where(strict > 0.0, kk[hh] * gamma[hh], 0.0) for hh in hs]
        t_mat = [eye + p[hh] for hh in hs]
        p16 = [p[hh].astype(BF16) for hh in hs]
        p16 = [_dot(p16[hh], p16[hh]).astype(BF16) for hh in hs]
        for _ in range(4):
            st = [_dot(jnp.concatenate([p16[hh], t_mat[hh].astype(BF16)], axis=0), p16[hh]) for hh in hs]
            p16 = [st[hh][:g].astype(BF16) for hh in hs]
            t_mat = [t_mat[hh] + st[hh][g:] for hh in hs]
        tp = [_dot(t_mat[hh].astype(BF16), p16[hh]) for hh in hs]
        t_mat = [t_mat[hh] + tp[hh] for hh in hs]
        uw = [_dot(t_mat[hh].astype(BF16),
                   jnp.concatenate([v_ref[pl.ds(r0, g), cols[hh]] * beta_b[hh], kb[hh] * eg[hh]], axis=1).astype(BF16))
              for hh in hs]
        u = [uw[hh][:, :HEAD_DIM] for hh in hs]
        w16 = [uw[hh][:, HEAD_DIM:].astype(BF16) for hh in hs]
        qk16 = [jnp.where(lbd > 0.0, qk_raw[hh] * gamma[hh], 0.0).astype(BF16) for hh in hs]
        qg = [(q_ref[pl.ds(r0, g), cols[hh]] * eg[hh]).astype(BF16) for hh in hs]
        kd_t = []
        for hh in hs:
            gc_last = jnp.concatenate(
                [jnp.broadcast_to(gc_b[hh][(ch + 1) * c - 1:(ch + 1) * c, :], (c, HEAD_DIM)) for ch in range(nch)],
                axis=0)
            kd_t.append((k[hh] * jnp.exp(gc_last - gc_b[hh])).T.astype(BF16))

        state = [state_ref[hh] for hh in hs]
        outs = [[] for _ in hs]
        zeros_c = jnp.zeros((c, HEAD_DIM), F32)
        for ch in range(nch):
            rows = slice(ch * c, (ch + 1) * c)
            s16 = [state[hh].astype(BF16) for hh in hs]
            wq = [_dot(jnp.concatenate([w16[hh][rows], qg[hh][rows]], axis=0), s16[hh]) for hh in hs]
            ws = [wq[hh][:c] for hh in hs]
            qs = [wq[hh][c:] for hh in hs]
            vn_all = [jnp.concatenate([zeros_c] * ch + [u[hh][rows] - ws[hh]] + [zeros_c] * (nch - 1 - ch),
                                      axis=0).astype(BF16) for hh in hs]
            so = [_dot(jnp.concatenate([kd_t[hh], qk16[hh][rows]], axis=0), vn_all[hh]) for hh in hs]
            sv = [so[hh][:HEAD_DIM] for hh in hs]
            ov = [so[hh][HEAD_DIM:] for hh in hs]
            for hh in hs:
                outs[hh].append(qs[hh] + ov[hh])
                state[hh] = state[hh] * eg[hh][(ch + 1) * c - 1:(ch + 1) * c, :] + sv[hh]
        for hh in hs:
            state_ref[hh] = state[hh]
            o = jnp.concatenate(outs[hh], axis=0)
            o = o * lax.rsqrt(jnp.mean(o * o, axis=-1, keepdims=True) + NORM_EPS) * nw_ref[...]
            o_ref[pl.ds(r0, g), cols[hh]] = (o * _silu(z_ref[pl.ds(r0, g), cols[hh]])).astype(o_ref.dtype)
        return carry

    lax.fori_loop(0, groups, group_body, 0)


def _gdn(qk, v, zs, a_pad, dt_pad, norm_w, batch, seq, ts=512, heads_per_step=8):
    m = qk.shape[0]
    tiles = seq // ts
    hp = heads_per_step
    hb = GDN_HEADS // hp
    wd = hp * HEAD_DIM
    row = lambda b, h, t: b * tiles + t
    return pl.pallas_call(
        functools.partial(_gdn_kernel, groups=ts // GDN_GROUP, heads_per_step=hp),
        grid=(batch, hb, tiles),
        in_specs=[pl.BlockSpec((ts, wd), lambda b, h, t: (row(b, h, t), h)),
                  pl.BlockSpec((ts, wd), lambda b, h, t: (row(b, h, t), hb + h)),
                  pl.BlockSpec((ts, wd), lambda b, h, t: (row(b, h, t), h)),
                  pl.BlockSpec((ts, wd), lambda b, h, t: (row(b, h, t), h)),
                  pl.BlockSpec((ts, V7X_LANES), lambda b, h, t: (row(b, h, t), GDN_WIDTH // V7X_LANES)),
                  pl.BlockSpec((1, V7X_LANES), lambda b, h, t: (0, 0)),
                  pl.BlockSpec((1, V7X_LANES), lambda b, h, t: (0, 0)),
                  pl.BlockSpec((1, HEAD_DIM), lambda b, h, t: (0, 0))],
        out_specs=pl.BlockSpec((ts, wd), lambda b, h, t: (row(b, h, t), h)),
        out_shape=jax.ShapeDtypeStruct((m, GDN_WIDTH), BF16),
        scratch_shapes=[pltpu.VMEM((hp, HEAD_DIM, HEAD_DIM), F32),
                        pltpu.VMEM((GDN_GROUP, GDN_GROUP), F32),
                        pltpu.VMEM((GDN_GROUP, GDN_GROUP), F32)],
        compiler_params=_cparams("arbitrary", "arbitrary", "arbitrary"),
        name="gated_delta_rule",
    )(qk, qk, v, zs, zs, a_pad, dt_pad, norm_w)


def _moba_kernel(q_ref, k_ref, v_ref, o_ref, kn_ref, km_ref, s_ref, p_ref, bias_ref, *, nblk):
    blk = MOBA_BLOCK
    for j in range(nblk):
        kt = k_ref[:, j * blk:(j + 1) * blk].astype(F32)
        kn = kt.T
        kn_ref[j * blk:(j + 1) * blk, 0:HEAD_DIM] = kn.astype(BF16)
        kn_ref[j * blk:(j + 1) * blk, HEAD_DIM:] = jnp.where(
            lax.broadcasted_iota(jnp.int32, (blk, HEAD_DIM), 1) == j, 1.0, 0.0).astype(BF16)
        km_ref[j:j + 1, :] = jnp.sum(kn, axis=0, keepdims=True) * (1.0 / blk)

    bias_ref[...] = jnp.zeros_like(bias_ref)

    key_i = lax.broadcasted_iota(jnp.int32, (blk, blk), 0)
    qry_i = lax.broadcasted_iota(jnp.int32, (blk, blk), 1)
    causal_bias = jnp.where(key_i <= qry_i, 0.0, NEG_INF)
    groups = blk // V7X_SUBLANES

    def scores(i):
        s_buf = s_ref.at[i % 2]
        qt = q_ref[:, i * blk:(i + 1) * blk]
        if i > MOBA_TOPK:
            gate = _dot_hi(km_ref[...], qt.astype(F32))
            n_iota = lax.broadcasted_iota(jnp.int32, (nblk, blk), 0)
            valid = n_iota < i
            for n in range(i):
                g_n = gate[n:n + 1, :]
                beats = valid & ((gate > g_n) | ((gate == g_n) & (n_iota < n)))
                cnt = jnp.sum(jnp.where(beats, 1.0, 0.0), axis=0, keepdims=True)
                bias_ref[n:n + 1, :] = jnp.where(cnt < float(MOBA_TOPK), 0.0, NEG_INF)
            bias_ref[i:i + 1, :] = jnp.zeros((1, blk), F32)
            qa = jnp.concatenate([qt, bias_ref[...].astype(BF16),
                                  jnp.zeros((HEAD_DIM - nblk, blk), BF16)], axis=0)
        m_acc = None
        for j in range(i + 1):
            if i > MOBA_TOPK:
                s = _dot(kn_ref[j * blk:(j + 1) * blk, :], qa)
            else:
                s = _dot(kn_ref[j * blk:(j + 1) * blk, 0:HEAD_DIM], qt)
            if j == i:
                s = s + causal_bias
            s_buf[j * blk:(j + 1) * blk, :] = s
            part = jnp.max(s.reshape(groups, V7X_SUBLANES, blk), axis=0)
            m_acc = part if m_acc is None else jnp.maximum(m_acc, part)
        return jnp.max(m_acc, axis=0, keepdims=True)

    def softmax_weights(i, m_row):
        s_buf = s_ref.at[i % 2]
        p_buf = p_ref.at[i % 2]
        l_acc = None
        for j in range(i + 1):
            p = jnp.exp2(s_buf[j * blk:(j + 1) * blk, :] - m_row)
            part = jnp.sum(p.reshape(groups, V7X_SUBLANES, blk), axis=0)
            l_acc = part if l_acc is None else l_acc + part
            p_buf[j * blk:(j + 1) * blk, :] = p.astype(BF16)
        return jnp.sum(l_acc, axis=0, keepdims=True)

    def weighted_values(i, l_row):
        p_buf = p_ref.at[i % 2]
        o_t = _dot(v_ref[:, 0:(i + 1) * blk], p_buf[0:(i + 1) * blk, :]) / l_row
        o_ref[i * blk:(i + 1) * blk, :] = o_t.T.astype(o_ref.dtype)

    m_row = {0: scores(0)}
    l_row = {}
    for i in range(nblk):
        if i + 1 < nblk:
            m_row[i + 1] = scores(i + 1)
        if i >= 1:
            weighted_values(i - 1, l_row.pop(i - 1))
        l_row[i] = softmax_weights(i, m_row.pop(i))
    weighted_values(nblk - 1, l_row.pop(nblk - 1))


def _moba(qkv_t, batch, seq):
    nblk = seq // MOBA_BLOCK
    hb = MOBA_HEADS
    return pl.pallas_call(
        functools.partial(_moba_kernel, nblk=nblk),
        grid=(batch, MOBA_HEADS),
        in_specs=[pl.BlockSpec((None, HEAD_DIM, seq), lambda b, h: (b, h, 0)),
                  pl.BlockSpec((None, HEAD_DIM, seq), lambda b, h: (b, hb + h, 0)),
                  pl.BlockSpec((None, HEAD_DIM, seq), lambda b, h: (b, 2 * hb + h, 0))],
        out_specs=pl.BlockSpec((seq, HEAD_DIM), lambda b, h: (b, h)),
        out_shape=jax.ShapeDtypeStruct((batch * seq, MOBA_WIDTH), BF16),
        scratch_shapes=[pltpu.VMEM((seq, 2 * HEAD_DIM), BF16),
                        pltpu.VMEM((nblk, HEAD_DIM), F32),
                        pltpu.VMEM((2, seq, MOBA_BLOCK), F32),
                        pltpu.VMEM((2, seq, MOBA_BLOCK), BF16),
                        pltpu.VMEM((nblk, MOBA_BLOCK), F32)],
        compiler_params=_cparams("arbitrary", "arbitrary"),
        name="moba_attention",
    )(qkv_t, qkv_t, qkv_t)


def _mix_kernel(a_ref, b_ref, ga_ref, gb_ref, wa_ref, wb_ref, wo_ref, x_ref, nw_ref, x1_ref, h_ref, *, sub):
    for r in range(a_ref.shape[0] // sub):
        rows = slice(r * sub, (r + 1) * sub)
        ya = _dot(a_ref[rows, :], wa_ref[...])
        yb = _dot(b_ref[rows, :], wb_ref[...])
        merged = (ga_ref[rows, :] * ya + gb_ref[rows, :] * yb).astype(BF16)
        x1 = x_ref[rows, :] + _dot(merged, wo_ref[...])
        x1_ref[rows, :] = x1
        h = x1 * lax.rsqrt(jnp.mean(x1 * x1, axis=-1, keepdims=True) + NORM_EPS) * nw_ref[...]
        h_ref[rows, :] = h.astype(h_ref.dtype)


def _mix(ya_in, yb_in, gates, wa, wb, wo, x2d, nw, tm=512, sub=512):
    m, k = ya_in.shape
    n = wo.shape[1]
    return pl.pallas_call(
        functools.partial(_mix_kernel, sub=sub),
        grid=(m // tm,),
        in_specs=[pl.BlockSpec((tm, k), lambda i: (i, 0)),
                  pl.BlockSpec((tm, k), lambda i: (i, 0)),
                  pl.BlockSpec((tm, n), lambda i: (i, 0)),
                  pl.BlockSpec((tm, n), lambda i: (i, 1)),
                  pl.BlockSpec((k, n), lambda i: (0, 0)),
                  pl.BlockSpec((k, n), lambda i: (0, 0)),
                  pl.BlockSpec((n, n), lambda i: (0, 0)),
                  pl.BlockSpec((tm, n), lambda i: (i, 0)),
                  pl.BlockSpec((1, n), lambda i: (0, 0))],
        out_specs=[pl.BlockSpec((tm, n), lambda i: (i, 0)),
                   pl.BlockSpec((tm, n), lambda i: (i, 0))],
        out_shape=[jax.ShapeDtypeStruct((m, n), F32),
                   jax.ShapeDtypeStruct((m, n), BF16)],
        compiler_params=_cparams("arbitrary"),
        name="branch_mix_out_proj_norm2",
    )(ya_in, yb_in, gates, gates, wa, wb, wo, x2d, nw.reshape(1, n))


def _upproj_kernel(h_ref, wg_ref, wv_ref, cg_ref, cv_ref, bg_ref, bv_ref, o_ref,
                   w_ref, carry_ref, *, tiles_per_seq, sub):
    m = pl.program_id(1)
    tn = wg_ref.shape[1]

    @pl.when(m == 0)
    def _():
        w_ref[:, :tn] = wg_ref[...].astype(BF16)
        w_ref[:, tn:] = wv_ref[...].astype(BF16)

    @pl.when(m % tiles_per_seq == 0)
    def _():
        carry_ref[...] = jnp.zeros_like(carry_ref)

    cw = jnp.concatenate([cg_ref[...], cv_ref[...]], axis=1)
    bias = jnp.concatenate([bg_ref[...], bv_ref[...]], axis=1)
    prev = carry_ref[...]
    for r in range(h_ref.shape[0] // sub):
        u = _dot(h_ref[r * sub:(r + 1) * sub, :], w_ref[...])
        y = _causal_conv(u, prev, cw, FFN_CONV) + bias
        prev = u[sub - V7X_SUBLANES:sub, :]
        o_ref[r * sub:(r + 1) * sub, :] = (_silu(y[:, :tn]) * y[:, tn:]).astype(o_ref.dtype)
    carry_ref[...] = prev


def _upproj(h, w_up, conv_w, conv_b, seq, tm=1024, tn=512, sub=1024):
    m, k = h.shape
    nb = D_FF // tn
    return pl.pallas_call(
        functools.partial(_upproj_kernel, tiles_per_seq=seq // tm, sub=sub),
        grid=(nb, m // tm),
        in_specs=[pl.BlockSpec((tm, k), lambda j, i: (i, 0)),
                  pl.BlockSpec((k, tn), lambda j, i: (0, j)),
                  pl.BlockSpec((k, tn), lambda j, i: (0, nb + j)),
                  pl.BlockSpec((FFN_CONV, tn), lambda j, i: (0, j)),
                  pl.BlockSpec((FFN_CONV, tn), lambda j, i: (0, nb + j)),
                  pl.BlockSpec((1, tn), lambda j, i: (0, j)),
                  pl.BlockSpec((1, tn), lambda j, i: (0, nb + j))],
        out_specs=pl.BlockSpec((tm, tn), lambda j, i: (i, j)),
        out_shape=jax.ShapeDtypeStruct((m, D_FF), BF16),
        scratch_shapes=[pltpu.VMEM((k, 2 * tn), BF16),
                        pltpu.VMEM((V7X_SUBLANES, 2 * tn), F32)],
        compiler_params=_cparams("arbitrary", "arbitrary"),
        name="up_proj_conv_gate",
    )(h, w_up, w_up, conv_w, conv_w, conv_b, conv_b)


def _downproj_kernel(a_ref, w_ref, x_ref, o_ref):
    o_ref[...] = x_ref[...] + _dot(a_ref[...], w_ref[...])


def _downproj(a, w, x1, tm=512, tn=1024):
    m, k = a.shape
    n = w.shape[1]
    return pl.pallas_call(
        _downproj_kernel,
        grid=(n // tn, m // tm),
        in_specs=[pl.BlockSpec((tm, k), lambda j, i: (i, 0)),
                  pl.BlockSpec((k, tn), lambda j, i: (0, j)),
                  pl.BlockSpec((tm, tn), lambda j, i: (i, j))],
        out_specs=pl.BlockSpec((tm, tn), lambda j, i: (i, j)),
        out_shape=jax.ShapeDtypeStruct((m, n), F32),
        compiler_params=_cparams("arbitrary", "arbitrary"),
        name="down_proj",
    )(a, w, x1)


def _downproj_norm_kernel(a_ref, w_ref, x_ref, nw_ref, o_ref):
    x2 = x_ref[...] + _dot(a_ref[...], w_ref[...])
    o_ref[...] = x2 * lax.rsqrt(jnp.mean(x2 * x2, axis=-1, keepdims=True) + NORM_EPS) * nw_ref[...]


def _downproj_norm(a, w, x1, nw, tm=512):
    m, k = a.shape
    n = w.shape[1]
    return pl.pallas_call(
        _downproj_norm_kernel,
        grid=(m // tm,),
        in_specs=[pl.BlockSpec((tm, k), lambda i: (i, 0)),
                  pl.BlockSpec((k, n), lambda i: (0, 0)),
                  pl.BlockSpec((tm, n), lambda i: (i, 0)),
                  pl.BlockSpec((1, n), lambda i: (0, 0))],
        out_specs=pl.BlockSpec((tm, n), lambda i: (i, 0)),
        out_shape=jax.ShapeDtypeStruct((m, n), F32),
        compiler_params=_cparams("arbitrary"),
        name="down_proj_final_norm",
    )(a, w, x1, nw.reshape(1, n))


def _pad_lanes(vec, offset):
    return jnp.zeros((1, V7X_LANES), F32).at[0, offset:offset + vec.shape[0]].set(vec.astype(F32))


def kernel(x, positions, ln1, w_in, gdn_conv, gdn_a_log, gdn_dt_bias, gdn_norm, w_branch_a, w_branch_b,
           w_out, ln2, w_up, ffn_conv, ffn_conv_bias, w_down, final_norm):
    batch, seq, d = x.shape
    m = batch * seq
    assert d == D_MODEL and seq % 1024 == 0 and seq % MOBA_BLOCK == 0, (x.shape,)
    assert w_in.shape[1:] == (D_MODEL, 3 * GDN_WIDTH + 2 * GDN_HEADS + GDN_WIDTH + 3 * MOBA_WIDTH + 2 * D_MODEL), w_in.shape
    depth = ln1.shape[0]
    o_qkv_a = 3 * GDN_WIDTH
    o_small = o_qkv_a + 2 * GDN_HEADS
    o_z = o_small + GDN_WIDTH
    o_qkv_b = o_z + 3 * MOBA_WIDTH
    inv_freq = (ROPE_THETA ** (-jnp.arange(ROPE_HALF, dtype=F32) / ROPE_HALF)).reshape(ROPE_HALF, 1)

    x2d = x.reshape(m, d)
    for l in range(depth):
        wt = jnp.swapaxes(w_in[l], 0, 1)
        wt_zs = jnp.concatenate(
            [wt[o_small:o_z], jnp.pad(wt[o_qkv_a:o_small], ((0, V7X_LANES - 2 * GDN_HEADS), (0, 0)))], axis=0)

        h1 = _norm_cast(x2d, ln1[l])
        qk_a = _gdn_proj(h1, wt, gdn_conv[l], seq, first=0, parts=2, l2norm=True, name="gdn_qk_proj")
        v_a = _gdn_proj(h1, wt, gdn_conv[l], seq, first=2, parts=1, l2norm=False, name="gdn_v_proj")
        zs = _proj_t(h1, wt_zs, 0, GDN_WIDTH + V7X_LANES, act=None, out_dtype=F32, tm=1024,
                     tn=GDN_WIDTH + V7X_LANES, name="z_beta_decay_proj")
        gates = _proj_t(h1, wt, o_qkv_b, 2 * D_MODEL, act="sigmoid", out_dtype=BF16, tm=1024, tn=1024, name="gate_proj")
        qkv_b_t = _moba_proj(h1, wt, o_z, positions, inv_freq, batch, seq)

        gdn_out = _gdn(qk_a, v_a, zs, _pad_lanes(gdn_a_log[l], GDN_HEADS), _pad_lanes(gdn_dt_bias[l], GDN_HEADS),
                       gdn_norm[l].reshape(1, HEAD_DIM), batch, seq)
        attn = _moba(qkv_b_t, batch, seq)
        x1, h2 = _mix(gdn_out, attn, gates, w_branch_a[l].astype(BF16), w_branch_b[l].astype(BF16),
                      w_out[l].astype(BF16), x2d, ln2[l])
        act = _upproj(h2, w_up[l], ffn_conv[l], ffn_conv_bias[l].reshape(1, 2 * D_FF), seq)
        if l == depth - 1:
            return _downproj_norm(act, w_down[l].astype(BF16), x1, final_norm).reshape(batch, seq, d)
        x2d = _downproj(act, w_down[l].astype(BF16), x1)
    return _final_norm(x2d, final_norm).reshape(batch, seq, d)
```

```python
import functools
import math

import jax
import jax.numpy as jnp
import numpy as np
from jax import lax
from jax.experimental import pallas as pl
from jax.experimental.pallas import tpu as pltpu

F32 = jnp.float32
BF16 = jnp.bfloat16

D_MODEL = 2048
GDN_HEADS = 8
HEAD_DIM = 128
GDN_WIDTH = GDN_HEADS * HEAD_DIM
GDN_CONV = 4
GDN_CHUNK = 64
MOBA_HEADS = 8
MOBA_WIDTH = MOBA_HEADS * HEAD_DIM
MOBA_BLOCK = 256
MOBA_TOPK = 3
ROPE_THETA = 500000.0
ROPE_DIM = HEAD_DIM // 4
ROPE_HALF = ROPE_DIM // 2
D_FF = 5632
FFN_CONV = 3
NORM_EPS = 1e-6
L2_EPS = 1e-6
NEG_INF = -1e30
NEG_LOG2_E = -math.log2(math.e)

V7X_LANES = 128
V7X_SUBLANES = 8
V7X_VMEM_LIMIT_BYTES = 56 * 1024 * 1024

GDN_GROUP = 4 * GDN_CHUNK
MOBA_Q_SCALE = HEAD_DIM ** -0.5 * math.log2(math.e)


def _cparams(*sem):
    return pltpu.CompilerParams(dimension_semantics=sem, vmem_limit_bytes=V7X_VMEM_LIMIT_BYTES)


def _dot(a, b):
    return jnp.dot(a, b, preferred_element_type=F32)


def _dot_nt(a, b):
    return lax.dot_general(a, b, (((1,), (1,)), ((), ())), preferred_element_type=F32)


def _dot_hi(a, b):
    return jnp.dot(a, b, preferred_element_type=F32, precision=lax.Precision.HIGHEST)


def _sigmoid(x):
    return 1.0 / (1.0 + jnp.exp2(x * NEG_LOG2_E))


def _silu(x):
    return x * _sigmoid(x)


def _shift_rows(cur, prev8, s):
    if s == 0:
        return cur
    rolled = pltpu.roll(cur, s, axis=0)
    rolled_prev = pltpu.roll(prev8, s, axis=0)
    row = lax.broadcasted_iota(jnp.int32, prev8.shape, 0)
    first = jnp.where(row < s, rolled_prev, rolled[0:V7X_SUBLANES])
    return jnp.concatenate([first, rolled[V7X_SUBLANES:]], axis=0)


def _causal_conv(cur, prev8, cw, width):
    y = cw[width - 1:width, :] * cur
    for j in range(width - 1):
        y = y + cw[j:j + 1, :] * _shift_rows(cur, prev8, width - 1 - j)
    return y


def _norm_kernel(x_ref, w_ref, h_ref):
    x = x_ref[...]
    y = x * lax.rsqrt(jnp.mean(x * x, axis=-1, keepdims=True) + NORM_EPS) * w_ref[...]
    h_ref[...] = y.astype(h_ref.dtype)


def _norm_cast(x2d, w, tm=1024):
    m, d = x2d.shape
    return pl.pallas_call(
        _norm_kernel,
        grid=(m // tm,),
        in_specs=[pl.BlockSpec((tm, d), lambda i: (i, 0)),
                  pl.BlockSpec((1, d), lambda i: (0, 0))],
        out_specs=pl.BlockSpec((tm, d), lambda i: (i, 0)),
        out_shape=jax.ShapeDtypeStruct((m, d), BF16),
        compiler_params=_cparams("arbitrary"),
        name="norm1",
    )(x2d, w.reshape(1, d))


def _final_norm_kernel(x_ref, w_ref, o_ref):
    x = x_ref[...]
    o_ref[...] = x * lax.rsqrt(jnp.mean(x * x, axis=-1, keepdims=True) + NORM_EPS) * w_ref[...]


def _final_norm(x2d, w, tm=512):
    m, d = x2d.shape
    return pl.pallas_call(
        _final_norm_kernel,
        grid=(m // tm,),
        in_specs=[pl.BlockSpec((tm, d), lambda i: (i, 0)),
                  pl.BlockSpec((1, d), lambda i: (0, 0))],
        out_specs=pl.BlockSpec((tm, d), lambda i: (i, 0)),
        out_shape=jax.ShapeDtypeStruct((m, d), F32),
        compiler_params=_cparams("arbitrary"),
        name="final_norm",
    )(x2d, w.reshape(1, d))


def _proj_t_kernel(h_ref, wt_ref, o_ref, w16_ref, *, act):
    @pl.when(pl.program_id(1) == 0)
    def _():
        w16_ref[...] = wt_ref[...].T.astype(BF16)

    acc = _dot(h_ref[...], w16_ref[...])
    if act == "sigmoid":
        acc = _sigmoid(acc)
    o_ref[...] = acc.astype(o_ref.dtype)


def _proj_t(h, wt, row0, n, *, act, out_dtype, tm, tn, name):
    m, k = h.shape
    return pl.pallas_call(
        functools.partial(_proj_t_kernel, act=act),
        grid=(n // tn, m // tm),
        in_specs=[pl.BlockSpec((tm, k), lambda j, i: (i, 0)),
                  pl.BlockSpec((pl.Element(tn), pl.Element(k)), lambda j, i: (pl.multiple_of(row0 + j * tn, 16), 0))],
        out_specs=pl.BlockSpec((tm, tn), lambda j, i: (i, j)),
        out_shape=jax.ShapeDtypeStruct((m, n), out_dtype),
        scratch_shapes=[pltpu.VMEM((k, tn), BF16)],
        compiler_params=_cparams("arbitrary", "arbitrary"),
        name=name,
    )(h, wt)


def _gdn_proj_kernel(h_ref, w_ref, cw_ref, o_ref, w16_ref, carry_ref, *, tiles_per_seq, tiles_per_part, sub, l2norm):
    n = pl.program_id(0)
    m = pl.program_id(1)
    tn = w_ref.shape[0]

    @pl.when(m == 0)
    def _():
        w16_ref[...] = w_ref[...].T.astype(BF16)

    @pl.when(m % tiles_per_seq == 0)
    def _():
        carry_ref[...] = jnp.zeros_like(carry_ref)

    part = n // tiles_per_part
    q_scale = jnp.where(part == 0, HEAD_DIM ** -0.5, 1.0).astype(F32)
    prev = carry_ref[...]
    for r in range(h_ref.shape[0] // sub):
        acc = _dot(h_ref[r * sub:(r + 1) * sub, :], w16_ref[...])
        y = _silu(_causal_conv(acc, prev, cw_ref[...], GDN_CONV))
        prev = acc[sub - V7X_SUBLANES:sub, :]
        if l2norm:
            heads = []
            for hd in range(tn // HEAD_DIM):
                blk = y[:, hd * HEAD_DIM:(hd + 1) * HEAD_DIM]
                heads.append(blk * (lax.rsqrt(jnp.sum(blk * blk, axis=-1, keepdims=True) + L2_EPS) * q_scale))
            y = jnp.concatenate(heads, axis=1)
        o_ref[r * sub:(r + 1) * sub, :] = y
    carry_ref[...] = prev


def _gdn_proj(h, w_in, conv_w, seq, *, first, parts, l2norm, name, tm=1024, tn=1024, sub=1024):
    m, k = h.shape
    n = parts * GDN_WIDTH
    off = first * GDN_WIDTH // tn
    return pl.pallas_call(
        functools.partial(_gdn_proj_kernel, tiles_per_seq=seq // tm, tiles_per_part=GDN_WIDTH // tn, sub=sub,
                          l2norm=l2norm),
        grid=(n // tn, m // tm),
        in_specs=[pl.BlockSpec((tm, k), lambda j, i: (i, 0)),
                  pl.BlockSpec((tn, k), lambda j, i: (off + j, 0)),
                  pl.BlockSpec((GDN_CONV, tn), lambda j, i: (0, off + j))],
        out_specs=pl.BlockSpec((tm, tn), lambda j, i: (i, j)),
        out_shape=jax.ShapeDtypeStruct((m, n), F32),
        scratch_shapes=[pltpu.VMEM((k, tn), BF16),
                        pltpu.VMEM((V7X_SUBLANES, tn), F32)],
        compiler_params=_cparams("arbitrary", "arbitrary"),
        name=name,
    )(h, w_in, conv_w)


def _moba_proj_kernel(h_ref, wt_ref, pos_ref, freq_ref, o_ref, w16_ref):
    part = pl.program_id(0)

    @pl.when(pl.program_id(1) == 0)
    def _():
        w16_ref[...] = wt_ref[...].astype(BF16)

    scale = jnp.where(part == 0, MOBA_Q_SCALE, 1.0).astype(F32)
    acc = _dot_nt(w16_ref[...], h_ref[...]) * scale
    ang = freq_ref[...] * pos_ref[...].astype(F32)
    roped = part < 2
    cos = jnp.where(roped, jnp.cos(ang), 1.0)
    sin = jnp.where(roped, jnp.sin(ang), 0.0)
    rows = []
    for hd in range(acc.shape[0] // HEAD_DIM):
        base = hd * HEAD_DIM
        x1 = acc[base:base + ROPE_HALF]
        x2 = acc[base + ROPE_HALF:base + ROPE_DIM]
        rows.append(x1 * cos - x2 * sin)
        rows.append(x2 * cos + x1 * sin)
        rows.append(acc[base + ROPE_DIM:base + HEAD_DIM])
    o_ref[...] = jnp.concatenate(rows, axis=0).astype(o_ref.dtype)


def _moba_proj(h, wt, row0, positions, inv_freq, batch, seq, tt=2048):
    m, k = h.shape
    tr = MOBA_WIDTH
    n = 3 * MOBA_WIDTH
    per_seq = seq // tt
    return pl.pallas_call(
        _moba_proj_kernel,
        grid=(n // tr, m // tt),
        in_specs=[pl.BlockSpec((tt, k), lambda r, i: (i, 0)),
                  pl.BlockSpec((pl.Element(tr), pl.Element(k)), lambda r, i: (pl.multiple_of(row0 + r * tr, 16), 0)),
                  pl.BlockSpec((None, 1, tt), lambda r, i: (i // per_seq, 0, i % per_seq)),
                  pl.BlockSpec((ROPE_HALF, 1), lambda r, i: (0, 0))],
        out_specs=pl.BlockSpec((None, tr, tt), lambda r, i: (i // per_seq, r, i % per_seq)),
        out_shape=jax.ShapeDtypeStruct((batch, n, seq), BF16),
        scratch_shapes=[pltpu.VMEM((tr, k), BF16)],
        compiler_params=_cparams("arbitrary", "arbitrary"),
        name="moba_qkv_proj",
    )(h, wt, positions.reshape(batch, 1, seq), inv_freq)


def _gdn_kernel(q_ref, k_ref, v_ref, z_ref, sm_ref, a_ref, dt_ref, nw_ref, o_ref,
                state_ref, lbd_ref, subd_ref, *, groups, heads_per_step):
    t = pl.program_id(2)
    g = GDN_GROUP
    c = GDN_CHUNK

    @pl.when(t == 0)
    def _():
        state_ref[...] = jnp.zeros_like(state_ref)

    ri = lax.broadcasted_iota(jnp.int32, (g, g), 0)
    ci = lax.broadcasted_iota(jnp.int32, (g, g), 1)
    shift = int(math.log2(c))
    same = jnp.right_shift(ri, shift) == jnp.right_shift(ci, shift)
    lbd_ref[...] = jnp.where(same & (ci <= ri), 1.0, 0.0)
    subd_ref[...] = jnp.where(same & (ri > ci), 1.0, 0.0)
    lane = lax.broadcasted_iota(jnp.int32, (g, V7X_LANES), 1)
    eye = jnp.where(ri == ci, 1.0, 0.0)

    def group_body(gi, carry):
        hs = range(heads_per_step)
        nch = g // c
        r0 = pl.multiple_of(gi * g, g)
        sm = sm_ref[pl.ds(r0, g), :]
        beta_all = _sigmoid(sm)
        xs = sm + dt_ref[...]
        softplus = jnp.maximum(xs, 0.0) + jnp.log1p(jnp.exp(-jnp.abs(xs)))
        g_all = -jnp.exp(a_ref[...]) * softplus
        lbd = lbd_ref[...]
        strict = subd_ref[...]
        cols = [slice(hh * HEAD_DIM, (hh + 1) * HEAD_DIM) for hh in hs]
        heads = [pl.program_id(1) * heads_per_step + hh for hh in hs]

        k = [k_ref[pl.ds(r0, g), cols[hh]] for hh in hs]
        k16 = [k[hh].astype(BF16) for hh in hs]
        beta_b = [jnp.broadcast_to(jnp.sum(jnp.where(lane == heads[hh], beta_all, 0.0), axis=-1, keepdims=True),
                                   (g, HEAD_DIM)) for hh in hs]
        g_b = [jnp.broadcast_to(jnp.sum(jnp.where(lane == GDN_HEADS + heads[hh], g_all, 0.0), axis=-1, keepdims=True),
                                (g, HEAD_DIM)) for hh in hs]
        kb = [k[hh] * beta_b[hh] for hh in hs]

        pos = jnp.bitwise_and(lax.broadcasted_iota(jnp.int32, (g, HEAD_DIM), 0), c - 1)
        gc_b = []
        for hh in hs:
            acc = g_b[hh]
            step = 1
            while step < c:
                acc = acc + jnp.where(pos >= step, pltpu.roll(acc, step, axis=0), 0.0)
                step *= 2
            gc_b.append(acc)
        kq = [_dot_nt(jnp.concatenate([kb[hh].astype(BF16), q_ref[pl.ds(r0, g), cols[hh]].astype(BF16)], axis=0), k16[hh])
              for hh in hs]
        kk = [kq[hh][:g] for hh in hs]
        qk_raw = [kq[hh][g:] for hh in hs]
        gamma = []
        for hh in hs:
            gc_row = gc_b[hh].T[0:1, :]
            dmat = jnp.where(lbd > 0.0, jnp.concatenate([gc_b[hh], gc_b[hh]], axis=1) - gc_row, 0.0)
            gamma.append(jnp.exp(dmat))
        eg = [jnp.exp(gc_b[hh]) for hh in hs]

        p = [-jnp.where(strict > 0.0, kk[hh] * gamma[hh], 0.0) for hh in hs]
        t_mat = [eye + p[hh] for hh in hs]
        p16 = [p[hh].astype(BF16) for hh in hs]
        p16 = [_dot(p16[hh], p16[hh]).astype(BF16) for hh in hs]
        for _ in range(4):
            st = [_dot(jnp.concatenate([p16[hh], t_mat[hh].astype(BF16)], axis=0), p16[hh]) for hh in hs]
            p16 = [st[hh][:g].astype(BF16) for hh in hs]
            t_mat = [t_mat[hh] + st[hh][g:] for hh in hs]
        tp = [_dot(t_mat[hh].astype(BF16), p16[hh]) for hh in hs]
        t_mat = [t_mat[hh] + tp[hh] for hh in hs]
        uw = [_dot(t_mat[hh].astype(BF16),
                   jnp.concatenate([v_ref[pl.ds(r0, g), cols[hh]] * beta_b[hh], kb[hh] * eg[hh]], axis=1).astype(BF16))
              for hh in hs]
        u = [uw[hh][:, :HEAD_DIM] for hh in hs]
        w16 = [uw[hh][:, HEAD_DIM:].astype(BF16) for hh in hs]
        qk16 = [jnp.where(lbd > 0.0, qk_raw[hh] * gamma[hh], 0.0).astype(BF16) for hh in hs]
        qg = [(q_ref[pl.ds(r0, g), cols[hh]] * eg[hh]).astype(BF16) for hh in hs]
        kd_t = []
        for hh in hs:
            gc_last = jnp.concatenate(
                [jnp.broadcast_to(gc_b[hh][(ch + 1) * c - 1:(ch + 1) * c, :], (c, HEAD_DIM)) for ch in range(nch)],
                axis=0)
            kd_t.append((k[hh] * jnp.exp(gc_last - gc_b[hh])).T.astype(BF16))

        state = [state_ref[hh] for hh in hs]
        outs = [[] for _ in hs]
        zeros_c = jnp.zeros((c, HEAD_DIM), F32)
        for ch in range(nch):
            rows = slice(ch * c, (ch + 1) * c)
            s16 = [state[hh].astype(BF16) for hh in hs]
            wq = [_dot(jnp.concatenate([w16[hh][rows], qg[hh][rows]], axis=0), s16[hh]) for hh in hs]
            ws = [wq[hh][:c] for hh in hs]
            qs = [wq[hh][c:] for hh in hs]
            vn_all = [jnp.concatenate([zeros_c] * ch + [u[hh][rows] - ws[hh]] + [zeros_c] * (nch - 1 - ch),
                                      axis=0).astype(BF16) for hh in hs]
            so = [_dot(jnp.concatenate([kd_t[hh], qk16[hh][rows]], axis=0), vn_all[hh]) for hh in hs]
            sv = [so[hh][:HEAD_DIM] for hh in hs]
            ov = [so[hh][HEAD_DIM:] for hh in hs]
            for hh in hs:
                outs[hh].append(qs[hh] + ov[hh])
                state[hh] = state[hh] * eg[hh][(ch + 1) * c - 1:(ch + 1) * c, :] + sv[hh]
        for hh in hs:
            state_ref[hh] = state[hh]
            o = jnp.concatenate(outs[hh], axis=0)
            o = o * lax.rsqrt(jnp.mean(o * o, axis=-1, keepdims=True) + NORM_EPS) * nw_ref[...]
            o_ref[pl.ds(r0, g), cols[hh]] = (o * _silu(z_ref[pl.ds(r0, g), cols[hh]])).astype(o_ref.dtype)
        return carry

    lax.fori_loop(0, groups, group_body, 0)


def _gdn(qk, v, zs, a_pad, dt_pad, norm_w, batch, seq, ts=512, heads_per_step=8):
    m = qk.shape[0]
    tiles = seq // ts
    hp = heads_per_step
    hb = GDN_HEADS // hp
    wd = hp * HEAD_DIM
    row = lambda b, h, t: b * tiles + t
    return pl.pallas_call(
        functools.partial(_gdn_kernel, groups=ts // GDN_GROUP, heads_per_step=hp),
        grid=(batch, hb, tiles),
        in_specs=[pl.BlockSpec((ts, wd), lambda b, h, t: (row(b, h, t), h)),
                  pl.BlockSpec((ts, wd), lambda b, h, t: (row(b, h, t), hb + h)),
                  pl.BlockSpec((ts, wd), lambda b, h, t: (row(b, h, t), h)),
                  pl.BlockSpec((ts, wd), lambda b, h, t: (row(b, h, t), h)),
                  pl.BlockSpec((ts, V7X_LANES), lambda b, h, t: (row(b, h, t), GDN_WIDTH // V7X_LANES)),
                  pl.BlockSpec((1, V7X_LANES), lambda b, h, t: (0, 0)),
                  pl.BlockSpec((1, V7X_LANES), lambda b, h, t: (0, 0)),
                  pl.BlockSpec((1, HEAD_DIM), lambda b, h, t: (0, 0))],
        out_specs=pl.BlockSpec((ts, wd), lambda b, h, t: (row(b, h, t), h)),
        out_shape=jax.ShapeDtypeStruct((m, GDN_WIDTH), BF16),
        scratch_shapes=[pltpu.VMEM((hp, HEAD_DIM, HEAD_DIM), F32),
                        pltpu.VMEM((GDN_GROUP, GDN_GROUP), F32),
                        pltpu.VMEM((GDN_GROUP, GDN_GROUP), F32)],
        compiler_params=_cparams("arbitrary", "arbitrary", "arbitrary"),
        name="gated_delta_rule",
    )(qk, qk, v, zs, zs, a_pad, dt_pad, norm_w)


def _moba_kernel(q_ref, k_ref, v_ref, o_ref, kn_ref, km_ref, s_ref, p_ref, bias_ref, *, nblk, heads_per_step):
    blk = MOBA_BLOCK
    hs = range(heads_per_step)
    drows = [slice(hh * HEAD_DIM, (hh + 1) * HEAD_DIM) for hh in hs]
    for hh in hs:
        for j in range(nblk):
            kt = k_ref[drows[hh], j * blk:(j + 1) * blk].astype(F32)
            kn = kt.T
            kn_ref[hh, j * blk:(j + 1) * blk, 0:HEAD_DIM] = kn.astype(BF16)
            kn_ref[hh, j * blk:(j + 1) * blk, HEAD_DIM:] = jnp.where(
                lax.broadcasted_iota(jnp.int32, (blk, HEAD_DIM), 1) == j, 1.0, 0.0).astype(BF16)
            km_ref[hh, j:j + 1, :] = jnp.sum(kn, axis=0, keepdims=True) * (1.0 / blk)

    bias_ref[...] = jnp.zeros_like(bias_ref)

    key_i = lax.broadcasted_iota(jnp.int32, (blk, blk), 0)
    qry_i = lax.broadcasted_iota(jnp.int32, (blk, blk), 1)
    causal_bias = jnp.where(key_i <= qry_i, 0.0, NEG_INF)
    groups = blk // V7X_SUBLANES

    def scores(hh, i):
        s_buf = s_ref.at[hh, i % 2]
        qt = q_ref[drows[hh], i * blk:(i + 1) * blk]
        if i > MOBA_TOPK:
            gate = _dot_hi(km_ref[hh], qt.astype(F32))
            n_iota = lax.broadcasted_iota(jnp.int32, (nblk, blk), 0)
            valid = n_iota < i
            for n in range(i):
                g_n = gate[n:n + 1, :]
                beats = valid & ((gate > g_n) | ((gate == g_n) & (n_iota < n)))
                cnt = jnp.sum(jnp.where(beats, 1.0, 0.0), axis=0, keepdims=True)
                bias_ref[hh, n:n + 1, :] = jnp.where(cnt < float(MOBA_TOPK), 0.0, NEG_INF)
            bias_ref[hh, i:i + 1, :] = jnp.zeros((1, blk), F32)
            qa = jnp.concatenate([qt, bias_ref[hh].astype(BF16),
                                  jnp.zeros((HEAD_DIM - nblk, blk), BF16)], axis=0)
        m_acc = None
        for j in range(i + 1):
            if i > MOBA_TOPK:
                s = _dot(kn_ref[hh, j * blk:(j + 1) * blk, :], qa)
            else:
                s = _dot(kn_ref[hh, j * blk:(j + 1) * blk, 0:HEAD_DIM], qt)
            if j == i:
                s = s + causal_bias
            s_buf[j * blk:(j + 1) * blk, :] = s
            part = jnp.max(s.reshape(groups, V7X_SUBLANES, blk), axis=0)
            m_acc = part if m_acc is None else jnp.maximum(m_acc, part)
        return jnp.max(m_acc, axis=0, keepdims=True)

    def softmax_weights(hh, i, m_row):
        s_buf = s_ref.at[hh, i % 2]
        p_buf = p_ref.at[hh, i % 2]
        l_acc = None
        for j in range(i + 1):
            p = jnp.exp2(s_buf[j * blk:(j + 1) * blk, :] - m_row)
            part = jnp.sum(p.reshape(groups, V7X_SUBLANES, blk), axis=0)
            l_acc = part if l_acc is None else l_acc + part
            p_buf[j * blk:(j + 1) * blk, :] = p.astype(BF16)
        return jnp.sum(l_acc, axis=0, keepdims=True)

    def weighted_values(hh, i, l_row):
        p_buf = p_ref.at[hh, i % 2]
        o_t = _dot(v_ref[drows[hh], 0:(i + 1) * blk], p_buf[0:(i + 1) * blk, :]) / l_row
        o_ref[i * blk:(i + 1) * blk, drows[hh]] = o_t.T.astype(o_ref.dtype)

    m_row = {(hh, 0): scores(hh, 0) for hh in hs}
    l_row = {}
    for i in range(nblk):
        for hh in hs:
            if i + 1 < nblk:
                m_row[hh, i + 1] = scores(hh, i + 1)
        for hh in hs:
            if i >= 1:
                weighted_values(hh, i - 1, l_row.pop((hh, i - 1)))
        for hh in hs:
            l_row[hh, i] = softmax_weights(hh, i, m_row.pop((hh, i)))
    for hh in hs:
        weighted_values(hh, nblk - 1, l_row.pop((hh, nblk - 1)))


def _moba(qkv_t, batch, seq, heads_per_step=2):
    nblk = seq // MOBA_BLOCK
    hp = heads_per_step
    hb = MOBA_HEADS // hp
    wd = hp * HEAD_DIM
    return pl.pallas_call(
        functools.partial(_moba_kernel, nblk=nblk, heads_per_step=hp),
        grid=(batch, hb),
        in_specs=[pl.BlockSpec((None, wd, seq), lambda b, h: (b, h, 0)),
                  pl.BlockSpec((None, wd, seq), lambda b, h: (b, hb + h, 0)),
                  pl.BlockSpec((None, wd, seq), lambda b, h: (b, 2 * hb + h, 0))],
        out_specs=pl.BlockSpec((seq, wd), lambda b, h: (b, h)),
        out_shape=jax.ShapeDtypeStruct((batch * seq, MOBA_WIDTH), BF16),
        scratch_shapes=[pltpu.VMEM((hp, seq, 2 * HEAD_DIM), BF16),
                        pltpu.VMEM((hp, nblk, HEAD_DIM), F32),
                        pltpu.VMEM((hp, 2, seq, MOBA_BLOCK), F32),
                        pltpu.VMEM((hp, 2, seq, MOBA_BLOCK), BF16),
                        pltpu.VMEM((hp, nblk, MOBA_BLOCK), F32)],
        compiler_params=_cparams("arbitrary", "arbitrary"),
        name="moba_attention",
    )(qkv_t, qkv_t, qkv_t)


def _mix_kernel(a_ref, b_ref, ga_ref, gb_ref, wa_ref, wb_ref, wo_ref, x_ref, nw_ref, x1_ref, h_ref, *, sub):
    for r in range(a_ref.shape[0] // sub):
        rows = slice(r * sub, (r + 1) * sub)
        ya = _dot(a_ref[rows, :], wa_ref[...])
        yb = _dot(b_ref[rows, :], wb_ref[...])
        merged = (ga_ref[rows, :] * ya + gb_ref[rows, :] * yb).astype(BF16)
        x1 = x_ref[rows, :] + _dot(merged, wo_ref[...])
        x1_ref[rows, :] = x1
        h = x1 * lax.rsqrt(jnp.mean(x1 * x1, axis=-1, keepdims=True) + NORM_EPS) * nw_ref[...]
        h_ref[rows, :] = h.astype(h_ref.dtype)


def _mix(ya_in, yb_in, gates, wa, wb, wo, x2d, nw, tm=512, sub=512):
    m, k = ya_in.shape
    n = wo.shape[1]
    return pl.pallas_call(
        functools.partial(_mix_kernel, sub=sub),
        grid=(m // tm,),
        in_specs=[pl.BlockSpec((tm, k), lambda i: (i, 0)),
                  pl.BlockSpec((tm, k), lambda i: (i, 0)),
                  pl.BlockSpec((tm, n), lambda i: (i, 0)),
                  pl.BlockSpec((tm, n), lambda i: (i, 1)),
                  pl.BlockSpec((k, n), lambda i: (0, 0)),
                  pl.BlockSpec((k, n), lambda i: (0, 0)),
                  pl.BlockSpec((n, n), lambda i: (0, 0)),
                  pl.BlockSpec((tm, n), lambda i: (i, 0)),
                  pl.BlockSpec((1, n), lambda i: (0, 0))],
        out_specs=[pl.BlockSpec((tm, n), lambda i: (i, 0)),
                   pl.BlockSpec((tm, n), lambda i: (i, 0))],
        out_shape=[jax.ShapeDtypeStruct((m, n), F32),
                   jax.ShapeDtypeStruct((m, n), BF16)],
        compiler_params=_cparams("arbitrary"),
        name="branch_mix_out_proj_norm2",
    )(ya_in, yb_in, gates, gates, wa, wb, wo, x2d, nw.reshape(1, n))


def _upproj_kernel(h_ref, wg_ref, wv_ref, cg_ref, cv_ref, bg_ref, bv_ref, o_ref,
                   w_ref, carry_ref, *, tiles_per_seq, sub):
    m = pl.program_id(1)
    tn = wg_ref.shape[1]

    @pl.when(m == 0)
    def _():
        w_ref[:, :tn] = wg_ref[...].astype(BF16)
        w_ref[:, tn:] = wv_ref[...].astype(BF16)

    @pl.when(m % tiles_per_seq == 0)
    def _():
        carry_ref[...] = jnp.zeros_like(carry_ref)

    cw = jnp.concatenate([cg_ref[...], cv_ref[...]], axis=1)
    bias = jnp.concatenate([bg_ref[...], bv_ref[...]], axis=1)
    prev = carry_ref[...]
    for r in range(h_ref.shape[0] // sub):
        u = _dot(h_ref[r * sub:(r + 1) * sub, :], w_ref[...])
        y = _causal_conv(u, prev, cw, FFN_CONV) + bias
        prev = u[sub - V7X_SUBLANES:sub, :]
        o_ref[r * sub:(r + 1) * sub, :] = (_silu(y[:, :tn]) * y[:, tn:]).astype(o_ref.dtype)
    carry_ref[...] = prev


def _upproj(h, w_up, conv_w, conv_b, seq, tm=1024, tn=512, sub=1024):
    m, k = h.shape
    nb = D_FF // tn
    return pl.pallas_call(
        functools.partial(_upproj_kernel, tiles_per_seq=seq // tm, sub=sub),
        grid=(nb, m // tm),
        in_specs=[pl.BlockSpec((tm, k), lambda j, i: (i, 0)),
                  pl.BlockSpec((k, tn), lambda j, i: (0, j)),
                  pl.BlockSpec((k, tn), lambda j, i: (0, nb + j)),
                  pl.BlockSpec((FFN_CONV, tn), lambda j, i: (0, j)),
                  pl.BlockSpec((FFN_CONV, tn), lambda j, i: (0, nb + j)),
                  pl.BlockSpec((1, tn), lambda j, i: (0, j)),
                  pl.BlockSpec((1, tn), lambda j, i: (0, nb + j))],
        out_specs=pl.BlockSpec((tm, tn), lambda j, i: (i, j)),
        out_shape=jax.ShapeDtypeStruct((m, D_FF), BF16),
        scratch_shapes=[pltpu.VMEM((k, 2 * tn), BF16),
                        pltpu.VMEM((V7X_SUBLANES, 2 * tn), F32)],
        compiler_params=_cparams("arbitrary", "arbitrary"),
        name="up_proj_conv_gate",
    )(h, w_up, w_up, conv_w, conv_w, conv_b, conv_b)


def _downproj_kernel(a_ref, w_ref, x_ref, o_ref):
    o_ref[...] = x_ref[...] + _dot(a_ref[...], w_ref[...])


def _downproj(a, w, x1, tm=512, tn=1024):
    m, k = a.shape
    n = w.shape[1]
    return pl.pallas_call(
        _downproj_kernel,
        grid=(n // tn, m // tm),
        in_specs=[pl.BlockSpec((tm, k), lambda j, i: (i, 0)),
                  pl.BlockSpec((k, tn), lambda j, i: (0, j)),
                  pl.BlockSpec((tm, tn), lambda j, i: (i, j))],
        out_specs=pl.BlockSpec((tm, tn), lambda j, i: (i, j)),
        out_shape=jax.ShapeDtypeStruct((m, n), F32),
        compiler_params=_cparams("arbitrary", "arbitrary"),
        name="down_proj",
    )(a, w, x1)


def _downproj_norm_kernel(a_ref, w_ref, x_ref, nw_ref, o_ref):
    x2 = x_ref[...] + _dot(a_ref[...], w_ref[...])
    o_ref[...] = x2 * lax.rsqrt(jnp.mean(x2 * x2, axis=-1, keepdims=True) + NORM_EPS) * nw_ref[...]


def _downproj_norm(a, w, x1, nw, tm=512):
    m, k = a.shape
    n = w.shape[1]
    return pl.pallas_call(
        _downproj_norm_kernel,
        grid=(m // tm,),
        in_specs=[pl.BlockSpec((tm, k), lambda i: (i, 0)),
                  pl.BlockSpec((k, n), lambda i: (0, 0)),
                  pl.BlockSpec((tm, n), lambda i: (i, 0)),
                  pl.BlockSpec((1, n), lambda i: (0, 0))],
        out_specs=pl.BlockSpec((tm, n), lambda i: (i, 0)),
        out_shape=jax.ShapeDtypeStruct((m, n), F32),
        compiler_params=_cparams("arbitrary"),
        name="down_proj_final_norm",
    )(a, w, x1, nw.reshape(1, n))


def _pad_lanes(vec, offset):
    return jnp.zeros((1, V7X_LANES), F32).at[0, offset:offset + vec.shape[0]].set(vec.astype(F32))


def kernel(x, positions, ln1, w_in, gdn_conv, gdn_a_log, gdn_dt_bias, gdn_norm, w_branch_a, w_branch_b,
           w_out, ln2, w_up, ffn_conv, ffn_conv_bias, w_down, final_norm):
    batch, seq, d = x.shape
    m = batch * seq
    assert d == D_MODEL and seq % 1024 == 0 and seq % MOBA_BLOCK == 0, (x.shape,)
    assert w_in.shape[1:] == (D_MODEL, 3 * GDN_WIDTH + 2 * GDN_HEADS + GDN_WIDTH + 3 * MOBA_WIDTH + 2 * D_MODEL), w_in.shape
    depth = ln1.shape[0]
    o_qkv_a = 3 * GDN_WIDTH
    o_small = o_qkv_a + 2 * GDN_HEADS
    o_z = o_small + GDN_WIDTH
    o_qkv_b = o_z + 3 * MOBA_WIDTH
    inv_freq = (ROPE_THETA ** (-jnp.arange(ROPE_HALF, dtype=F32) / ROPE_HALF)).reshape(ROPE_HALF, 1)

    x2d = x.reshape(m, d)
    for l in range(depth):
        wt = jnp.swapaxes(w_in[l], 0, 1)
        wt_zs = jnp.concatenate(
            [wt[o_small:o_z], jnp.pad(wt[o_qkv_a:o_small], ((0, V7X_LANES - 2 * GDN_HEADS), (0, 0)))], axis=0)

        h1 = _norm_cast(x2d, ln1[l])
        qk_a = _gdn_proj(h1, wt, gdn_conv[l], seq, first=0, parts=2, l2norm=True, name="gdn_qk_proj")
        v_a = _gdn_proj(h1, wt, gdn_conv[l], seq, first=2, parts=1, l2norm=False, name="gdn_v_proj")
        zs = _proj_t(h1, wt_zs, 0, GDN_WIDTH + V7X_LANES, act=None, out_dtype=F32, tm=1024,
                     tn=GDN_WIDTH + V7X_LANES, name="z_beta_decay_proj")
        gates = _proj_t(h1, wt, o_qkv_b, 2 * D_MODEL, act="sigmoid", out_dtype=BF16, tm=1024, tn=1024, name="gate_proj")
        qkv_b_t = _moba_proj(h1, wt, o_z, positions, inv_freq, batch, seq)

        gdn_out = _gdn(qk_a, v_a, zs, _pad_lanes(gdn_a_log[l], GDN_HEADS), _pad_lanes(gdn_dt_bias[l], GDN_HEADS),
                       gdn_norm[l].reshape(1, HEAD_DIM), batch, seq)
        attn = _moba(qkv_b_t, batch, seq)
        x1, h2 = _mix(gdn_out, attn, gates, w_branch_a[l].astype(BF16), w_branch_b[l].astype(BF16),
                      w_out[l].astype(BF16), x2d, ln2[l])
        act = _upproj(h2, w_up[l], ffn_conv[l], ffn_conv_bias[l].reshape(1, 2 * D_FF), seq)
        if l == depth - 1:
            return _downproj_norm(act, w_down[l].astype(BF16), x1, final_norm).reshape(batch, seq, d)
        x2d = _downproj(act, w_down[l].astype(BF16), x1)
    return _final_norm(x2d, final_norm).reshape(batch, seq, d)
```
